```python
import math
import jax, jax.numpy as jnp
from jax import lax
import numpy as np

D_MODEL = 1024
BATCH = 4
SEQ = 4096
DEPTH = 4

N_A_LAYERS = DEPTH // 2
N_B_LAYERS = DEPTH - N_A_LAYERS

GLA_HEADS = 4
GLA_KEY_DIM = D_MODEL // 2
GLA_VAL_DIM = D_MODEL
GLA_HK = GLA_KEY_DIM // GLA_HEADS
GLA_HV = GLA_VAL_DIM // GLA_HEADS
GLA_GATE_RANK = 16
GLA_GATE_NORM = 16.0
GLA_CHUNK = 64
A_PROJ = 2 * GLA_KEY_DIM + 2 * GLA_VAL_DIM + GLA_GATE_RANK

SWA_HEAD_DIM = 64
SWA_Q_HEADS = D_MODEL // SWA_HEAD_DIM
SWA_GROUP = 8
SWA_KV_HEADS = SWA_Q_HEADS // SWA_GROUP
WINDOW = 128
BLOCK = 128
KV_PROJ = 2 * SWA_KV_HEADS * SWA_HEAD_DIM

REL_BUCKETS = 32
REL_MAX_DIST = 128

D_FF = 4 * D_MODEL
EPS = 1e-6
NEG = -1e30

kernel_name = "yoco_gla_swa_sink_hybrid"


def rmsnorm(x, g):
    xf = x.astype(jnp.float32)
    y = xf * lax.rsqrt(jnp.mean(xf * xf, axis=-1, keepdims=True) + EPS) * g.astype(jnp.float32)
    return y.astype(x.dtype)


def sq_relu_mlp(h, w_up, w_down):
    u = jax.nn.relu(h @ w_up)
    return (u * u) @ w_down


def gla_mixer(h, w_in, w_gk2, b_gk, g_onorm, w_out):
    B, T, _ = h.shape
    C = GLA_CHUNK
    NC = T // C
    f32 = jnp.float32
    proj = h @ w_in
    q, k, v, gate, glr = jnp.split(
        proj, [GLA_KEY_DIM, 2 * GLA_KEY_DIM, 2 * GLA_KEY_DIM + GLA_VAL_DIM,
               2 * GLA_KEY_DIM + 2 * GLA_VAL_DIM], axis=-1)
    gk = jax.nn.log_sigmoid((glr @ w_gk2 + b_gk).astype(f32)) / GLA_GATE_NORM

    def heads(t, d):
        return t.astype(f32).reshape(B, NC, C, GLA_HEADS, d).transpose(1, 0, 3, 2, 4)

    qf = heads(q, GLA_HK) * (GLA_HK ** -0.5)
    kf = heads(k, GLA_HK)
    vf = heads(v, GLA_HV)
    bcum = jnp.cumsum(heads(gk, GLA_HK), axis=-2)
    q_t = qf * jnp.exp(bcum)
    k_t = kf * jnp.exp(-bcum)
    causal = jnp.asarray(np.tril(np.ones((C, C), dtype=bool)))
    attn = jnp.where(causal, jnp.einsum('nbhik,nbhjk->nbhij', q_t, k_t), 0.0)
    o_intra = jnp.einsum('nbhij,nbhjv->nbhiv', attn, vf)
    b_last = bcum[..., -1:, :]
    k_dec = kf * jnp.exp(b_last - bcum)

    def step(S, xs):
        q_c, k_c, v_c, bl = xs
        o = jnp.einsum('bhik,bhkv->bhiv', q_c, S)
        S = S * jnp.exp(bl)[..., 0, :, None] + jnp.einsum('bhjk,bhjv->bhkv', k_c, v_c)
        return S, o

    S0 = jnp.zeros((B, GLA_HEADS, GLA_HK, GLA_HV), f32)
    _, o_inter = lax.scan(step, S0, (q_t, k_dec, vf, b_last))
    o = (o_intra + o_inter).transpose(1, 0, 3, 2, 4).reshape(B, T, GLA_HEADS, GLA_HV)
    o = o * lax.rsqrt(jnp.mean(o * o, axis=-1, keepdims=True) + EPS) * g_onorm.astype(f32)
    o = o * jax.nn.silu(gate.astype(f32).reshape(B, T, GLA_HEADS, GLA_HV))
    return o.reshape(B, T, GLA_VAL_DIM).astype(h.dtype) @ w_out


def rel_bucket_band():
    i = np.arange(BLOCK)[:, None]
    j = np.arange(2 * BLOCK)[None, :]
    dist = i + BLOCK - j
    n = np.maximum(dist, 0)
    max_exact = REL_BUCKETS // 2
    large = max_exact + (np.log(np.maximum(n, 1) / max_exact)
                         / np.log(REL_MAX_DIST / max_exact)
                         * (REL_BUCKETS - max_exact)).astype(np.int32)
    large = np.minimum(large, REL_BUCKETS - 1)
    bucket = np.where(n < max_exact, n, large).astype(np.int32)
    valid = (dist >= 0) & (dist < WINDOW)
    return bucket, valid


def shared_kv(h, w_kv):
    B, T, _ = h.shape
    NB = T // BLOCK
    k, v = jnp.split(h @ w_kv, 2, axis=-1)

    def band(t):
        t = t.reshape(B, NB, BLOCK, SWA_KV_HEADS, SWA_HEAD_DIM)
        prev = jnp.concatenate([jnp.zeros_like(t[:, :1]), t[:, :-1]], axis=1)
        return jnp.concatenate([prev, t], axis=2)

    return band(k), band(v)


def swa_mixer(h, kb, vb, w_q, sinks, rel_table, w_out):
    B, T, _ = h.shape
    NB = T // BLOCK
    q = (h @ w_q).reshape(B, NB, BLOCK, SWA_KV_HEADS, SWA_GROUP, SWA_HEAD_DIM)
    s = jnp.einsum('bnqhgd,bnkhd->bnhgqk', q, kb).astype(jnp.float32) * (SWA_HEAD_DIM ** -0.5)
    bucket, valid = rel_bucket_band()
    bias = rel_table.astype(jnp.float32)[bucket]
    bias = bias.transpose(2, 0, 1).reshape(SWA_KV_HEADS, SWA_GROUP, BLOCK, 2 * BLOCK)
    first = (np.arange(NB)[:, None] == 0) & (np.arange(2 * BLOCK)[None, :] < BLOCK)
    mask = jnp.asarray(valid[None] & ~first[:, None, :])
    s = jnp.where(mask[None, :, None, None], s + bias, NEG)
    sink = sinks.astype(jnp.float32).reshape(SWA_KV_HEADS, SWA_GROUP, 1, 1)
    m = jnp.maximum(jnp.max(s, axis=-1, keepdims=True), sink)
    p = jnp.exp(s - m)
    p = p / (jnp.sum(p, axis=-1, keepdims=True) + jnp.exp(sink - m))
    o = jnp.einsum('bnhgqk,bnkhd->bnqhgd', p.astype(vb.dtype), vb)
    return o.reshape(B, T, SWA_Q_HEADS * SWA_HEAD_DIM) @ w_out


def setup_inputs(seed: int = 0) -> dict:
    key = jax.random.key(seed)
    ks = jax.random.split(key, 20)
    f32 = jnp.float32

    def w(k, shape, fan_in, scale=1.0):
        return jax.random.normal(k, shape, f32) * (scale * fan_in ** -0.5)

    def gain(k, shape):
        return 1.0 + 0.02 * jax.random.normal(k, shape, f32)

    return {
        "x": jax.random.normal(ks[0], (BATCH, SEQ, D_MODEL), f32),
        "a_w_in": w(ks[1], (N_A_LAYERS, D_MODEL, A_PROJ), D_MODEL),
        "a_w_gk2": w(ks[2], (N_A_LAYERS, GLA_GATE_RANK, GLA_KEY_DIM), GLA_GATE_RANK),
        "a_b_gk": 0.1 * jax.random.normal(ks[3], (N_A_LAYERS, GLA_KEY_DIM), f32),
        "a_onorm": gain(ks[4], (N_A_LAYERS, GLA_HV)),
        "a_w_out": w(ks[5], (N_A_LAYERS, GLA_VAL_DIM, D_MODEL), GLA_VAL_DIM),
        "kv_norm": gain(ks[6], (D_MODEL,)),
        "w_kv": w(ks[7], (D_MODEL, KV_PROJ), D_MODEL),
        "b_w_q": w(ks[8], (N_B_LAYERS, D_MODEL, SWA_Q_HEADS * SWA_HEAD_DIM), D_MODEL),
        "b_sinks": 0.5 * jax.random.normal(ks[9], (N_B_LAYERS, SWA_Q_HEADS), f32),
        "b_w_out": w(ks[10], (N_B_LAYERS, SWA_Q_HEADS * SWA_HEAD_DIM, D_MODEL), D_MODEL),
        "rel_table": 0.5 * jax.random.normal(ks[11], (REL_BUCKETS, SWA_Q_HEADS), f32),
        "ln_mix": gain(ks[12], (DEPTH, D_MODEL)),
        "ln_mlp": gain(ks[13], (DEPTH, D_MODEL)),
        "w_up": w(ks[14], (DEPTH, D_MODEL, D_FF), D_MODEL),
        "w_down": w(ks[15], (DEPTH, D_FF, D_MODEL), D_FF, scale=0.5),
        "ln_final": gain(ks[16], (D_MODEL,)),
    }


def reference(x, a_w_in, a_w_gk2, a_b_gk, a_onorm, a_w_out, kv_norm, w_kv, b_w_q, b_sinks,
              b_w_out, rel_table, ln_mix, ln_mlp, w_up, w_down, ln_final):
    h = x
    kb = vb = None
    for layer in range(DEPTH):
        if layer < N_A_LAYERS:
            h = h + gla_mixer(rmsnorm(h, ln_mix[layer]), a_w_in[layer], a_w_gk2[layer],
                              a_b_gk[layer], a_onorm[layer], a_w_out[layer])
        else:
            if layer == N_A_LAYERS:
                kb, vb = shared_kv(rmsnorm(h, kv_norm), w_kv)
            j = layer - N_A_LAYERS
            h = h + swa_mixer(rmsnorm(h, ln_mix[layer]), kb, vb, b_w_q[j], b_sinks[j],
                              rel_table, b_w_out[j])
        h = h + sq_relu_mlp(rmsnorm(h, ln_mlp[layer]), w_up[layer], w_down[layer])
    return rmsnorm(h, ln_final)
```

```python
import functools

import numpy as np
import jax
import jax.numpy as jnp
from jax import lax
from jax.experimental import pallas as pl
from jax.experimental.pallas import tpu as pltpu

F32 = jnp.float32
BF16 = jnp.bfloat16

EPS = 1e-6
NEG = -1e30

GLA_HEADS = 4
GLA_GATE_RANK = 16
GLA_GATE_NORM = 16.0
GLA_CHUNK = 64

SWA_HEAD_DIM = 64
SWA_GROUP = 8
WINDOW = 128
BLOCK = 128
REL_BUCKETS = 32
REL_MAX_DIST = 128

LANE = 128
ROW_TILE = 512
COL_CHUNK = 512
GLA_ROWS = 512
VMEM_LIMIT = 56 * 1024 * 1024

_NT = (((1,), (1,)), ((), ()))
_TN = (((0,), (0,)), ((), ()))


def _rms(x, g):
    ms = jnp.mean(x * x, axis=-1, keepdims=True)
    return x * lax.rsqrt(ms + EPS) * g


def _dot(a, b):
    return jnp.dot(a, b, preferred_element_type=F32)


def _params(*sem):
    return pltpu.CompilerParams(dimension_semantics=sem, vmem_limit_bytes=VMEM_LIMIT)


def _resident(shape):
    zeros = (0,) * len(shape)
    return pl.BlockSpec(shape, lambda *_: zeros, pipeline_mode=pl.Buffered(1))


def _norm_proj_kernel(x_ref, g_ref, w_ref, o_ref):
    xn = _rms(x_ref[...], g_ref[...]).astype(BF16)
    n_out = o_ref.shape[-1]
    step = min(COL_CHUNK, n_out)
    for j in range(0, n_out, step):
        o_ref[:, j:j + step] = _dot(xn, w_ref[:, j:j + step]).astype(BF16)


def _norm_proj(x, g, w, name):
    n, d = x.shape
    n_out = w.shape[1]
    return pl.pallas_call(
        _norm_proj_kernel,
        grid=(n // ROW_TILE,),
        in_specs=[pl.BlockSpec((ROW_TILE, d), lambda i: (i, 0)),
                  _resident((1, d)), _resident((d, n_out))],
        out_specs=pl.BlockSpec((ROW_TILE, n_out), lambda i: (i, 0)),
        out_shape=jax.ShapeDtypeStruct((n, n_out), BF16),
        compiler_params=_params("parallel"),
        name=name,
    )(x, g, w)


def _gla_inproj_kernel(x_ref, g_ref, w_ref, wglr_ref, wgk2_ref, bgk_ref, o_ref, gk_ref):
    xn = _rms(x_ref[...], g_ref[...]).astype(BF16)
    n_out = o_ref.shape[-1]
    for j in range(0, n_out, COL_CHUNK):
        o_ref[:, j:j + COL_CHUNK] = _dot(xn, w_ref[:, j:j + COL_CHUNK]).astype(BF16)
    glr = _dot(xn, wglr_ref[...])
    z = _dot(glr.astype(BF16), wgk2_ref[...]) + bgk_ref[...]
    gk_ref[...] = (jnp.minimum(z, 0.0) - jnp.log1p(jnp.exp(-jnp.abs(z)))) * (1.0 / GLA_GATE_NORM)


def _gla_inproj(x, g, w_main, w_glr, w_gk2, b_gk):
    n, d = x.shape
    n_out = w_main.shape[1]
    kd = w_gk2.shape[1]
    return pl.pallas_call(
        _gla_inproj_kernel,
        grid=(n // ROW_TILE,),
        in_specs=[pl.BlockSpec((ROW_TILE, d), lambda i: (i, 0)),
                  _resident((1, d)), _resident((d, n_out)), _resident((d, LANE)),
                  _resident((LANE, kd)), _resident((1, kd))],
        out_specs=[pl.BlockSpec((ROW_TILE, n_out), lambda i: (i, 0)),
                   pl.BlockSpec((ROW_TILE, kd), lambda i: (i, 0))],
        out_shape=[jax.ShapeDtypeStruct((n, n_out), BF16),
                   jax.ShapeDtypeStruct((n, kd), F32)],
        compiler_params=_params("parallel"),
        name="gla_inproj",
    )(x, g, w_main, w_glr, w_gk2, b_gk)


def _cumsum_rows(tril, g):
    hi = g.astype(BF16)
    r = g - hi.astype(F32)
    mid = r.astype(BF16)
    lo = (r - mid.astype(F32)).astype(BF16)
    return _dot(tril, hi) + _dot(tril, mid) + _dot(tril, lo)


def _gla_kernel(q_ref, k_ref, v_ref, gate_ref, gk_ref, gon_ref, o_ref, st_ref, *, hk, hv):
    @pl.when(pl.program_id(1) == 0)
    def _():
        st_ref[...] = jnp.zeros_like(st_ref)

    c = GLA_CHUNK
    ri = lax.broadcasted_iota(jnp.int32, (c, c), 0)
    ci = lax.broadcasted_iota(jnp.int32, (c, c), 1)
    causal = ri >= ci
    tril = jnp.where(causal, 1.0, 0.0).astype(BF16)
    gon = gon_ref[...]
    q_scale = hk ** -0.5

    def chunk(ic, carry):
        rows = pl.ds(pl.multiple_of(ic * c, c), c)
        bcum = _cumsum_rows(tril, gk_ref[rows, :])
        bl = bcum[c - 1:c, :]
        qt = ((q_ref[rows, :].astype(F32) * q_scale) * jnp.exp(bcum)).astype(BF16)
        kf = k_ref[rows, :].astype(F32)
        kt = (kf * jnp.exp(-bcum)).astype(BF16)
        kdec = (kf * jnp.exp(bl - bcum)).astype(BF16)
        el = jnp.exp(bl)
        for h in range(GLA_HEADS):
            ks = slice(h * hk, (h + 1) * hk)
            vs = slice(h * hv, (h + 1) * hv)
            vh = v_ref[rows, vs]
            attn = lax.dot_general(qt[:, ks], kt[:, ks], _NT, preferred_element_type=F32)
            attn = jnp.where(causal, attn, 0.0).astype(BF16)
            st = st_ref[h]
            o = _dot(attn, vh) + lax.dot_general(qt[:, ks], st.astype(BF16), _NT,
                                                 preferred_element_type=F32)
            st_ref[h] = st * el[:, ks] + lax.dot_general(vh, kdec[:, ks], _TN,
                                                         preferred_element_type=F32)
            o = o * lax.rsqrt(jnp.mean(o * o, axis=-1, keepdims=True) + EPS) * gon
            gt = gate_ref[rows, vs].astype(F32)
            o = o * (gt / (1.0 + jnp.exp(-gt)))
            o_ref[rows, vs] = o.astype(BF16)
        return carry

    lax.fori_loop(0, q_ref.shape[0] // c, chunk, 0)


def _gla_core(proj, gk, gon, batch, seq):
    n, kd = gk.shape
    hk = kd // GLA_HEADS
    vd = (proj.shape[1] - 2 * kd) // 2
    hv = vd // GLA_HEADS
    proj3 = proj.reshape(batch, seq, proj.shape[1])
    gk3 = gk.reshape(batch, seq, kd)
    kblk, vblk = kd // kd, (2 * kd) // vd
    out = pl.pallas_call(
        functools.partial(_gla_kernel, hk=hk, hv=hv),
        grid=(batch, seq // GLA_ROWS),
        in_specs=[pl.BlockSpec((None, GLA_ROWS, kd), lambda b, t: (b, t, 0)),
                  pl.BlockSpec((None, GLA_ROWS, kd), lambda b, t: (b, t, kblk)),
                  pl.BlockSpec((None, GLA_ROWS, vd), lambda b, t: (b, t, vblk)),
                  pl.BlockSpec((None, GLA_ROWS, vd), lambda b, t: (b, t, vblk + 1)),
                  pl.BlockSpec((None, GLA_ROWS, kd), lambda b, t: (b, t, 0)),
                  pl.BlockSpec((1, hv), lambda b, t: (0, 0))],
        out_specs=pl.BlockSpec((None, GLA_ROWS, vd), lambda b, t: (b, t, 0)),
        out_shape=jax.ShapeDtypeStruct((batch, seq, vd), BF16),
        scratch_shapes=[pltpu.VMEM((GLA_HEADS, hv, hk), F32)],
        compiler_params=_params("parallel", "arbitrary"),
        name="gla_core",
    )(proj3, proj3, proj3, proj3, gk3, gon)
    return out.reshape(n, vd)


def _post_kernel(h_ref, o_ref, wo_ref, g_ref, wup_ref, wdn_ref, gfin_ref, out_ref, *, final_norm):
    h1 = h_ref[...] + _dot(o_ref[...], wo_ref[...])
    hn = _rms(h1, g_ref[...]).astype(BF16)
    acc = h1
    d_ff = wup_ref.shape[1]
    for j in range(0, d_ff, COL_CHUNK):
        u = jnp.maximum(_dot(hn, wup_ref[:, j:j + COL_CHUNK]), 0.0)
        acc = acc + _dot((u * u).astype(BF16), wdn_ref[j:j + COL_CHUNK, :])
    if final_norm:
        acc = _rms(acc, gfin_ref[...])
    out_ref[...] = acc


def _post(h, o, w_out, g_mlp, w_up, w_down, g_final, final_norm):
    n, d = h.shape
    d_in = o.shape[1]
    d_ff = w_up.shape[1]
    return pl.pallas_call(
        functools.partial(_post_kernel, final_norm=final_norm),
        grid=(n // ROW_TILE,),
        in_specs=[pl.BlockSpec((ROW_TILE, d), lambda i: (i, 0)),
                  pl.BlockSpec((ROW_TILE, d_in), lambda i: (i, 0)),
                  _resident((d_in, d)), _resident((1, d)),
                  _resident((d, d_ff)), _resident((d_ff, d)), _resident((1, d))],
        out_specs=pl.BlockSpec((ROW_TILE, d), lambda i: (i, 0)),
        out_shape=jax.ShapeDtypeStruct((n, d), F32),
        compiler_params=_params("parallel"),
        name="post_mlp",
    )(h, o, w_out, g_mlp, w_up, w_down, g_final)


def _rel_buckets():
    i = np.arange(BLOCK)[:, None]
    j = np.arange(2 * BLOCK)[None, :]
    n = np.maximum(i + BLOCK - j, 0)
    max_exact = REL_BUCKETS // 2
    large = max_exact + (np.log(np.maximum(n, 1) / max_exact)
                         / np.log(REL_MAX_DIST / max_exact)
                         * (REL_BUCKETS - max_exact)).astype(np.int32)
    large = np.minimum(large, REL_BUCKETS - 1)
    return np.where(n < max_exact, n, large).astype(np.int32)


def _swa_kernel(rel_ref, sink_ref, bucket_ref, q_ref, kvp_ref, kvc_ref, o_ref, bias_ref, *, n_kv):
    n_q = bias_ref.shape[0]
    hd = SWA_HEAD_DIM

    @pl.when((pl.program_id(0) == 0) & (pl.program_id(1) == 0))
    def _():
        bucket = bucket_ref[...]

        def per_head(h, carry):
            def per_bucket(b, acc):
                return jnp.where(bucket == b, rel_ref[b, h], acc)
            bias_ref[h] = lax.fori_loop(0, REL_BUCKETS, per_bucket,
                                        jnp.zeros(bucket.shape, F32))
            return carry

        lax.fori_loop(0, n_q, per_head, 0)

    blk = pl.program_id(1)
    qi = lax.broadcasted_iota(jnp.int32, (BLOCK, 2 * BLOCK), 0)
    kj = lax.broadcasted_iota(jnp.int32, (BLOCK, 2 * BLOCK), 1)
    dist = qi + BLOCK - kj
    valid = (dist >= 0) & (dist < WINDOW) & ((blk > 0) | (kj >= BLOCK))
    kvp = kvp_ref[...]
    kvc = kvc_ref[...]
    scale = hd ** -0.5
    for hkv in range(n_kv):
        kcol = slice(hkv * hd, (hkv + 1) * hd)
        vcol = slice((n_kv + hkv) * hd, (n_kv + hkv + 1) * hd)
        kb = jnp.concatenate([kvp[:, kcol], kvc[:, kcol]], axis=0)
        vb = jnp.concatenate([kvp[:, vcol], kvc[:, vcol]], axis=0)
        for g in range(SWA_GROUP):
            hq = hkv * SWA_GROUP + g
            qcol = slice(hq * hd, (hq + 1) * hd)
            s = lax.dot_general(q_ref[:, qcol], kb, _NT, preferred_element_type=F32) * scale
            s = jnp.where(valid, s + bias_ref[hq], NEG)
            sink = sink_ref[hq]
            m = jnp.maximum(jnp.max(s, axis=-1, keepdims=True), sink)
            p = jnp.exp(s - m)
            den = jnp.sum(p, axis=-1, keepdims=True) + jnp.exp(sink - m)
            o = _dot(p.astype(BF16), vb) / den
            o_ref[:, qcol] = o.astype(BF16)


def _swa_core(q, kv, rel_table, sinks, batch, seq):
    n, dq = q.shape
    n_q = dq // SWA_HEAD_DIM
    n_kv = n_q // SWA_GROUP
    nb = seq // BLOCK
    q3 = q.reshape(batch, seq, dq)
    kv3 = kv.reshape(batch, seq, kv.shape[1])
    bucket = jnp.asarray(_rel_buckets())
    smem = pl.BlockSpec(memory_space=pltpu.SMEM)
    out = pl.pallas_call(
        functools.partial(_swa_kernel, n_kv=n_kv),
        grid=(batch, nb),
        in_specs=[smem, smem,
                  pl.BlockSpec((BLOCK, 2 * BLOCK), lambda b, t: (0, 0)),
                  pl.BlockSpec((None, BLOCK, dq), lambda b, t: (b, t, 0)),
                  pl.BlockSpec((None, BLOCK, kv.shape[1]), lambda b, t: (b, jnp.maximum(t - 1, 0), 0)),
                  pl.BlockSpec((None, BLOCK, kv.shape[1]), lambda b, t: (b, t, 0))],
        out_specs=pl.BlockSpec((None, BLOCK, dq), lambda b, t: (b, t, 0)),
        out_shape=jax.ShapeDtypeStruct((batch, seq, dq), BF16),
        scratch_shapes=[pltpu.VMEM((n_q, BLOCK, 2 * BLOCK), F32)],
        compiler_params=_params("arbitrary", "arbitrary"),
        name="swa_core",
    )(rel_table, sinks, bucket, q3, kv3, kv3)
    return out.reshape(n, dq)


def kernel(x, a_w_in, a_w_gk2, a_b_gk, a_onorm, a_w_out, kv_norm, w_kv, b_w_q, b_sinks, b_w_out,
           rel_table, ln_mix, ln_mlp, w_up, w_down, ln_final):
    batch, seq, d = x.shape
    n_a = a_w_in.shape[0]
    n_b = b_w_q.shape[0]
    kd = a_w_gk2.shape[2]
    n_main = a_w_in.shape[2] - GLA_GATE_RANK

    h = x.reshape(batch * seq, d)
    g_final = ln_final.reshape(1, d)
    for layer in range(n_a + n_b):
        g_mix = ln_mix[layer].reshape(1, d)
        if layer < n_a:
            w_in = a_w_in[layer]
            w_glr = jnp.pad(w_in[:, n_main:], ((0, 0), (0, LANE - GLA_GATE_RANK)))
            w_gk2 = jnp.pad(a_w_gk2[layer], ((0, LANE - GLA_GATE_RANK), (0, 0)))
            proj, gk = _gla_inproj(h, g_mix, w_in[:, :n_main].astype(BF16), w_glr.astype(BF16),
                                   w_gk2.astype(BF16), a_b_gk[layer].reshape(1, kd))
            o = _gla_core(proj, gk, a_onorm[layer].reshape(1, -1), batch, seq)
            w_o = a_w_out[layer]
        else:
            j = layer - n_a
            if j == 0:
                kv = _norm_proj(h, kv_norm.reshape(1, d), w_kv.astype(BF16), "kv_proj")
            q = _norm_proj(h, g_mix, b_w_q[j].astype(BF16), "q_proj")
            o = _swa_core(q, kv, rel_table, b_sinks[j], batch, seq)
            w_o = b_w_out[j]
        h = _post(h, o, w_o.astype(BF16), ln_mlp[layer].reshape(1, d), w_up[layer].astype(BF16),
                  w_down[layer].astype(BF16), g_final, layer == n_a + n_b - 1)
    return h.reshape(batch, seq, d)
```

```python
import functools
import math

import numpy as np
import jax
import jax.numpy as jnp
from jax import lax
from jax.experimental import pallas as pl
from jax.experimental.pallas import tpu as pltpu

F32 = jnp.float32
BF16 = jnp.bfloat16

EPS = 1e-6
NEG = -1e30
LOG2E = math.log2(math.e)

GLA_HEADS = 4
GLA_GATE_RANK = 16
GLA_GATE_NORM = 16.0
GLA_CHUNK = 64

SWA_HEAD_DIM = 64
SWA_GROUP = 8
WINDOW = 128
BLOCK = 128
REL_BUCKETS = 32
REL_MAX_DIST = 128

LANE = 128
ROW_TILE = 512
COL_CHUNK = 512
GLA_ROWS = 512
VMEM_LIMIT = 56 * 1024 * 1024

_NT = (((1,), (1,)), ((), ()))
_TN = (((0,), (0,)), ((), ()))


def _rms(x, g):
    ms = jnp.mean(x * x, axis=-1, keepdims=True)
    return x * lax.rsqrt(ms + EPS) * g


def _dot(a, b):
    return jnp.dot(a, b, preferred_element_type=F32)


def _dot_nt(a, b):
    return lax.dot_general(a, b, _NT, preferred_element_type=F32)


def _dot_tn(a, b):
    return lax.dot_general(a, b, _TN, preferred_element_type=F32)


def _params(*sem):
    return pltpu.CompilerParams(dimension_semantics=sem, vmem_limit_bytes=VMEM_LIMIT)


def _resident(shape):
    zeros = (0,) * len(shape)
    return pl.BlockSpec(shape, lambda *_: zeros, pipeline_mode=pl.Buffered(1))


def _norm_proj_kernel(x_ref, g_ref, w_ref, o_ref, *, out_scale):
    xn = _rms(x_ref[...], g_ref[...]).astype(BF16)
    n_out = o_ref.shape[-1]
    step = min(COL_CHUNK, n_out)
    for j in range(0, n_out, step):
        o_ref[:, j:j + step] = (_dot(xn, w_ref[:, j:j + step]) * out_scale).astype(BF16)


def _norm_proj(x, g, w, out_scale, name):
    n, d = x.shape
    n_out = w.shape[1]
    return pl.pallas_call(
        functools.partial(_norm_proj_kernel, out_scale=out_scale),
        grid=(n // ROW_TILE,),
        in_specs=[pl.BlockSpec((ROW_TILE, d), lambda i: (i, 0)),
                  _resident((1, d)), _resident((d, n_out))],
        out_specs=pl.BlockSpec((ROW_TILE, n_out), lambda i: (i, 0)),
        out_shape=jax.ShapeDtypeStruct((n, n_out), BF16),
        compiler_params=_params("parallel"),
        name=name,
    )(x, g, w)


def _kv_proj_kernel(x_ref, g_ref, wk_ref, wvt_ref, kp_ref, vt_ref):
    xn = _rms(x_ref[...], g_ref[...]).astype(BF16)
    kp_ref[...] = _dot(xn, wk_ref[...]).astype(BF16)
    vt_ref[...] = _dot_nt(wvt_ref[...], xn).astype(BF16)


def _kv_proj(x, g, w_kpad, w_vt):
    n, d = x.shape
    return pl.pallas_call(
        _kv_proj_kernel,
        grid=(n // ROW_TILE,),
        in_specs=[pl.BlockSpec((ROW_TILE, d), lambda i: (i, 0)),
                  _resident((1, d)), _resident(w_kpad.shape), _resident(w_vt.shape)],
        out_specs=[pl.BlockSpec((ROW_TILE, w_kpad.shape[1]), lambda i: (i, 0)),
                   pl.BlockSpec((w_vt.shape[0], ROW_TILE), lambda i: (0, i))],
        out_shape=[jax.ShapeDtypeStruct((n, w_kpad.shape[1]), BF16),
                   jax.ShapeDtypeStruct((w_vt.shape[0], n), BF16)],
        compiler_params=_params("parallel"),
        name="kv_proj",
    )(x, g, w_kpad, w_vt)


def _gla_inproj_kernel(x_ref, g_ref, w_ref, wglr_ref, wgk2_ref, bgk_ref, o_ref, el_ref, *, kd, vd, hk):
    tm = x_ref.shape[0]
    c = GLA_CHUNK
    xn = _rms(x_ref[...], g_ref[...]).astype(BF16)
    glr = _dot(xn, wglr_ref[...])
    z = _dot(glr.astype(BF16), wgk2_ref[...]) + bgk_ref[...]
    gk = (jnp.minimum(z, 0.0) - jnp.log1p(jnp.exp(-jnp.abs(z)))) * (LOG2E / GLA_GATE_NORM)
    hi = gk.astype(BF16)
    mid = (gk - hi.astype(F32)).astype(BF16)
    ri = lax.broadcasted_iota(jnp.int32, (c, c), 0)
    ci = lax.broadcasted_iota(jnp.int32, (c, c), 1)
    tril = jnp.where(ri >= ci, 1.0, 0.0).astype(BF16)
    bcum = jnp.concatenate(
        [_dot(tril, hi[i * c:(i + 1) * c]) + _dot(tril, mid[i * c:(i + 1) * c])
         for i in range(tm // c)], axis=0)
    for j in range(0, vd, COL_CHUNK):
        o_ref[:, j:j + COL_CHUNK] = _dot(xn, w_ref[:, j:j + COL_CHUNK]).astype(BF16)
    for j in range(vd, 2 * vd, COL_CHUNK):
        gt = _dot(xn, w_ref[:, j:j + COL_CHUNK])
        o_ref[:, j:j + COL_CHUNK] = (gt / (1.0 + jnp.exp(-gt))).astype(BF16)
    q = _dot(xn, w_ref[:, 2 * vd:2 * vd + kd])
    o_ref[:, 2 * vd:2 * vd + kd] = ((q * (hk ** -0.5)) * jnp.exp2(bcum)).astype(BF16)
    k = _dot(xn, w_ref[:, 2 * vd + kd:2 * vd + 2 * kd])
    kt = k * jnp.exp2(-bcum)
    o_ref[:, 2 * vd + kd:2 * vd + 2 * kd] = kt.astype(BF16)
    for i in range(tm // c):
        el = jnp.exp2(bcum[(i + 1) * c - 1:(i + 1) * c, :])
        el_ref[i:i + 1, :] = el
        o_ref[i * c:(i + 1) * c, 2 * vd + 2 * kd:2 * vd + 3 * kd] = (kt[i * c:(i + 1) * c] * el).astype(BF16)


def _gla_inproj(x, g, w_main, w_glr, w_gk2, b_gk, kd, vd):
    n, d = x.shape
    n_out = 2 * vd + 3 * kd
    per_tile = ROW_TILE // GLA_CHUNK
    return pl.pallas_call(
        functools.partial(_gla_inproj_kernel, kd=kd, vd=vd, hk=kd // GLA_HEADS),
        grid=(n // ROW_TILE,),
        in_specs=[pl.BlockSpec((ROW_TILE, d), lambda i: (i, 0)),
                  _resident((1, d)), _resident(w_main.shape), _resident((d, LANE)),
                  _resident((LANE, kd)), _resident((1, kd))],
        out_specs=[pl.BlockSpec((ROW_TILE, n_out), lambda i: (i, 0)),
                   pl.BlockSpec((per_tile, kd), lambda i: (i, 0))],
        out_shape=[jax.ShapeDtypeStruct((n, n_out), BF16),
                   jax.ShapeDtypeStruct((n // GLA_CHUNK, kd), F32)],
        compiler_params=_params("parallel"),
        name="gla_inproj",
    )(x, g, w_main, w_glr, w_gk2, b_gk)


def _gla_kernel(v_ref, sg_ref, qt_ref, kt_ref, kd_ref, el_ref, gon_ref, o_ref, st_ref, *, hk, hv):
    @pl.when(pl.program_id(1) == 0)
    def _():
        st_ref[...] = jnp.zeros_like(st_ref)

    c = GLA_CHUNK
    ri = lax.broadcasted_iota(jnp.int32, (c, c), 0)
    ci = lax.broadcasted_iota(jnp.int32, (c, c), 1)
    causal = ri >= ci
    gon = gon_ref[...]
    for i in range(v_ref.shape[0] // c):
        rows = slice(i * c, (i + 1) * c)
        el = el_ref[i:i + 1, :]
        for h in range(GLA_HEADS):
            ks = slice(h * hk, (h + 1) * hk)
            vs = slice(h * hv, (h + 1) * hv)
            qh = qt_ref[rows, ks]
            vh = v_ref[rows, vs]
            attn = jnp.where(causal, _dot_nt(qh, kt_ref[rows, ks]), 0.0).astype(BF16)
            st = st_ref[h]
            o = _dot(attn, vh) + _dot_nt(qh, st.astype(BF16))
            st_ref[h] = st * el[:, ks] + _dot_tn(vh, kd_ref[rows, ks])
            o = o * lax.rsqrt(jnp.mean(o * o, axis=-1, keepdims=True) + EPS) * gon
            o_ref[rows, vs] = (o * sg_ref[rows, vs].astype(F32)).astype(BF16)


def _gla_core(proj, el, gon, batch, seq, kd, vd):
    n = proj.shape[0]
    hk = kd // GLA_HEADS
    hv = vd // GLA_HEADS
    proj3 = proj.reshape(batch, seq, proj.shape[1])
    el3 = el.reshape(batch, seq // GLA_CHUNK, kd)
    kblk = 2 * vd // kd
    out = pl.pallas_call(
        functools.partial(_gla_kernel, hk=hk, hv=hv),
        grid=(batch, seq // GLA_ROWS),
        in_specs=[pl.BlockSpec((None, GLA_ROWS, vd), lambda b, t: (b, t, 0)),
                  pl.BlockSpec((None, GLA_ROWS, vd), lambda b, t: (b, t, 1)),
                  pl.BlockSpec((None, GLA_ROWS, kd), lambda b, t: (b, t, kblk)),
                  pl.BlockSpec((None, GLA_ROWS, kd), lambda b, t: (b, t, kblk + 1)),
                  pl.BlockSpec((None, GLA_ROWS, kd), lambda b, t: (b, t, kblk + 2)),
                  pl.BlockSpec((None, GLA_ROWS // GLA_CHUNK, kd), lambda b, t: (b, t, 0)),
                  pl.BlockSpec((1, hv), lambda b, t: (0, 0))],
        out_specs=pl.BlockSpec((None, GLA_ROWS, vd), lambda b, t: (b, t, 0)),
        out_shape=jax.ShapeDtypeStruct((batch, seq, vd), BF16),
        scratch_shapes=[pltpu.VMEM((GLA_HEADS, hv, hk), F32)],
        compiler_params=_params("parallel", "arbitrary"),
        name="gla_core",
    )(proj3, proj3, proj3, proj3, proj3, el3, gon)
    return out.reshape(n, vd)


def _post_kernel(h_ref, o_ref, wo_ref, g_ref, wup_ref, wdn_ref, gfin_ref, out_ref, *, final_norm):
    h1 = h_ref[...] + _dot(o_ref[...], wo_ref[...])
    hn = _rms(h1, g_ref[...]).astype(BF16)
    acc = h1
    d_ff = wup_ref.shape[1]
    for j in range(0, d_ff, COL_CHUNK):
        u = jnp.maximum(_dot(hn, wup_ref[:, j:j + COL_CHUNK]), 0.0)
        acc = acc + _dot((u * u).astype(BF16), wdn_ref[j:j + COL_CHUNK, :])
    if final_norm:
        acc = _rms(acc, gfin_ref[...])
    out_ref[...] = acc


def _post(h, o, w_out, g_mlp, w_up, w_down, g_final, final_norm):
    n, d = h.shape
    d_in = o.shape[1]
    d_ff = w_up.shape[1]
    return pl.pallas_call(
        functools.partial(_post_kernel, final_norm=final_norm),
        grid=(n // ROW_TILE,),
        in_specs=[pl.BlockSpec((ROW_TILE, d), lambda i: (i, 0)),
                  pl.BlockSpec((ROW_TILE, d_in), lambda i: (i, 0)),
                  _resident((d_in, d)), _resident((1, d)),
                  _resident((d, d_ff)), _resident((d_ff, d)), _resident((1, d))],
        out_specs=pl.BlockSpec((ROW_TILE, d), lambda i: (i, 0)),
        out_shape=jax.ShapeDtypeStruct((n, d), F32),
        compiler_params=_params("parallel"),
        name="post_mlp",
    )(h, o, w_out, g_mlp, w_up, w_down, g_final)


def _rel_buckets_t():
    j = np.arange(2 * BLOCK)[:, None]
    i = np.arange(BLOCK)[None, :]
    n = np.maximum(i + BLOCK - j, 0)
    max_exact = REL_BUCKETS // 2
    large = max_exact + (np.log(np.maximum(n, 1) / max_exact)
                         / np.log(REL_MAX_DIST / max_exact)
                         * (REL_BUCKETS - max_exact)).astype(np.int32)
    large = np.minimum(large, REL_BUCKETS - 1)
    return np.where(n < max_exact, n, large).astype(np.int32)


def _swa_kernel(rel_ref, sink_ref, bucket_ref, q_ref, kpp_ref, kpc_ref, vtp_ref, vtc_ref, o_ref,
                bias_ref, ot_ref, *, n_kv):
    n_q = bias_ref.shape[0]
    hd = SWA_HEAD_DIM
    pairs_per_kv = SWA_GROUP // 2

    @pl.when((pl.program_id(0) == 0) & (pl.program_id(1) == 0))
    def _():
        bucket = bucket_ref[...]

        def per_head(h, carry):
            def per_bucket(b, acc):
                return jnp.where(bucket == b, rel_ref[b, h], acc)
            bias_ref[h] = lax.fori_loop(0, REL_BUCKETS, per_bucket,
                                        jnp.zeros(bucket.shape, F32))
            return carry

        lax.fori_loop(0, n_q, per_head, 0)

    blk = pl.program_id(1)
    kj = lax.broadcasted_iota(jnp.int32, (2 * BLOCK, BLOCK), 0)
    qi = lax.broadcasted_iota(jnp.int32, (2 * BLOCK, BLOCK), 1)
    dist = qi + BLOCK - kj
    valid = (dist >= 0) & (dist < WINDOW) & ((blk > 0) | (kj >= BLOCK))
    eye = jnp.where(lax.broadcasted_iota(jnp.int32, (BLOCK, BLOCK), 0)
                    == lax.broadcasted_iota(jnp.int32, (BLOCK, BLOCK), 1), 1.0, 0.0).astype(BF16)
    zeros = jnp.zeros((hd, 2 * BLOCK), BF16)
    for hkv in range(n_kv):
        lo = slice(2 * hkv * LANE, (2 * hkv + 1) * LANE)
        hi = slice((2 * hkv + 1) * LANE, (2 * hkv + 2) * LANE)
        k_stack = jnp.concatenate([kpp_ref[:, lo], kpc_ref[:, lo],
                                   kpp_ref[:, hi], kpc_ref[:, hi]], axis=0)
        vrow = slice(hkv * hd, (hkv + 1) * hd)
        vt_band = jnp.concatenate([vtp_ref[vrow, :], vtc_ref[vrow, :]], axis=1)
        v_bd = jnp.concatenate([jnp.concatenate([vt_band, zeros], axis=1),
                                jnp.concatenate([zeros, vt_band], axis=1)], axis=0)
        for pr in range(pairs_per_kv):
            pair = hkv * pairs_per_kv + pr
            cols = slice(pair * LANE, (pair + 1) * LANE)
            st = _dot_nt(k_stack, q_ref[:, cols])
            ps, rden = [], []
            for e in range(2):
                hq = 2 * pair + e
                s = st[e * 2 * BLOCK:(e + 1) * 2 * BLOCK]
                s = jnp.where(valid, s + bias_ref[hq], NEG)
                sink = sink_ref[hq]
                m = jnp.maximum(jnp.max(s, axis=0, keepdims=True), sink)
                p = jnp.exp(s - m)
                den = jnp.sum(p, axis=0, keepdims=True) + jnp.exp(sink - m)
                ps.append(p.astype(BF16))
                rden.append(jnp.broadcast_to(1.0 / den, (hd, BLOCK)))
            ot = _dot(v_bd, jnp.concatenate(ps, axis=0))
            ot_ref[cols, :] = (ot * jnp.concatenate(rden, axis=0)).astype(BF16)
    o_ref[...] = _dot_nt(eye, ot_ref[...]).astype(BF16)


def _swa_core(q, kpad, vt, rel_table, sinks, batch, seq):
    n, dq = q.shape
    n_q = dq // SWA_HEAD_DIM
    n_kv = n_q // SWA_GROUP
    nb = seq // BLOCK
    q3 = q.reshape(batch, seq, dq)
    kp3 = kpad.reshape(batch, seq, kpad.shape[1])
    bucket = jnp.asarray(_rel_buckets_t())
    smem = pl.BlockSpec(memory_space=pltpu.SMEM)
    prev = lambda t: jnp.maximum(t - 1, 0)
    out = pl.pallas_call(
        functools.partial(_swa_kernel, n_kv=n_kv),
        grid=(batch, nb),
        in_specs=[smem, smem,
                  pl.BlockSpec((2 * BLOCK, BLOCK), lambda b, t: (0, 0)),
                  pl.BlockSpec((None, BLOCK, dq), lambda b, t: (b, t, 0)),
                  pl.BlockSpec((None, BLOCK, kpad.shape[1]), lambda b, t: (b, prev(t), 0)),
                  pl.BlockSpec((None, BLOCK, kpad.shape[1]), lambda b, t: (b, t, 0)),
                  pl.BlockSpec((vt.shape[0], BLOCK), lambda b, t: (0, b * nb + prev(t))),
                  pl.BlockSpec((vt.shape[0], BLOCK), lambda b, t: (0, b * nb + t))],
        out_specs=pl.BlockSpec((None, BLOCK, dq), lambda b, t: (b, t, 0)),
        out_shape=jax.ShapeDtypeStruct((batch, seq, dq), BF16),
        scratch_shapes=[pltpu.VMEM((n_q, 2 * BLOCK, BLOCK), F32),
                        pltpu.VMEM((dq, BLOCK), BF16)],
        compiler_params=_params("arbitrary", "arbitrary"),
        name="swa_core",
    )(rel_table, sinks, bucket, q3, kp3, kp3, vt, vt)
    return out.reshape(n, dq)


def _pad_k_weights(w_k, n_kv):
    hd = SWA_HEAD_DIM
    z = jnp.zeros((w_k.shape[0], LANE - hd), w_k.dtype)
    cols = []
    for h in range(n_kv):
        wk = w_k[:, h * hd:(h + 1) * hd]
        cols += [wk, z, z, wk]
    return jnp.concatenate(cols, axis=1)


def kernel(x, a_w_in, a_w_gk2, a_b_gk, a_onorm, a_w_out, kv_norm, w_kv, b_w_q, b_sinks, b_w_out,
           rel_table, ln_mix, ln_mlp, w_up, w_down, ln_final):
    batch, seq, d = x.shape
    n_a = a_w_in.shape[0]
    n_b = b_w_q.shape[0]
    kd = a_w_gk2.shape[2]
    vd = (a_w_in.shape[2] - GLA_GATE_RANK - 2 * kd) // 2
    n_kv = w_kv.shape[1] // (2 * SWA_HEAD_DIM)

    h = x.reshape(batch * seq, d)
    g_final = ln_final.reshape(1, d)
    for layer in range(n_a + n_b):
        g_mix = ln_mix[layer].reshape(1, d)
        if layer < n_a:
            w_in = a_w_in[layer]
            w_main = jnp.concatenate([w_in[:, 2 * kd:2 * kd + 2 * vd], w_in[:, :2 * kd]], axis=1)
            w_glr = jnp.pad(w_in[:, 2 * kd + 2 * vd:], ((0, 0), (0, LANE - GLA_GATE_RANK)))
            w_gk2 = jnp.pad(a_w_gk2[layer], ((0, LANE - GLA_GATE_RANK), (0, 0)))
            proj, el = _gla_inproj(h, g_mix, w_main.astype(BF16), w_glr.astype(BF16),
                                   w_gk2.astype(BF16), a_b_gk[layer].reshape(1, kd), kd, vd)
            o = _gla_core(proj, el, a_onorm[layer].reshape(1, -1), batch, seq, kd, vd)
            w_o = a_w_out[layer]
        else:
            j = layer - n_a
            if j == 0:
                half = w_kv.shape[1] // 2
                kpad, vt = _kv_proj(h, kv_norm.reshape(1, d),
                                    _pad_k_weights(w_kv[:, :half], n_kv).astype(BF16),
                                    w_kv[:, half:].T.astype(BF16))
            q = _norm_proj(h, g_mix, b_w_q[j].astype(BF16), SWA_HEAD_DIM ** -0.5, "q_proj")
            o = _swa_core(q, kpad, vt, rel_table, b_sinks[j], batch, seq)
            w_o = b_w_out[j]
        h = _post(h, o, w_o.astype(BF16), ln_mlp[layer].reshape(1, d), w_up[layer].astype(BF16),
                  w_down[layer].astype(BF16), g_final, layer == n_a + n_b - 1)
    return h.reshape(batch, seq, d)
```

```python
import functools
import math

import numpy as np
import jax
import jax.numpy as jnp
from jax import lax
from jax.experimental import pallas as pl
from jax.experimental.pallas import tpu as pltpu

F32 = jnp.float32
BF16 = jnp.bfloat16

EPS = 1e-6
NEG = -1e30
LOG2E = math.log2(math.e)

GLA_HEADS = 4
GLA_GATE_RANK = 16
GLA_GATE_NORM = 16.0
GLA_CHUNK = 64

SWA_HEAD_DIM = 64
SWA_GROUP = 8
WINDOW = 128
BLOCK = 128
REL_BUCKETS = 32
REL_MAX_DIST = 128

LANE = 128
ROW_TILE = 512
COL_CHUNK = 512
GLA_ROWS = 512
SWA_QBLOCKS = 4
VMEM_LIMIT = 56 * 1024 * 1024

_NT = (((1,), (1,)), ((), ()))
_TN = (((0,), (0,)), ((), ()))


def _rms(x, g):
    ms = jnp.mean(x * x, axis=-1, keepdims=True)
    return x * lax.rsqrt(ms + EPS) * g


def _dot(a, b):
    return jnp.dot(a, b, preferred_element_type=F32)


def _dot_nt(a, b):
    return lax.dot_general(a, b, _NT, preferred_element_type=F32)


def _dot_tn(a, b):
    return lax.dot_general(a, b, _TN, preferred_element_type=F32)


def _params(*sem):
    return pltpu.CompilerParams(dimension_semantics=sem, vmem_limit_bytes=VMEM_LIMIT)


def _resident(shape):
    zeros = (0,) * len(shape)
    return pl.BlockSpec(shape, lambda *_: zeros, pipeline_mode=pl.Buffered(1))


def _norm_proj_kernel(x_ref, g_ref, w_ref, o_ref, *, out_scale):
    xn = _rms(x_ref[...], g_ref[...]).astype(BF16)
    n_out = o_ref.shape[-1]
    step = min(COL_CHUNK, n_out)
    for j in range(0, n_out, step):
        o_ref[:, j:j + step] = (_dot(xn, w_ref[:, j:j + step]) * out_scale).astype(BF16)


def _norm_proj(x, g, w, out_scale, name):
    n, d = x.shape
    n_out = w.shape[1]
    return pl.pallas_call(
        functools.partial(_norm_proj_kernel, out_scale=out_scale),
        grid=(n // ROW_TILE,),
        in_specs=[pl.BlockSpec((ROW_TILE, d), lambda i: (i, 0)),
                  _resident((1, d)), _resident((d, n_out))],
        out_specs=pl.BlockSpec((ROW_TILE, n_out), lambda i: (i, 0)),
        out_shape=jax.ShapeDtypeStruct((n, n_out), BF16),
        compiler_params=_params("parallel"),
        name=name,
    )(x, g, w)


def _kv_proj_kernel(x_ref, g_ref, wk_ref, wvt_ref, kp_ref, vt_ref):
    xn = _rms(x_ref[...], g_ref[...]).astype(BF16)
    k = _dot(xn, wk_ref[...])
    kr = pltpu.roll(k, SWA_HEAD_DIM, axis=1)
    low = lax.broadcasted_iota(jnp.int32, k.shape, 1) < SWA_HEAD_DIM
    pieces = (jnp.where(low, k, 0.0), jnp.where(low, 0.0, kr),
              jnp.where(low, kr, 0.0), jnp.where(low, 0.0, k))
    for i, piece in enumerate(pieces):
        kp_ref[:, i * LANE:(i + 1) * LANE] = piece.astype(BF16)
    vt_ref[...] = _dot_nt(wvt_ref[...], xn).astype(BF16)


def _kv_proj(x, g, w_k, w_vt):
    n, d = x.shape
    assert w_k.shape[1] == LANE and 2 * SWA_HEAD_DIM == LANE
    return pl.pallas_call(
        _kv_proj_kernel,
        grid=(n // ROW_TILE,),
        in_specs=[pl.BlockSpec((ROW_TILE, d), lambda i: (i, 0)),
                  _resident((1, d)), _resident(w_k.shape), _resident(w_vt.shape)],
        out_specs=[pl.BlockSpec((ROW_TILE, 4 * LANE), lambda i: (i, 0)),
                   pl.BlockSpec((w_vt.shape[0], ROW_TILE), lambda i: (0, i))],
        out_shape=[jax.ShapeDtypeStruct((n, 4 * LANE), BF16),
                   jax.ShapeDtypeStruct((w_vt.shape[0], n), BF16)],
        compiler_params=_params("parallel"),
        name="kv_proj",
    )(x, g, w_k, w_vt)


def _gla_inproj_kernel(x_ref, g_ref, w_ref, wglr_ref, wgk2_ref, bgk_ref, tril_ref, o_ref, el_ref,
                       *, kd, vd, hk):
    tm = x_ref.shape[0]
    c = GLA_CHUNK
    xn = _rms(x_ref[...], g_ref[...]).astype(BF16)
    glr = _dot(xn, wglr_ref[...])
    z = _dot(glr.astype(BF16), wgk2_ref[...]) + bgk_ref[...]
    gk = (jnp.minimum(z, 0.0) - jnp.log1p(jnp.exp(-jnp.abs(z)))) * (LOG2E / GLA_GATE_NORM)
    hi = gk.astype(BF16)
    mid = (gk - hi.astype(F32)).astype(BF16)
    bcum = _dot(tril_ref[...], hi) + _dot(tril_ref[...], mid)
    for j in range(0, vd, COL_CHUNK):
        o_ref[:, j:j + COL_CHUNK] = _dot(xn, w_ref[:, j:j + COL_CHUNK]).astype(BF16)
    for j in range(vd, 2 * vd, COL_CHUNK):
        gt = _dot(xn, w_ref[:, j:j + COL_CHUNK])
        o_ref[:, j:j + COL_CHUNK] = (gt / (1.0 + jnp.exp(-gt))).astype(BF16)
    q = _dot(xn, w_ref[:, 2 * vd:2 * vd + kd])
    o_ref[:, 2 * vd:2 * vd + kd] = ((q * (hk ** -0.5)) * jnp.exp2(bcum)).astype(BF16)
    k = _dot(xn, w_ref[:, 2 * vd + kd:2 * vd + 2 * kd])
    kt = k * jnp.exp2(-bcum)
    o_ref[:, 2 * vd + kd:2 * vd + 2 * kd] = kt.astype(BF16)
    for i in range(tm // c):
        el = jnp.exp2(bcum[(i + 1) * c - 1:(i + 1) * c, :])
        el_ref[i:i + 1, :] = el
        o_ref[i * c:(i + 1) * c, 2 * vd + 2 * kd:2 * vd + 3 * kd] = (kt[i * c:(i + 1) * c] * el).astype(BF16)


def _gla_inproj(x, g, w_main, w_glr, w_gk2, b_gk, kd, vd):
    n, d = x.shape
    n_out = 2 * vd + 3 * kd
    per_tile = ROW_TILE // GLA_CHUNK
    r = np.arange(ROW_TILE)
    tril = jnp.asarray((r[:, None] >= r[None, :]) & (r[:, None] // GLA_CHUNK == r[None, :] // GLA_CHUNK),
                       dtype=BF16)
    return pl.pallas_call(
        functools.partial(_gla_inproj_kernel, kd=kd, vd=vd, hk=kd // GLA_HEADS),
        grid=(n // ROW_TILE,),
        in_specs=[pl.BlockSpec((ROW_TILE, d), lambda i: (i, 0)),
                  _resident((1, d)), _resident(w_main.shape), _resident((d, LANE)),
                  _resident((LANE, kd)), _resident((1, kd)), _resident(tril.shape)],
        out_specs=[pl.BlockSpec((ROW_TILE, n_out), lambda i: (i, 0)),
                   pl.BlockSpec((per_tile, kd), lambda i: (i, 0))],
        out_shape=[jax.ShapeDtypeStruct((n, n_out), BF16),
                   jax.ShapeDtypeStruct((n // GLA_CHUNK, kd), F32)],
        compiler_params=_params("parallel"),
        name="gla_inproj",
    )(x, g, w_main, w_glr, w_gk2, b_gk, tril)


def _gla_kernel(v_ref, sg_ref, qt_ref, kt_ref, kd_ref, el_ref, gon_ref, o_ref, st_ref, *, hk, hv):
    @pl.when(pl.program_id(1) == 0)
    def _():
        st_ref[...] = jnp.zeros_like(st_ref)

    c = GLA_CHUNK
    ri = lax.broadcasted_iota(jnp.int32, (c, c), 0)
    ci = lax.broadcasted_iota(jnp.int32, (c, c), 1)
    causal = ri >= ci
    gon = gon_ref[...]
    for i in range(v_ref.shape[0] // c):
        rows = slice(i * c, (i + 1) * c)
        el = el_ref[i:i + 1, :]
        for h in range(GLA_HEADS):
            ks = slice(h * hk, (h + 1) * hk)
            vs = slice(h * hv, (h + 1) * hv)
            qh = qt_ref[rows, ks]
            vh = v_ref[rows, vs]
            attn = jnp.where(causal, _dot_nt(qh, kt_ref[rows, ks]), 0.0).astype(BF16)
            st = st_ref[h]
            o = _dot(attn, vh) + _dot_nt(qh, st.astype(BF16))
            st_ref[h] = st * el[:, ks] + _dot_tn(vh, kd_ref[rows, ks])
            o = o * lax.rsqrt(jnp.mean(o * o, axis=-1, keepdims=True) + EPS) * gon
            o_ref[rows, vs] = (o * sg_ref[rows, vs].astype(F32)).astype(BF16)


def _gla_core(proj, el, gon, batch, seq, kd, vd):
    n = proj.shape[0]
    hk = kd // GLA_HEADS
    hv = vd // GLA_HEADS
    proj3 = proj.reshape(batch, seq, proj.shape[1])
    el3 = el.reshape(batch, seq // GLA_CHUNK, kd)
    kblk = 2 * vd // kd
    out = pl.pallas_call(
        functools.partial(_gla_kernel, hk=hk, hv=hv),
        grid=(batch, seq // GLA_ROWS),
        in_specs=[pl.BlockSpec((None, GLA_ROWS, vd), lambda b, t: (b, t, 0)),
                  pl.BlockSpec((None, GLA_ROWS, vd), lambda b, t: (b, t, 1)),
                  pl.BlockSpec((None, GLA_ROWS, kd), lambda b, t: (b, t, kblk)),
                  pl.BlockSpec((None, GLA_ROWS, kd), lambda b, t: (b, t, kblk + 1)),
                  pl.BlockSpec((None, GLA_ROWS, kd), lambda b, t: (b, t, kblk + 2)),
                  pl.BlockSpec((None, GLA_ROWS // GLA_CHUNK, kd), lambda b, t: (b, t, 0)),
                  pl.BlockSpec((1, hv), lambda b, t: (0, 0))],
        out_specs=pl.BlockSpec((None, GLA_ROWS, vd), lambda b, t: (b, t, 0)),
        out_shape=jax.ShapeDtypeStruct((batch, seq, vd), BF16),
        scratch_shapes=[pltpu.VMEM((GLA_HEADS, hv, hk), F32)],
        compiler_params=_params("parallel", "arbitrary"),
        name="gla_core",
    )(proj3, proj3, proj3, proj3, proj3, el3, gon)
    return out.reshape(n, vd)


def _post_kernel(h_ref, o_ref, wo_ref, g_ref, wup_ref, wdn_ref, gfin_ref, out_ref, *, final_norm):
    h1 = h_ref[...] + _dot(o_ref[...], wo_ref[...])
    hn = _rms(h1, g_ref[...]).astype(BF16)
    acc = h1
    d_ff = wup_ref.shape[1]
    for j in range(0, d_ff, COL_CHUNK):
        u = jnp.maximum(_dot(hn, wup_ref[:, j:j + COL_CHUNK]), 0.0)
        acc = acc + _dot((u * u).astype(BF16), wdn_ref[j:j + COL_CHUNK, :])
    if final_norm:
        acc = _rms(acc, gfin_ref[...])
    out_ref[...] = acc


def _post(h, o, w_out, g_mlp, w_up, w_down, g_final, final_norm):
    n, d = h.shape
    d_in = o.shape[1]
    d_ff = w_up.shape[1]
    return pl.pallas_call(
        functools.partial(_post_kernel, final_norm=final_norm),
        grid=(n // ROW_TILE,),
        in_specs=[pl.BlockSpec((ROW_TILE, d), lambda i: (i, 0)),
                  pl.BlockSpec((ROW_TILE, d_in), lambda i: (i, 0)),
                  _resident((d_in, d)), _resident((1, d)),
                  _resident((d, d_ff)), _resident((d_ff, d)), _resident((1, d))],
        out_specs=pl.BlockSpec((ROW_TILE, d), lambda i: (i, 0)),
        out_shape=jax.ShapeDtypeStruct((n, d), F32),
        compiler_params=_params("parallel"),
        name="post_mlp",
    )(h, o, w_out, g_mlp, w_up, w_down, g_final)


def _rel_buckets_t():
    j = np.arange(2 * BLOCK)[:, None]
    i = np.arange(BLOCK)[None, :]
    n = np.maximum(i + BLOCK - j, 0)
    max_exact = REL_BUCKETS // 2
    large = max_exact + (np.log(np.maximum(n, 1) / max_exact)
                         / np.log(REL_MAX_DIST / max_exact)
                         * (REL_BUCKETS - max_exact)).astype(np.int32)
    large = np.minimum(large, REL_BUCKETS - 1)
    return np.where(n < max_exact, n, large).astype(np.int32)


def _swa_kernel(rel_ref, sink_ref, bucket_ref, q_ref, kpp_ref, kpc_ref, vtp_ref, vtc_ref, o_ref,
                bias_ref, *, n_kv):
    n_q = bias_ref.shape[0]
    hd = SWA_HEAD_DIM
    pairs_per_kv = SWA_GROUP // 2

    @pl.when((pl.program_id(0) == 0) & (pl.program_id(1) == 0))
    def _():
        bucket = bucket_ref[...]

        def per_head(h, carry):
            def per_bucket(b, acc):
                return jnp.where(bucket == b, rel_ref[b, h] * LOG2E, acc)
            bias_ref[h] = lax.fori_loop(0, REL_BUCKETS, per_bucket,
                                        jnp.zeros(bucket.shape, F32))
            return carry

        lax.fori_loop(0, n_q, per_head, 0)

    kj = lax.broadcasted_iota(jnp.int32, (2 * BLOCK, BLOCK), 0)
    qi = lax.broadcasted_iota(jnp.int32, (2 * BLOCK, BLOCK), 1)
    dist = qi + BLOCK - kj
    in_window = (dist >= 0) & (dist < WINDOW)
    valid_first = in_window & ((pl.program_id(1) > 0) | (kj >= BLOCK))
    zeros = jnp.zeros((hd, 2 * BLOCK), BF16)
    for qb in range(SWA_QBLOCKS):
        rows = slice(qb * BLOCK, (qb + 1) * BLOCK)
        prev_rows = slice((qb - 1) * BLOCK, qb * BLOCK)
        valid = valid_first if qb == 0 else in_window
        for hkv in range(n_kv):
            def k_band(col):
                cols = slice(col * LANE, (col + 1) * LANE)
                prev = kpp_ref[:, cols] if qb == 0 else kpc_ref[prev_rows, cols]
                return [prev, kpc_ref[rows, cols]]
            k_stack = jnp.concatenate(k_band(2 * hkv) + k_band(2 * hkv + 1), axis=0)
            vrow = slice(hkv * hd, (hkv + 1) * hd)
            if qb == 0:
                vt_band = jnp.concatenate([vtp_ref[vrow, :], vtc_ref[vrow, :BLOCK]], axis=1)
            else:
                vt_band = vtc_ref[vrow, (qb - 1) * BLOCK:(qb + 1) * BLOCK]
            v_bd = jnp.concatenate([jnp.concatenate([vt_band, zeros], axis=1),
                                    jnp.concatenate([zeros, vt_band], axis=1)], axis=0)
            for pr in range(pairs_per_kv):
                pair = hkv * pairs_per_kv + pr
                cols = slice(pair * LANE, (pair + 1) * LANE)
                st = _dot_nt(k_stack, q_ref[rows, cols])
                ps, rden = [], []
                for e in range(2):
                    hq = 2 * pair + e
                    s = jnp.where(valid, st[e * 2 * BLOCK:(e + 1) * 2 * BLOCK] + bias_ref[hq], NEG)
                    sink = sink_ref[hq] * LOG2E
                    m = jnp.maximum(jnp.max(s, axis=0, keepdims=True), sink)
                    p = jnp.exp2(s - m)
                    den = jnp.sum(p, axis=0, keepdims=True) + jnp.exp2(sink - m)
                    ps.append(p.astype(BF16))
                    rden.append(jnp.broadcast_to(1.0 / den, (hd, BLOCK)))
                ot = _dot(v_bd, jnp.concatenate(ps, axis=0)) * jnp.concatenate(rden, axis=0)
                o_ref[rows, cols] = ot.T.astype(BF16)


def _swa_core(q, kpad, vt, rel_table, sinks, batch, seq):
    n, dq = q.shape
    n_q = dq // SWA_HEAD_DIM
    n_kv = n_q // SWA_GROUP
    rows = SWA_QBLOCKS * BLOCK
    steps = seq // rows
    q3 = q.reshape(batch, seq, dq)
    kp3 = kpad.reshape(batch, seq, kpad.shape[1])
    bucket = jnp.asarray(_rel_buckets_t())
    smem = pl.BlockSpec(memory_space=pltpu.SMEM)
    prev = lambda t: jnp.maximum(SWA_QBLOCKS * t - 1, 0)
    out = pl.pallas_call(
        functools.partial(_swa_kernel, n_kv=n_kv),
        grid=(batch, steps),
        in_specs=[smem, smem,
                  pl.BlockSpec((2 * BLOCK, BLOCK), lambda b, t: (0, 0)),
                  pl.BlockSpec((None, rows, dq), lambda b, t: (b, t, 0)),
                  pl.BlockSpec((None, BLOCK, kpad.shape[1]), lambda b, t: (b, prev(t), 0)),
                  pl.BlockSpec((None, rows, kpad.shape[1]), lambda b, t: (b, t, 0)),
                  pl.BlockSpec((vt.shape[0], BLOCK),
                               lambda b, t: (0, b * steps * SWA_QBLOCKS + prev(t))),
                  pl.BlockSpec((vt.shape[0], rows), lambda b, t: (0, b * steps + t))],
        out_specs=pl.BlockSpec((None, rows, dq), lambda b, t: (b, t, 0)),
        out_shape=jax.ShapeDtypeStruct((batch, seq, dq), BF16),
        scratch_shapes=[pltpu.VMEM((n_q, 2 * BLOCK, BLOCK), F32)],
        compiler_params=_params("arbitrary", "arbitrary"),
        name="swa_core",
    )(rel_table, sinks, bucket, q3, kp3, kp3, vt, vt)
    return out.reshape(n, dq)


def kernel(x, a_w_in, a_w_gk2, a_b_gk, a_onorm, a_w_out, kv_norm, w_kv, b_w_q, b_sinks, b_w_out,
           rel_table, ln_mix, ln_mlp, w_up, w_down, ln_final):
    batch, seq, d = x.shape
    n_a = a_w_in.shape[0]
    n_b = b_w_q.shape[0]
    kd = a_w_gk2.shape[2]
    vd = (a_w_in.shape[2] - GLA_GATE_RANK - 2 * kd) // 2

    h = x.reshape(batch * seq, d)
    g_final = ln_final.reshape(1, d)
    for layer in range(n_a + n_b):
        g_mix = ln_mix[layer].reshape(1, d)
        if layer < n_a:
            w_in = a_w_in[layer]
            w_main = jnp.concatenate([w_in[:, 2 * kd:2 * kd + 2 * vd], w_in[:, :2 * kd]], axis=1)
            w_glr = jnp.pad(w_in[:, 2 * kd + 2 * vd:], ((0, 0), (0, LANE - GLA_GATE_RANK)))
            w_gk2 = jnp.pad(a_w_gk2[layer], ((0, LANE - GLA_GATE_RANK), (0, 0)))
            proj, el = _gla_inproj(h, g_mix, w_main.astype(BF16), w_glr.astype(BF16),
                                   w_gk2.astype(BF16), a_b_gk[layer].reshape(1, kd), kd, vd)
            o = _gla_core(proj, el, a_onorm[layer].reshape(1, -1), batch, seq, kd, vd)
            w_o = a_w_out[layer]
        else:
            j = layer - n_a
            if j == 0:
                half = w_kv.shape[1] // 2
                kpad, vt = _kv_proj(h, kv_norm.reshape(1, d), w_kv[:, :half].astype(BF16),
                                    w_kv[:, half:].T.astype(BF16))
            q = _norm_proj(h, g_mix, b_w_q[j].astype(BF16), SWA_HEAD_DIM ** -0.5 * LOG2E, "q_proj")
            o = _swa_core(q, kpad, vt, rel_table, b_sinks[j], batch, seq)
            w_o = b_w_out[j]
        h = _post(h, o, w_o.astype(BF16), ln_mlp[layer].reshape(1, d), w_up[layer].astype(BF16),
                  w_down[layer].astype(BF16), g_final, layer == n_a + n_b - 1)
    return h.reshape(batch, seq, d)
```

```python
import functools
import math

import numpy as np
import jax
import jax.numpy as jnp
from jax import lax
from jax.experimental import pallas as pl
from jax.experimental.pallas import tpu as pltpu

F32 = jnp.float32
BF16 = jnp.bfloat16

EPS = 1e-6
NEG = -1e30
LOG2E = math.log2(math.e)

GLA_HEADS = 4
GLA_GATE_RANK = 16
GLA_GATE_NORM = 16.0
GLA_CHUNK = 64

SWA_HEAD_DIM = 64
SWA_GROUP = 8
WINDOW = 128
BLOCK = 128
REL_BUCKETS = 32
REL_MAX_DIST = 128

LANE = 128
ROW_TILE = 512
COL_CHUNK = 512
GLA_ROWS = 512
SWA_QBLOCKS = 4
VMEM_LIMIT = 56 * 1024 * 1024

_NT = (((1,), (1,)), ((), ()))
_TN = (((0,), (0,)), ((), ()))


def _rms(x, g):
    ms = jnp.mean(x * x, axis=-1, keepdims=True)
    return x * lax.rsqrt(ms + EPS) * g


def _dot(a, b):
    return jnp.dot(a, b, preferred_element_type=F32)


def _dot_nt(a, b):
    return lax.dot_general(a, b, _NT, preferred_element_type=F32)


def _dot_tn(a, b):
    return lax.dot_general(a, b, _TN, preferred_element_type=F32)


def _params(*sem):
    return pltpu.CompilerParams(dimension_semantics=sem, vmem_limit_bytes=VMEM_LIMIT)


def _resident(shape):
    zeros = (0,) * len(shape)
    return pl.BlockSpec(shape, lambda *_: zeros, pipeline_mode=pl.Buffered(1))


def _layer(stacked, layer):
    idx = (layer,) + (0,) * (stacked.ndim - 1)
    return pl.BlockSpec((None,) + stacked.shape[1:], lambda *_: idx, pipeline_mode=pl.Buffered(1))


def _store_q(xn, wq_ref, q_ref):
    scale = SWA_HEAD_DIM ** -0.5 * LOG2E
    for j in range(0, q_ref.shape[-1], COL_CHUNK):
        q_ref[:, j:j + COL_CHUNK] = (_dot(xn, wq_ref[:, j:j + COL_CHUNK]) * scale).astype(BF16)


def _store_kv(xn, wk_ref, wvt_ref, kp_ref, vt_ref):
    k = _dot(xn, wk_ref[...])
    kr = pltpu.roll(k, SWA_HEAD_DIM, axis=1)
    low = lax.broadcasted_iota(jnp.int32, k.shape, 1) < SWA_HEAD_DIM
    pieces = (jnp.where(low, k, 0.0), jnp.where(low, 0.0, kr),
              jnp.where(low, kr, 0.0), jnp.where(low, 0.0, k))
    for i, piece in enumerate(pieces):
        kp_ref[:, i * LANE:(i + 1) * LANE] = piece.astype(BF16)
    vt_ref[...] = _dot_nt(wvt_ref[...], xn).astype(BF16)


def _gla_inproj_kernel(x_ref, g_ref, w_ref, wglr_ref, wgk2_ref, bgk_ref, tril_ref, o_ref, el_ref,
                       *, kd, vd, hk):
    tm = x_ref.shape[0]
    c = GLA_CHUNK
    xn = _rms(x_ref[...], g_ref[...]).astype(BF16)
    glr = _dot(xn, wglr_ref[...])
    z = _dot(glr.astype(BF16), wgk2_ref[...]) + bgk_ref[...]
    gk = (jnp.minimum(z, 0.0) - jnp.log1p(jnp.exp(-jnp.abs(z)))) * (LOG2E / GLA_GATE_NORM)
    hi = gk.astype(BF16)
    mid = (gk - hi.astype(F32)).astype(BF16)
    bcum = _dot(tril_ref[...], hi) + _dot(tril_ref[...], mid)
    for j in range(0, vd, COL_CHUNK):
        o_ref[:, j:j + COL_CHUNK] = _dot(xn, w_ref[:, 2 * kd + j:2 * kd + j + COL_CHUNK]).astype(BF16)
    for j in range(vd, 2 * vd, COL_CHUNK):
        gt = _dot(xn, w_ref[:, 2 * kd + j:2 * kd + j + COL_CHUNK])
        o_ref[:, j:j + COL_CHUNK] = (gt / (1.0 + jnp.exp(-gt))).astype(BF16)
    q = _dot(xn, w_ref[:, :kd])
    o_ref[:, 2 * vd:2 * vd + kd] = ((q * (hk ** -0.5)) * jnp.exp2(bcum)).astype(BF16)
    k = _dot(xn, w_ref[:, kd:2 * kd])
    kt = k * jnp.exp2(-bcum)
    o_ref[:, 2 * vd + kd:2 * vd + 2 * kd] = kt.astype(BF16)
    for i in range(tm // c):
        el = jnp.exp2(bcum[(i + 1) * c - 1:(i + 1) * c, :])
        el_ref[i:i + 1, :] = el
        o_ref[i * c:(i + 1) * c, 2 * vd + 2 * kd:2 * vd + 3 * kd] = (kt[i * c:(i + 1) * c] * el).astype(BF16)


def _gla_inproj(x, layer, mix_layer, g_mix, w_in, w_glr, w_gk2, b_gk, kd, vd):
    n, d = x.shape
    n_out = 2 * vd + 3 * kd
    per_tile = ROW_TILE // GLA_CHUNK
    r = np.arange(ROW_TILE)
    tril = jnp.asarray((r[:, None] >= r[None, :]) & (r[:, None] // GLA_CHUNK == r[None, :] // GLA_CHUNK),
                       dtype=BF16)
    return pl.pallas_call(
        functools.partial(_gla_inproj_kernel, kd=kd, vd=vd, hk=kd // GLA_HEADS),
        grid=(n // ROW_TILE,),
        in_specs=[pl.BlockSpec((ROW_TILE, d), lambda i: (i, 0)),
                  _layer(g_mix, mix_layer), _layer(w_in, layer), _layer(w_glr, layer),
                  _layer(w_gk2, layer), _layer(b_gk, layer), _resident(tril.shape)],
        out_specs=[pl.BlockSpec((ROW_TILE, n_out), lambda i: (i, 0)),
                   pl.BlockSpec((per_tile, kd), lambda i: (i, 0))],
        out_shape=[jax.ShapeDtypeStruct((n, n_out), BF16),
                   jax.ShapeDtypeStruct((n // GLA_CHUNK, kd), F32)],
        compiler_params=_params("parallel"),
        name="gla_inproj",
    )(x, g_mix, w_in, w_glr, w_gk2, b_gk, tril)


def _gla_kernel(v_ref, sg_ref, qt_ref, kt_ref, kd_ref, el_ref, gon_ref, o_ref, st_ref, *, hk, hv):
    @pl.when(pl.program_id(1) == 0)
    def _():
        st_ref[...] = jnp.zeros_like(st_ref)

    c = GLA_CHUNK
    ri = lax.broadcasted_iota(jnp.int32, (c, c), 0)
    ci = lax.broadcasted_iota(jnp.int32, (c, c), 1)
    causal = ri >= ci
    gon = gon_ref[...]
    for i in range(v_ref.shape[0] // c):
        rows = slice(i * c, (i + 1) * c)
        el = el_ref[i:i + 1, :]
        for h in range(GLA_HEADS):
            ks = slice(h * hk, (h + 1) * hk)
            vs = slice(h * hv, (h + 1) * hv)
            qh = qt_ref[rows, ks]
            vh = v_ref[rows, vs]
            attn = jnp.where(causal, _dot_nt(qh, kt_ref[rows, ks]), 0.0).astype(BF16)
            st = st_ref[h]
            o = _dot(attn, vh) + _dot_nt(qh, st.astype(BF16))
            st_ref[h] = st * el[:, ks] + _dot_tn(vh, kd_ref[rows, ks])
            o = o * lax.rsqrt(jnp.mean(o * o, axis=-1, keepdims=True) + EPS) * gon
            o_ref[rows, vs] = (o * sg_ref[rows, vs].astype(F32)).astype(BF16)


def _gla_core(proj, el, gon, batch, seq, kd, vd):
    n = proj.shape[0]
    hk = kd // GLA_HEADS
    hv = vd // GLA_HEADS
    proj3 = proj.reshape(batch, seq, proj.shape[1])
    el3 = el.reshape(batch, seq // GLA_CHUNK, kd)
    kblk = 2 * vd // kd
    out = pl.pallas_call(
        functools.partial(_gla_kernel, hk=hk, hv=hv),
        grid=(batch, seq // GLA_ROWS),
        in_specs=[pl.BlockSpec((None, GLA_ROWS, vd), lambda b, t: (b, t, 0)),
                  pl.BlockSpec((None, GLA_ROWS, vd), lambda b, t: (b, t, 1)),
                  pl.BlockSpec((None, GLA_ROWS, kd), lambda b, t: (b, t, kblk)),
                  pl.BlockSpec((None, GLA_ROWS, kd), lambda b, t: (b, t, kblk + 1)),
                  pl.BlockSpec((None, GLA_ROWS, kd), lambda b, t: (b, t, kblk + 2)),
                  pl.BlockSpec((None, GLA_ROWS // GLA_CHUNK, kd), lambda b, t: (b, t, 0)),
                  pl.BlockSpec((1, hv), lambda b, t: (0, 0))],
        out_specs=pl.BlockSpec((None, GLA_ROWS, vd), lambda b, t: (b, t, 0)),
        out_shape=jax.ShapeDtypeStruct((batch, seq, vd), BF16),
        scratch_shapes=[pltpu.VMEM((GLA_HEADS, hv, hk), F32)],
        compiler_params=_params("parallel", "arbitrary"),
        name="gla_core",
    )(proj3, proj3, proj3, proj3, proj3, el3, gon)
    return out.reshape(n, vd)


def _post_kernel(h_ref, o_ref, wo_ref, g_ref, wup_ref, wdn_ref, *rest, tail):
    h1 = h_ref[...] + _dot(o_ref[...], wo_ref[...])
    hn = _rms(h1, g_ref[...]).astype(BF16)
    acc = h1
    d_ff = wup_ref.shape[1]
    for j in range(0, d_ff, COL_CHUNK):
        u = jnp.maximum(_dot(hn, wup_ref[:, j:j + COL_CHUNK]), 0.0)
        acc = acc + _dot((u * u).astype(BF16), wdn_ref[j:j + COL_CHUNK, :])
    if tail == "none":
        (out_ref,) = rest
    elif tail == "final":
        gfin_ref, out_ref = rest
        acc = _rms(acc, gfin_ref[...])
    else:
        if tail == "kvq":
            gq_ref, wq_ref, gkv_ref, wk_ref, wvt_ref, out_ref, q_ref, kp_ref, vt_ref = rest
        else:
            gq_ref, wq_ref, out_ref, q_ref = rest
        xhat = acc * lax.rsqrt(jnp.mean(acc * acc, axis=-1, keepdims=True) + EPS)
        _store_q((xhat * gq_ref[...]).astype(BF16), wq_ref, q_ref)
        if tail == "kvq":
            _store_kv((xhat * gkv_ref[...]).astype(BF16), wk_ref, wvt_ref, kp_ref, vt_ref)
    out_ref[...] = acc


def _post(h, o, params, tail, n_q_cols=0, n_vt_rows=0):
    n, d = h.shape
    rows = lambda width: pl.BlockSpec((ROW_TILE, width), lambda i: (i, 0))
    out_specs = [rows(d)]
    out_shape = [jax.ShapeDtypeStruct((n, d), F32)]
    if tail in ("q", "kvq"):
        out_specs.append(rows(n_q_cols))
        out_shape.append(jax.ShapeDtypeStruct((n, n_q_cols), BF16))
    if tail == "kvq":
        out_specs += [rows(4 * LANE), pl.BlockSpec((n_vt_rows, ROW_TILE), lambda i: (0, i))]
        out_shape += [jax.ShapeDtypeStruct((n, 4 * LANE), BF16),
                      jax.ShapeDtypeStruct((n_vt_rows, n), BF16)]
    return pl.pallas_call(
        functools.partial(_post_kernel, tail=tail),
        grid=(n // ROW_TILE,),
        in_specs=[rows(d), rows(o.shape[1])] + [spec for _, spec in params],
        out_specs=out_specs,
        out_shape=out_shape,
        compiler_params=_params("parallel"),
        name="post_mlp",
    )(h, o, *[arr for arr, _ in params])


def _rel_buckets_t():
    j = np.arange(2 * BLOCK)[:, None]
    i = np.arange(BLOCK)[None, :]
    n = np.maximum(i + BLOCK - j, 0)
    max_exact = REL_BUCKETS // 2
    large = max_exact + (np.log(np.maximum(n, 1) / max_exact)
                         / np.log(REL_MAX_DIST / max_exact)
                         * (REL_BUCKETS - max_exact)).astype(np.int32)
    large = np.minimum(large, REL_BUCKETS - 1)
    return np.where(n < max_exact, n, large).astype(np.int32)


def _swa_kernel(rel_ref, sink_ref, bucket_ref, q_ref, kpp_ref, kpc_ref, vtp_ref, vtc_ref, o_ref,
                bias_ref, *, n_kv):
    n_q = bias_ref.shape[0]
    hd = SWA_HEAD_DIM
    pairs_per_kv = SWA_GROUP // 2

    @pl.when((pl.program_id(0) == 0) & (pl.program_id(1) == 0))
    def _():
        bucket = bucket_ref[...]

        def per_head(h, carry):
            def per_bucket(b, acc):
                return jnp.where(bucket == b, rel_ref[b, h] * LOG2E, acc)
            bias_ref[h] = lax.fori_loop(0, REL_BUCKETS, per_bucket,
                                        jnp.zeros(bucket.shape, F32))
            return carry

        lax.fori_loop(0, n_q, per_head, 0)

    kj = lax.broadcasted_iota(jnp.int32, (2 * BLOCK, BLOCK), 0)
    qi = lax.broadcasted_iota(jnp.int32, (2 * BLOCK, BLOCK), 1)
    dist = qi + BLOCK - kj
    in_window = (dist >= 0) & (dist < WINDOW)
    valid_first = in_window & ((pl.program_id(1) > 0) | (kj >= BLOCK))
    zeros = jnp.zeros((hd, 2 * BLOCK), BF16)
    for qb in range(SWA_QBLOCKS):
        rows = slice(qb * BLOCK, (qb + 1) * BLOCK)
        prev_rows = slice((qb - 1) * BLOCK, qb * BLOCK)
        valid = valid_first if qb == 0 else in_window
        for hkv in range(n_kv):
            def k_band(col):
                cols = slice(col * LANE, (col + 1) * LANE)
                prev = kpp_ref[:, cols] if qb == 0 else kpc_ref[prev_rows, cols]
                return [prev, kpc_ref[rows, cols]]
            k_stack = jnp.concatenate(k_band(2 * hkv) + k_band(2 * hkv + 1), axis=0)
            vrow = slice(hkv * hd, (hkv + 1) * hd)
            if qb == 0:
                vt_band = jnp.concatenate([vtp_ref[vrow, :], vtc_ref[vrow, :BLOCK]], axis=1)
            else:
                vt_band = vtc_ref[vrow, (qb - 1) * BLOCK:(qb + 1) * BLOCK]
            v_bd = jnp.concatenate([jnp.concatenate([vt_band, zeros], axis=1),
                                    jnp.concatenate([zeros, vt_band], axis=1)], axis=0)
            for pr in range(pairs_per_kv):
                pair = hkv * pairs_per_kv + pr
                cols = slice(pair * LANE, (pair + 1) * LANE)
                st = _dot_nt(k_stack, q_ref[rows, cols])
                ps, rden = [], []
                for e in range(2):
                    hq = 2 * pair + e
                    s = jnp.where(valid, st[e * 2 * BLOCK:(e + 1) * 2 * BLOCK] + bias_ref[hq], NEG)
                    sink = sink_ref[hq] * LOG2E
                    m = jnp.maximum(jnp.max(s, axis=0, keepdims=True), sink)
                    p = jnp.exp2(s - m)
                    den = jnp.sum(p, axis=0, keepdims=True) + jnp.exp2(sink - m)
                    ps.append(p.astype(BF16))
                    rden.append(jnp.broadcast_to(1.0 / den, (hd, BLOCK)))
                ot = _dot(v_bd, jnp.concatenate(ps, axis=0)) * jnp.concatenate(rden, axis=0)
                o_ref[rows, cols] = ot.T.astype(BF16)


def _swa_core(q, kpad, vt, rel_table, sinks, batch, seq):
    n, dq = q.shape
    n_q = dq // SWA_HEAD_DIM
    n_kv = n_q // SWA_GROUP
    rows = SWA_QBLOCKS * BLOCK
    steps = seq // rows
    q3 = q.reshape(batch, seq, dq)
    kp3 = kpad.reshape(batch, seq, kpad.shape[1])
    bucket = jnp.asarray(_rel_buckets_t())
    smem = pl.BlockSpec(memory_space=pltpu.SMEM)
    prev = lambda t: jnp.maximum(SWA_QBLOCKS * t - 1, 0)
    out = pl.pallas_call(
        functools.partial(_swa_kernel, n_kv=n_kv),
        grid=(batch, steps),
        in_specs=[smem, smem,
                  pl.BlockSpec((2 * BLOCK, BLOCK), lambda b, t: (0, 0)),
                  pl.BlockSpec((None, rows, dq), lambda b, t: (b, t, 0)),
                  pl.BlockSpec((None, BLOCK, kpad.shape[1]), lambda b, t: (b, prev(t), 0)),
                  pl.BlockSpec((None, rows, kpad.shape[1]), lambda b, t: (b, t, 0)),
                  pl.BlockSpec((vt.shape[0], BLOCK),
                               lambda b, t: (0, b * steps * SWA_QBLOCKS + prev(t))),
                  pl.BlockSpec((vt.shape[0], rows), lambda b, t: (0, b * steps + t))],
        out_specs=pl.BlockSpec((None, rows, dq), lambda b, t: (b, t, 0)),
        out_shape=jax.ShapeDtypeStruct((batch, seq, dq), BF16),
        scratch_shapes=[pltpu.VMEM((n_q, 2 * BLOCK, BLOCK), F32)],
        compiler_params=_params("arbitrary", "arbitrary"),
        name="swa_core",
    )(rel_table, sinks, bucket, q3, kp3, kp3, vt, vt)
    return out.reshape(n, dq)


def kernel(x, a_w_in, a_w_gk2, a_b_gk, a_onorm, a_w_out, kv_norm, w_kv, b_w_q, b_sinks, b_w_out,
           rel_table, ln_mix, ln_mlp, w_up, w_down, ln_final):
    batch, seq, d = x.shape
    n_a = a_w_in.shape[0]
    n_b = b_w_q.shape[0]
    kd = a_w_gk2.shape[2]
    vd = (a_w_in.shape[2] - GLA_GATE_RANK - 2 * kd) // 2

    depth = n_a + n_b
    assert n_a >= 1
    half = w_kv.shape[1] // 2
    assert half == LANE and 2 * SWA_HEAD_DIM == LANE

    w_in = a_w_in.astype(BF16)
    w_glr = jnp.pad(a_w_in[:, :, 2 * kd + 2 * vd:],
                    ((0, 0), (0, 0), (0, LANE - GLA_GATE_RANK))).astype(BF16)
    w_gk2 = jnp.pad(a_w_gk2, ((0, 0), (0, LANE - GLA_GATE_RANK), (0, 0))).astype(BF16)
    w_out = {True: a_w_out.astype(BF16), False: b_w_out.astype(BF16)}
    w_q = b_w_q.astype(BF16)
    w_k = w_kv[:, :half].astype(BF16)
    w_vt = w_kv[:, half:].T.astype(BF16)
    w_up_b = w_up.astype(BF16)
    w_down_b = w_down.astype(BF16)
    g_mix = ln_mix.reshape(depth, 1, d)
    g_mlp = ln_mlp.reshape(depth, 1, d)
    g_kv = kv_norm.reshape(1, d)
    g_final = ln_final.reshape(1, d)
    b_gk = a_b_gk.reshape(n_a, 1, kd)

    h = x.reshape(batch * seq, d)
    for layer in range(depth):
        is_gla = layer < n_a
        j = layer if is_gla else layer - n_a
        if is_gla:
            proj, el = _gla_inproj(h, j, layer, g_mix, w_in, w_glr, w_gk2, b_gk, kd, vd)
            o = _gla_core(proj, el, a_onorm[j].reshape(1, -1), batch, seq, kd, vd)
        else:
            o = _swa_core(q, kpad, vt, rel_table, b_sinks[j], batch, seq)
        params = [(w_out[is_gla], _layer(w_out[is_gla], j)), (g_mlp, _layer(g_mlp, layer)),
                  (w_up_b, _layer(w_up_b, layer)), (w_down_b, _layer(w_down_b, layer))]
        if layer == depth - 1:
            (h,) = _post(h, o, params + [(g_final, _resident(g_final.shape))], "final")
        elif layer + 1 < n_a:
            (h,) = _post(h, o, params, "none")
        else:
            params += [(g_mix, _layer(g_mix, layer + 1)), (w_q, _layer(w_q, layer + 1 - n_a))]
            if layer + 1 == n_a:
                params += [(g_kv, _resident(g_kv.shape)), (w_k, _resident(w_k.shape)),
                           (w_vt, _resident(w_vt.shape))]
                h, q, kpad, vt = _post(h, o, params, "kvq", w_q.shape[2], w_vt.shape[0])
            else:
                h, q = _post(h, o, params, "q", w_q.shape[2])
    return h.reshape(batch, seq, d)
```

```python
import functools
import math

import numpy as np
import jax
import jax.numpy as jnp
from jax import lax
from jax.experimental import pallas as pl
from jax.experimental.pallas import tpu as pltpu

F32 = jnp.float32
BF16 = jnp.bfloat16

EPS = 1e-6
NEG = -1e30
LOG2E = math.log2(math.e)

GLA_HEADS = 4
GLA_GATE_RANK = 16
GLA_GATE_NORM = 16.0
GLA_CHUNK = 64

SWA_HEAD_DIM = 64
SWA_GROUP = 8
WINDOW = 128
BLOCK = 128
REL_BUCKETS = 32
REL_MAX_DIST = 128

LANE = 128
ROW_TILE = 512
COL_CHUNK = 512
GLA_ROWS = 512
SWA_QBLOCKS = 8
VMEM_LIMIT = 56 * 1024 * 1024

_NT = (((1,), (1,)), ((), ()))
_TN = (((0,), (0,)), ((), ()))


def _rms(x, g):
    ms = jnp.mean(x * x, axis=-1, keepdims=True)
    return x * lax.rsqrt(ms + EPS) * g


def _dot(a, b):
    return jnp.dot(a, b, preferred_element_type=F32)


def _dot_nt(a, b):
    return lax.dot_general(a, b, _NT, preferred_element_type=F32)


def _dot_tn(a, b):
    return lax.dot_general(a, b, _TN, preferred_element_type=F32)


def _params(*sem):
    return pltpu.CompilerParams(dimension_semantics=sem, vmem_limit_bytes=VMEM_LIMIT)


def _resident(shape):
    zeros = (0,) * len(shape)
    return pl.BlockSpec(shape, lambda *_: zeros, pipeline_mode=pl.Buffered(1))


def _layer(stacked, layer):
    idx = (layer,) + (0,) * (stacked.ndim - 1)
    return pl.BlockSpec((None,) + stacked.shape[1:], lambda *_: idx, pipeline_mode=pl.Buffered(1))


def _store_q(xn, wq_ref, q_ref):
    scale = SWA_HEAD_DIM ** -0.5 * LOG2E
    for j in range(0, q_ref.shape[-1], COL_CHUNK):
        q_ref[:, j:j + COL_CHUNK] = (_dot(xn, wq_ref[:, j:j + COL_CHUNK]) * scale).astype(BF16)


def _store_kv(xn, wk_ref, wvt_ref, kp_ref, vt_ref):
    k = _dot(xn, wk_ref[...])
    kr = pltpu.roll(k, SWA_HEAD_DIM, axis=1)
    low = lax.broadcasted_iota(jnp.int32, k.shape, 1) < SWA_HEAD_DIM
    pieces = (jnp.where(low, k, 0.0), jnp.where(low, 0.0, kr),
              jnp.where(low, kr, 0.0), jnp.where(low, 0.0, k))
    for i, piece in enumerate(pieces):
        kp_ref[:, i * LANE:(i + 1) * LANE] = piece.astype(BF16)
    vt_ref[...] = _dot_nt(wvt_ref[...], xn).astype(BF16)


def _gla_inproj_kernel(x_ref, g_ref, w_ref, wglr_ref, wgk2_ref, bgk_ref, tril_ref, o_ref, el_ref,
                       *, kd, vd, hk):
    tm = x_ref.shape[0]
    c = GLA_CHUNK
    xn = _rms(x_ref[...], g_ref[...]).astype(BF16)
    glr = _dot(xn, wglr_ref[...])
    z = _dot(glr.astype(BF16), wgk2_ref[...]) + bgk_ref[...]
    gk = (jnp.minimum(z, 0.0) - jnp.log1p(jnp.exp(-jnp.abs(z)))) * (LOG2E / GLA_GATE_NORM)
    hi = gk.astype(BF16)
    mid = (gk - hi.astype(F32)).astype(BF16)
    bcum = _dot(tril_ref[...], hi) + _dot(tril_ref[...], mid)
    for j in range(0, vd, COL_CHUNK):
        o_ref[:, j:j + COL_CHUNK] = _dot(xn, w_ref[:, 2 * kd + j:2 * kd + j + COL_CHUNK]).astype(BF16)
    for j in range(vd, 2 * vd, COL_CHUNK):
        gt = _dot(xn, w_ref[:, 2 * kd + j:2 * kd + j + COL_CHUNK])
        o_ref[:, j:j + COL_CHUNK] = (gt / (1.0 + jnp.exp(-gt))).astype(BF16)
    q = _dot(xn, w_ref[:, :kd])
    o_ref[:, 2 * vd:2 * vd + kd] = ((q * (hk ** -0.5)) * jnp.exp2(bcum)).astype(BF16)
    k = _dot(xn, w_ref[:, kd:2 * kd])
    kt = k * jnp.exp2(-bcum)
    o_ref[:, 2 * vd + kd:2 * vd + 2 * kd] = kt.astype(BF16)
    for i in range(tm // c):
        el = jnp.exp2(bcum[(i + 1) * c - 1:(i + 1) * c, :])
        el_ref[i:i + 1, :] = el
        o_ref[i * c:(i + 1) * c, 2 * vd + 2 * kd:2 * vd + 3 * kd] = (kt[i * c:(i + 1) * c] * el).astype(BF16)


def _gla_inproj(x, layer, mix_layer, g_mix, w_in, w_glr, w_gk2, b_gk, kd, vd):
    n, d = x.shape
    n_out = 2 * vd + 3 * kd
    per_tile = ROW_TILE // GLA_CHUNK
    r = np.arange(ROW_TILE)
    tril = jnp.asarray((r[:, None] >= r[None, :]) & (r[:, None] // GLA_CHUNK == r[None, :] // GLA_CHUNK),
                       dtype=BF16)
    return pl.pallas_call(
        functools.partial(_gla_inproj_kernel, kd=kd, vd=vd, hk=kd // GLA_HEADS),
        grid=(n // ROW_TILE,),
        in_specs=[pl.BlockSpec((ROW_TILE, d), lambda i: (i, 0)),
                  _layer(g_mix, mix_layer), _layer(w_in, layer), _layer(w_glr, layer),
                  _layer(w_gk2, layer), _layer(b_gk, layer), _resident(tril.shape)],
        out_specs=[pl.BlockSpec((ROW_TILE, n_out), lambda i: (i, 0)),
                   pl.BlockSpec((per_tile, kd), lambda i: (i, 0))],
        out_shape=[jax.ShapeDtypeStruct((n, n_out), BF16),
                   jax.ShapeDtypeStruct((n // GLA_CHUNK, kd), F32)],
        compiler_params=_params("parallel"),
        name="gla_inproj",
    )(x, g_mix, w_in, w_glr, w_gk2, b_gk, tril)


def _gla_kernel(v_ref, sg_ref, qt_ref, kt_ref, kd_ref, el_ref, gon_ref, o_ref, st_ref, *, hk, hv):
    @pl.when(pl.program_id(1) == 0)
    def _():
        st_ref[...] = jnp.zeros_like(st_ref)

    c = GLA_CHUNK
    hc = GLA_HEADS * c
    ri = lax.broadcasted_iota(jnp.int32, (hc, hc), 0)
    ci = lax.broadcasted_iota(jnp.int32, (hc, hc), 1)
    causal = (ri >= ci) & (ri // c == ci // c)
    gon = gon_ref[...]
    for i in range(v_ref.shape[0] // c):
        rows = slice(i * c, (i + 1) * c)
        el = el_ref[i:i + 1, :]
        heads_k = [slice(h * hk, (h + 1) * hk) for h in range(GLA_HEADS)]
        heads_v = [slice(h * hv, (h + 1) * hv) for h in range(GLA_HEADS)]
        q_stack = jnp.concatenate([qt_ref[rows, ks] for ks in heads_k], axis=0)
        k_stack = jnp.concatenate([kt_ref[rows, ks] for ks in heads_k], axis=0)
        v_stack = jnp.concatenate([v_ref[rows, vs] for vs in heads_v], axis=0)
        attn = jnp.where(causal, _dot_nt(q_stack, k_stack), 0.0).astype(BF16)
        o_intra = _dot(attn, v_stack)
        for h in range(GLA_HEADS):
            ks, vs = heads_k[h], heads_v[h]
            st = st_ref[h]
            o = o_intra[h * c:(h + 1) * c] + _dot_nt(qt_ref[rows, ks], st.astype(BF16))
            st_ref[h] = st * el[:, ks] + _dot_tn(v_ref[rows, vs], kd_ref[rows, ks])
            o = o * lax.rsqrt(jnp.mean(o * o, axis=-1, keepdims=True) + EPS) * gon
            o_ref[rows, vs] = (o * sg_ref[rows, vs].astype(F32)).astype(BF16)


def _gla_core(proj, el, gon, batch, seq, kd, vd):
    n = proj.shape[0]
    hk = kd // GLA_HEADS
    hv = vd // GLA_HEADS
    proj3 = proj.reshape(batch, seq, proj.shape[1])
    el3 = el.reshape(batch, seq // GLA_CHUNK, kd)
    kblk = 2 * vd // kd
    out = pl.pallas_call(
        functools.partial(_gla_kernel, hk=hk, hv=hv),
        grid=(batch, seq // GLA_ROWS),
        in_specs=[pl.BlockSpec((None, GLA_ROWS, vd), lambda b, t: (b, t, 0)),
                  pl.BlockSpec((None, GLA_ROWS, vd), lambda b, t: (b, t, 1)),
                  pl.BlockSpec((None, GLA_ROWS, kd), lambda b, t: (b, t, kblk)),
                  pl.BlockSpec((None, GLA_ROWS, kd), lambda b, t: (b, t, kblk + 1)),
                  pl.BlockSpec((None, GLA_ROWS, kd), lambda b, t: (b, t, kblk + 2)),
                  pl.BlockSpec((None, GLA_ROWS // GLA_CHUNK, kd), lambda b, t: (b, t, 0)),
                  pl.BlockSpec((1, hv), lambda b, t: (0, 0))],
        out_specs=pl.BlockSpec((None, GLA_ROWS, vd), lambda b, t: (b, t, 0)),
        out_shape=jax.ShapeDtypeStruct((batch, seq, vd), BF16),
        scratch_shapes=[pltpu.VMEM((GLA_HEADS, hv, hk), F32)],
        compiler_params=_params("parallel", "arbitrary"),
        name="gla_core",
    )(proj3, proj3, proj3, proj3, proj3, el3, gon)
    return out.reshape(n, vd)


def _post_kernel(h_ref, o_ref, wo_ref, g_ref, wup_ref, wdn_ref, *rest, tail):
    h1 = h_ref[...] + _dot(o_ref[...], wo_ref[...])
    hn = _rms(h1, g_ref[...]).astype(BF16)
    acc = h1
    d_ff = wup_ref.shape[1]
    for j in range(0, d_ff, COL_CHUNK):
        u = jnp.maximum(_dot(hn, wup_ref[:, j:j + COL_CHUNK]), 0.0)
        acc = acc + _dot((u * u).astype(BF16), wdn_ref[j:j + COL_CHUNK, :])
    if tail == "none":
        (out_ref,) = rest
    elif tail == "final":
        gfin_ref, out_ref = rest
        acc = _rms(acc, gfin_ref[...])
    else:
        if tail == "kvq":
            gq_ref, wq_ref, gkv_ref, wk_ref, wvt_ref, out_ref, q_ref, kp_ref, vt_ref = rest
        else:
            gq_ref, wq_ref, out_ref, q_ref = rest
        xhat = acc * lax.rsqrt(jnp.mean(acc * acc, axis=-1, keepdims=True) + EPS)
        _store_q((xhat * gq_ref[...]).astype(BF16), wq_ref, q_ref)
        if tail == "kvq":
            _store_kv((xhat * gkv_ref[...]).astype(BF16), wk_ref, wvt_ref, kp_ref, vt_ref)
    out_ref[...] = acc


def _post(h, o, params, tail, n_q_cols=0, n_vt_rows=0):
    n, d = h.shape
    rows = lambda width: pl.BlockSpec((ROW_TILE, width), lambda i: (i, 0))
    out_specs = [rows(d)]
    out_shape = [jax.ShapeDtypeStruct((n, d), F32)]
    if tail in ("q", "kvq"):
        out_specs.append(rows(n_q_cols))
        out_shape.append(jax.ShapeDtypeStruct((n, n_q_cols), BF16))
    if tail == "kvq":
        out_specs += [rows(4 * LANE), pl.BlockSpec((n_vt_rows, ROW_TILE), lambda i: (0, i))]
        out_shape += [jax.ShapeDtypeStruct((n, 4 * LANE), BF16),
                      jax.ShapeDtypeStruct((n_vt_rows, n), BF16)]
    return pl.pallas_call(
        functools.partial(_post_kernel, tail=tail),
        grid=(n // ROW_TILE,),
        in_specs=[rows(d), rows(o.shape[1])] + [spec for _, spec in params],
        out_specs=out_specs,
        out_shape=out_shape,
        compiler_params=_params("parallel"),
        name="post_mlp",
    )(h, o, *[arr for arr, _ in params])


def _rel_buckets_t():
    j = np.arange(2 * BLOCK)[:, None]
    i = np.arange(BLOCK)[None, :]
    n = np.maximum(i + BLOCK - j, 0)
    max_exact = REL_BUCKETS // 2
    large = max_exact + (np.log(np.maximum(n, 1) / max_exact)
                         / np.log(REL_MAX_DIST / max_exact)
                         * (REL_BUCKETS - max_exact)).astype(np.int32)
    large = np.minimum(large, REL_BUCKETS - 1)
    return np.where(n < max_exact, n, large).astype(np.int32)


def _swa_kernel(rel_ref, sink_ref, bucket_ref, q_ref, kpp_ref, kpc_ref, vtp_ref, vtc_ref, o_ref,
                bias_ref, *, n_kv):
    n_q = bias_ref.shape[0]
    hd = SWA_HEAD_DIM
    pairs_per_kv = SWA_GROUP // 2

    @pl.when((pl.program_id(0) == 0) & (pl.program_id(1) == 0))
    def _():
        bucket = bucket_ref[...]

        def per_head(h, carry):
            def per_bucket(b, acc):
                return jnp.where(bucket == b, rel_ref[b, h] * LOG2E, acc)
            bias_ref[h] = lax.fori_loop(0, REL_BUCKETS, per_bucket,
                                        jnp.zeros(bucket.shape, F32))
            return carry

        lax.fori_loop(0, n_q, per_head, 0)

    kj = lax.broadcasted_iota(jnp.int32, (2 * BLOCK, BLOCK), 0)
    qi = lax.broadcasted_iota(jnp.int32, (2 * BLOCK, BLOCK), 1)
    dist = qi + BLOCK - kj
    in_window = (dist >= 0) & (dist < WINDOW)
    valid_first = in_window & ((pl.program_id(1) > 0) | (kj >= BLOCK))
    zeros = jnp.zeros((hd, 2 * BLOCK), BF16)
    for qb in range(SWA_QBLOCKS):
        rows = slice(qb * BLOCK, (qb + 1) * BLOCK)
        prev_rows = slice((qb - 1) * BLOCK, qb * BLOCK)
        valid = valid_first if qb == 0 else in_window
        for hkv in range(n_kv):
            def k_band(col):
                cols = slice(col * LANE, (col + 1) * LANE)
                prev = kpp_ref[:, cols] if qb == 0 else kpc_ref[prev_rows, cols]
                return [prev, kpc_ref[rows, cols]]
            k_stack = jnp.concatenate(k_band(2 * hkv) + k_band(2 * hkv + 1), axis=0)
            vrow = slice(hkv * hd, (hkv + 1) * hd)
            if qb == 0:
                vt_band = jnp.concatenate([vtp_ref[vrow, :], vtc_ref[vrow, :BLOCK]], axis=1)
            else:
                vt_band = vtc_ref[vrow, (qb - 1) * BLOCK:(qb + 1) * BLOCK]
            v_bd = jnp.concatenate([jnp.concatenate([vt_band, zeros], axis=1),
                                    jnp.concatenate([zeros, vt_band], axis=1)], axis=0)
            for pr in range(pairs_per_kv):
                pair = hkv * pairs_per_kv + pr
                cols = slice(pair * LANE, (pair + 1) * LANE)
                st = _dot_nt(k_stack, q_ref[rows, cols])
                ps, rden = [], []
                for e in range(2):
                    hq = 2 * pair + e
                    s = jnp.where(valid, st[e * 2 * BLOCK:(e + 1) * 2 * BLOCK] + bias_ref[hq], NEG)
                    sink = sink_ref[hq] * LOG2E
                    m = jnp.maximum(jnp.max(s, axis=0, keepdims=True), sink)
                    p = jnp.exp2(s - m)
                    den = jnp.sum(p, axis=0, keepdims=True) + jnp.exp2(sink - m)
                    ps.append(p.astype(BF16))
                    rden.append(jnp.broadcast_to(1.0 / den, (hd, BLOCK)))
                ot = _dot(v_bd, jnp.concatenate(ps, axis=0)) * jnp.concatenate(rden, axis=0)
                o_ref[rows, cols] = ot.T.astype(BF16)


def _swa_core(q, kpad, vt, rel_table, sinks, batch, seq):
    n, dq = q.shape
    n_q = dq // SWA_HEAD_DIM
    n_kv = n_q // SWA_GROUP
    rows = SWA_QBLOCKS * BLOCK
    steps = seq // rows
    q3 = q.reshape(batch, seq, dq)
    kp3 = kpad.reshape(batch, seq, kpad.shape[1])
    bucket = jnp.asarray(_rel_buckets_t())
    smem = pl.BlockSpec(memory_space=pltpu.SMEM)
    prev = lambda t: jnp.maximum(SWA_QBLOCKS * t - 1, 0)
    out = pl.pallas_call(
        functools.partial(_swa_kernel, n_kv=n_kv),
        grid=(batch, steps),
        in_specs=[smem, smem,
                  pl.BlockSpec((2 * BLOCK, BLOCK), lambda b, t: (0, 0)),
                  pl.BlockSpec((None, rows, dq), lambda b, t: (b, t, 0)),
                  pl.BlockSpec((None, BLOCK, kpad.shape[1]), lambda b, t: (b, prev(t), 0)),
                  pl.BlockSpec((None, rows, kpad.shape[1]), lambda b, t: (b, t, 0)),
                  pl.BlockSpec((vt.shape[0], BLOCK),
                               lambda b, t: (0, b * steps * SWA_QBLOCKS + prev(t))),
                  pl.BlockSpec((vt.shape[0], rows), lambda b, t: (0, b * steps + t))],
        out_specs=pl.BlockSpec((None, rows, dq), lambda b, t: (b, t, 0)),
        out_shape=jax.ShapeDtypeStruct((batch, seq, dq), BF16),
        scratch_shapes=[pltpu.VMEM((n_q, 2 * BLOCK, BLOCK), F32)],
        compiler_params=_params("arbitrary", "arbitrary"),
        name="swa_core",
    )(rel_table, sinks, bucket, q3, kp3, kp3, vt, vt)
    return out.reshape(n, dq)


def kernel(x, a_w_in, a_w_gk2, a_b_gk, a_onorm, a_w_out, kv_norm, w_kv, b_w_q, b_sinks, b_w_out,
           rel_table, ln_mix, ln_mlp, w_up, w_down, ln_final):
    batch, seq, d = x.shape
    n_a = a_w_in.shape[0]
    n_b = b_w_q.shape[0]
    kd = a_w_gk2.shape[2]
    vd = (a_w_in.shape[2] - GLA_GATE_RANK - 2 * kd) // 2

    depth = n_a + n_b
    assert n_a >= 1
    half = w_kv.shape[1] // 2
    assert half == LANE and 2 * SWA_HEAD_DIM == LANE

    w_in = a_w_in[:, :, :2 * kd + 2 * vd].astype(BF16)
    w_glr = jnp.pad(a_w_in[:, :, 2 * kd + 2 * vd:],
                    ((0, 0), (0, 0), (0, LANE - GLA_GATE_RANK))).astype(BF16)
    w_gk2 = jnp.pad(a_w_gk2, ((0, 0), (0, LANE - GLA_GATE_RANK), (0, 0))).astype(BF16)
    w_out = {True: a_w_out.astype(BF16), False: b_w_out.astype(BF16)}
    w_q = b_w_q.astype(BF16)
    w_k = w_kv[:, :half].astype(BF16)
    w_vt = w_kv[:, half:].T.astype(BF16)
    w_up_b = w_up.astype(BF16)
    w_down_b = w_down.astype(BF16)
    g_mix = ln_mix.reshape(depth, 1, d)
    g_mlp = ln_mlp.reshape(depth, 1, d)
    g_kv = kv_norm.reshape(1, d)
    g_final = ln_final.reshape(1, d)
    b_gk = a_b_gk.reshape(n_a, 1, kd)

    h = x.reshape(batch * seq, d)
    for layer in range(depth):
        is_gla = layer < n_a
        j = layer if is_gla else layer - n_a
        if is_gla:
            proj, el = _gla_inproj(h, j, layer, g_mix, w_in, w_glr, w_gk2, b_gk, kd, vd)
            o = _gla_core(proj, el, a_onorm[j].reshape(1, -1), batch, seq, kd, vd)
        else:
            o = _swa_core(q, kpad, vt, rel_table, b_sinks[j], batch, seq)
        params = [(w_out[is_gla], _layer(w_out[is_gla], j)), (g_mlp, _layer(g_mlp, layer)),
                  (w_up_b, _layer(w_up_b, layer)), (w_down_b, _layer(w_down_b, layer))]
        if layer == depth - 1:
            (h,) = _post(h, o, params + [(g_final, _resident(g_final.shape))], "final")
        elif layer + 1 < n_a:
            (h,) = _post(h, o, params, "none")
        else:
            params += [(g_mix, _layer(g_mix, layer + 1)), (w_q, _layer(w_q, layer + 1 - n_a))]
            if layer + 1 == n_a:
                params += [(g_kv, _resident(g_kv.shape)), (w_k, _resident(w_k.shape)),
                           (w_vt, _resident(w_vt.shape))]
                h, q, kpad, vt = _post(h, o, params, "kvq", w_q.shape[2], w_vt.shape[0])
            else:
                h, q = _post(h, o, params, "q", w_q.shape[2])
    return h.reshape(batch, seq, d)
```

```python
import functools
import math

import numpy as np
import jax
import jax.numpy as jnp
from jax import lax
from jax.experimental import pallas as pl
from jax.experimental.pallas import tpu as pltpu

F32 = jnp.float32
BF16 = jnp.bfloat16

EPS = 1e-6
NEG = -1e30
LOG2E = math.log2(math.e)

GLA_HEADS = 4
GLA_GATE_RANK = 16
GLA_GATE_NORM = 16.0
GLA_CHUNK = 64

SWA_HEAD_DIM = 64
SWA_GROUP = 8
WINDOW = 128
BLOCK = 128
REL_BUCKETS = 32
REL_MAX_DIST = 128

LANE = 128
ROW_TILE = 512
COL_CHUNK = 512
GLA_ROWS = 512
SWA_QBLOCKS = 8
VMEM_LIMIT = 56 * 1024 * 1024

_NT = (((1,), (1,)), ((), ()))
_TN = (((0,), (0,)), ((), ()))


def _rms(x, g):
    ms = jnp.mean(x * x, axis=-1, keepdims=True)
    return x * lax.rsqrt(ms + EPS) * g


def _dot(a, b):
    return jnp.dot(a, b, preferred_element_type=F32)


def _dot_nt(a, b):
    return lax.dot_general(a, b, _NT, preferred_element_type=F32)


def _dot_tn(a, b):
    return lax.dot_general(a, b, _TN, preferred_element_type=F32)


def _params(*sem):
    return pltpu.CompilerParams(dimension_semantics=sem, vmem_limit_bytes=VMEM_LIMIT)


def _resident(shape):
    zeros = (0,) * len(shape)
    return pl.BlockSpec(shape, lambda *_: zeros, pipeline_mode=pl.Buffered(1))


def _layer(stacked, layer):
    idx = (layer,) + (0,) * (stacked.ndim - 1)
    return pl.BlockSpec((None,) + stacked.shape[1:], lambda *_: idx, pipeline_mode=pl.Buffered(1))


class _Cast:
    def __init__(self, stack, layer, cols, steps):
        rows, width = stack.shape[1] // steps, stack.shape[2]
        self.stack = stack
        self.in_spec = pl.BlockSpec((None, rows, width), lambda i: (layer, i, 0))
        self.out_spec = pl.BlockSpec((rows, cols), lambda i: (i, 0))
        self.out_shape = jax.ShapeDtypeStruct((stack.shape[1], cols), BF16)


def _run_casts(refs, n_in, n_cast):
    n_out = len(refs) - n_in - 2 * n_cast
    for src, dst in zip(refs[n_in:n_in + n_cast], refs[n_in + n_cast + n_out:]):
        dst[...] = src[:, :dst.shape[-1]].astype(BF16)
    return refs[:n_in], refs[n_in + n_cast:n_in + n_cast + n_out]


def _store_q(xn, wq_ref, q_ref):
    scale = SWA_HEAD_DIM ** -0.5 * LOG2E
    for j in range(0, q_ref.shape[-1], COL_CHUNK):
        q_ref[:, j:j + COL_CHUNK] = (_dot(xn, wq_ref[:, j:j + COL_CHUNK]) * scale).astype(BF16)


def _store_kv(xn, wk_ref, wvt_ref, kp_ref, vt_ref):
    k = _dot(xn, wk_ref[...])
    kr = pltpu.roll(k, SWA_HEAD_DIM, axis=1)
    low = lax.broadcasted_iota(jnp.int32, k.shape, 1) < SWA_HEAD_DIM
    pieces = (jnp.where(low, k, 0.0), jnp.where(low, 0.0, kr),
              jnp.where(low, kr, 0.0), jnp.where(low, 0.0, k))
    for i, piece in enumerate(pieces):
        kp_ref[:, i * LANE:(i + 1) * LANE] = piece.astype(BF16)
    vt_ref[...] = _dot_nt(wvt_ref[...], xn).astype(BF16)


def _gla_inproj_kernel(*refs, kd, vd, hk, n_cast):
    (x_ref, g_ref, w_ref, wglr_ref, wgk2_ref, bgk_ref, tril_ref), (o_ref, el_ref) = _run_casts(
        refs, 7, n_cast)
    tm = x_ref.shape[0]
    c = GLA_CHUNK
    xn = _rms(x_ref[...], g_ref[...]).astype(BF16)
    glr = _dot(xn, wglr_ref[...])
    z = _dot(glr.astype(BF16), wgk2_ref[...]) + bgk_ref[...]
    gk = (jnp.minimum(z, 0.0) - jnp.log1p(jnp.exp(-jnp.abs(z)))) * (LOG2E / GLA_GATE_NORM)
    hi = gk.astype(BF16)
    mid = (gk - hi.astype(F32)).astype(BF16)
    bcum = _dot(tril_ref[...], hi) + _dot(tril_ref[...], mid)
    for j in range(0, vd, COL_CHUNK):
        o_ref[:, j:j + COL_CHUNK] = _dot(xn, w_ref[:, 2 * kd + j:2 * kd + j + COL_CHUNK]).astype(BF16)
    for j in range(vd, 2 * vd, COL_CHUNK):
        gt = _dot(xn, w_ref[:, 2 * kd + j:2 * kd + j + COL_CHUNK])
        o_ref[:, j:j + COL_CHUNK] = (gt / (1.0 + jnp.exp(-gt))).astype(BF16)
    q = _dot(xn, w_ref[:, :kd])
    o_ref[:, 2 * vd:2 * vd + kd] = ((q * (hk ** -0.5)) * jnp.exp2(bcum)).astype(BF16)
    k = _dot(xn, w_ref[:, kd:2 * kd])
    kt = k * jnp.exp2(-bcum)
    o_ref[:, 2 * vd + kd:2 * vd + 2 * kd] = kt.astype(BF16)
    for i in range(tm // c):
        el = jnp.exp2(bcum[(i + 1) * c - 1:(i + 1) * c, :])
        el_ref[i:i + 1, :] = el
        o_ref[i * c:(i + 1) * c, 2 * vd + 2 * kd:2 * vd + 3 * kd] = (kt[i * c:(i + 1) * c] * el).astype(BF16)


def _gla_inproj(x, layer, mix_layer, g_mix, w_in, w_glr, w_gk2, b_gk, kd, vd, casts):
    n, d = x.shape
    n_out = 2 * vd + 3 * kd
    per_tile = ROW_TILE // GLA_CHUNK
    r = np.arange(ROW_TILE)
    tril = jnp.asarray((r[:, None] >= r[None, :]) & (r[:, None] // GLA_CHUNK == r[None, :] // GLA_CHUNK),
                       dtype=BF16)
    proj, el, *cast_out = pl.pallas_call(
        functools.partial(_gla_inproj_kernel, kd=kd, vd=vd, hk=kd // GLA_HEADS, n_cast=len(casts)),
        grid=(n // ROW_TILE,),
        in_specs=[pl.BlockSpec((ROW_TILE, d), lambda i: (i, 0)),
                  _layer(g_mix, mix_layer), _resident(w_in.shape), _layer(w_glr, layer),
                  _layer(w_gk2, layer), _layer(b_gk, layer), _resident(tril.shape)]
                 + [c.in_spec for c in casts],
        out_specs=[pl.BlockSpec((ROW_TILE, n_out), lambda i: (i, 0)),
                   pl.BlockSpec((per_tile, kd), lambda i: (i, 0))] + [c.out_spec for c in casts],
        out_shape=[jax.ShapeDtypeStruct((n, n_out), BF16),
                   jax.ShapeDtypeStruct((n // GLA_CHUNK, kd), F32)] + [c.out_shape for c in casts],
        compiler_params=_params("parallel"),
        name="gla_inproj",
    )(x, g_mix, w_in, w_glr, w_gk2, b_gk, tril, *[c.stack for c in casts])
    return proj, el, cast_out


def _gla_kernel(v_ref, sg_ref, qt_ref, kt_ref, kd_ref, el_ref, gon_ref, o_ref, st_ref, *, hk, hv):
    @pl.when(pl.program_id(1) == 0)
    def _():
        st_ref[...] = jnp.zeros_like(st_ref)

    c = GLA_CHUNK
    hc = GLA_HEADS * c
    ri = lax.broadcasted_iota(jnp.int32, (hc, hc), 0)
    ci = lax.broadcasted_iota(jnp.int32, (hc, hc), 1)
    causal = (ri >= ci) & (ri // c == ci // c)
    gon = gon_ref[...]
    for i in range(v_ref.shape[0] // c):
        rows = slice(i * c, (i + 1) * c)
        el = el_ref[i:i + 1, :]
        heads_k = [slice(h * hk, (h + 1) * hk) for h in range(GLA_HEADS)]
        heads_v = [slice(h * hv, (h + 1) * hv) for h in range(GLA_HEADS)]
        q_stack = jnp.concatenate([qt_ref[rows, ks] for ks in heads_k], axis=0)
        k_stack = jnp.concatenate([kt_ref[rows, ks] for ks in heads_k], axis=0)
        v_stack = jnp.concatenate([v_ref[rows, vs] for vs in heads_v], axis=0)
        attn = jnp.where(causal, _dot_nt(q_stack, k_stack), 0.0).astype(BF16)
        o_intra = _dot(attn, v_stack)
        for h in range(GLA_HEADS):
            ks, vs = heads_k[h], heads_v[h]
            st = st_ref[h]
            o = o_intra[h * c:(h + 1) * c] + _dot_nt(qt_ref[rows, ks], st.astype(BF16))
            st_ref[h] = st * el[:, ks] + _dot_tn(v_ref[rows, vs], kd_ref[rows, ks])
            o = o * lax.rsqrt(jnp.mean(o * o, axis=-1, keepdims=True) + EPS) * gon
            o_ref[rows, vs] = (o * sg_ref[rows, vs].astype(F32)).astype(BF16)


def _gla_core(proj, el, gon, batch, seq, kd, vd):
    n = proj.shape[0]
    hk = kd // GLA_HEADS
    hv = vd // GLA_HEADS
    proj3 = proj.reshape(batch, seq, proj.shape[1])
    el3 = el.reshape(batch, seq // GLA_CHUNK, kd)
    kblk = 2 * vd // kd
    out = pl.pallas_call(
        functools.partial(_gla_kernel, hk=hk, hv=hv),
        grid=(batch, seq // GLA_ROWS),
        in_specs=[pl.BlockSpec((None, GLA_ROWS, vd), lambda b, t: (b, t, 0)),
                  pl.BlockSpec((None, GLA_ROWS, vd), lambda b, t: (b, t, 1)),
                  pl.BlockSpec((None, GLA_ROWS, kd), lambda b, t: (b, t, kblk)),
                  pl.BlockSpec((None, GLA_ROWS, kd), lambda b, t: (b, t, kblk + 1)),
                  pl.BlockSpec((None, GLA_ROWS, kd), lambda b, t: (b, t, kblk + 2)),
                  pl.BlockSpec((None, GLA_ROWS // GLA_CHUNK, kd), lambda b, t: (b, t, 0)),
                  pl.BlockSpec((1, hv), lambda b, t: (0, 0))],
        out_specs=pl.BlockSpec((None, GLA_ROWS, vd), lambda b, t: (b, t, 0)),
        out_shape=jax.ShapeDtypeStruct((batch, seq, vd), BF16),
        scratch_shapes=[pltpu.VMEM((GLA_HEADS, hv, hk), F32)],
        compiler_params=_params("parallel", "arbitrary"),
        name="gla_core",
    )(proj3, proj3, proj3, proj3, proj3, el3, gon)
    return out.reshape(n, vd)


_POST_TAIL_INPUTS = {"none": 0, "final": 1, "q": 2, "kvq": 5}


def _post_kernel(*refs, tail, n_cast):
    ins, outs = _run_casts(refs, 6 + _POST_TAIL_INPUTS[tail], n_cast)
    h_ref, o_ref, wo_ref, g_ref, wup_ref, wdn_ref = ins[:6]
    rest = ins[6:] + outs
    h1 = h_ref[...] + _dot(o_ref[...], wo_ref[...])
    hn = _rms(h1, g_ref[...]).astype(BF16)
    acc = h1
    d_ff = wup_ref.shape[1]
    for j in range(0, d_ff, COL_CHUNK):
        u = jnp.maximum(_dot(hn, wup_ref[:, j:j + COL_CHUNK]), 0.0)
        acc = acc + _dot((u * u).astype(BF16), wdn_ref[j:j + COL_CHUNK, :])
    if tail == "none":
        (out_ref,) = rest
    elif tail == "final":
        gfin_ref, out_ref = rest
        acc = _rms(acc, gfin_ref[...])
    else:
        if tail == "kvq":
            gq_ref, wq_ref, gkv_ref, wk_ref, wvt_ref, out_ref, q_ref, kp_ref, vt_ref = rest
        else:
            gq_ref, wq_ref, out_ref, q_ref = rest
        xhat = acc * lax.rsqrt(jnp.mean(acc * acc, axis=-1, keepdims=True) + EPS)
        _store_q((xhat * gq_ref[...]).astype(BF16), wq_ref, q_ref)
        if tail == "kvq":
            _store_kv((xhat * gkv_ref[...]).astype(BF16), wk_ref, wvt_ref, kp_ref, vt_ref)
    out_ref[...] = acc


def _post(h, o, params, tail, casts, n_q_cols=0, n_vt_rows=0):
    n, d = h.shape
    assert len(params) == 4 + _POST_TAIL_INPUTS[tail]
    rows = lambda width: pl.BlockSpec((ROW_TILE, width), lambda i: (i, 0))
    out_specs = [rows(d)]
    out_shape = [jax.ShapeDtypeStruct((n, d), F32)]
    if tail in ("q", "kvq"):
        out_specs.append(rows(n_q_cols))
        out_shape.append(jax.ShapeDtypeStruct((n, n_q_cols), BF16))
    if tail == "kvq":
        out_specs += [rows(4 * LANE), pl.BlockSpec((n_vt_rows, ROW_TILE), lambda i: (0, i))]
        out_shape += [jax.ShapeDtypeStruct((n, 4 * LANE), BF16),
                      jax.ShapeDtypeStruct((n_vt_rows, n), BF16)]
    n_main = len(out_specs)
    outs = pl.pallas_call(
        functools.partial(_post_kernel, tail=tail, n_cast=len(casts)),
        grid=(n // ROW_TILE,),
        in_specs=[rows(d), rows(o.shape[1])] + [spec for _, spec in params]
                 + [c.in_spec for c in casts],
        out_specs=out_specs + [c.out_spec for c in casts],
        out_shape=out_shape + [c.out_shape for c in casts],
        compiler_params=_params("parallel"),
        name="post_mlp",
    )(h, o, *[arr for arr, _ in params], *[c.stack for c in casts])
    return outs[:n_main], outs[n_main:]


def _rel_buckets_t():
    j = np.arange(2 * BLOCK)[:, None]
    i = np.arange(BLOCK)[None, :]
    n = np.maximum(i + BLOCK - j, 0)
    max_exact = REL_BUCKETS // 2
    large = max_exact + (np.log(np.maximum(n, 1) / max_exact)
                         / np.log(REL_MAX_DIST / max_exact)
                         * (REL_BUCKETS - max_exact)).astype(np.int32)
    large = np.minimum(large, REL_BUCKETS - 1)
    return np.where(n < max_exact, n, large).astype(np.int32)


def _swa_kernel(rel_ref, sink_ref, bucket_ref, q_ref, kpp_ref, kpc_ref, vtp_ref, vtc_ref, o_ref,
                bias_ref, *, n_kv):
    n_q = bias_ref.shape[0]
    hd = SWA_HEAD_DIM
    pairs_per_kv = SWA_GROUP // 2

    @pl.when((pl.program_id(0) == 0) & (pl.program_id(1) == 0))
    def _():
        bucket = bucket_ref[...]

        def per_head(h, carry):
            def per_bucket(b, acc):
                return jnp.where(bucket == b, rel_ref[b, h] * LOG2E, acc)
            bias_ref[h] = lax.fori_loop(0, REL_BUCKETS, per_bucket,
                                        jnp.zeros(bucket.shape, F32))
            return carry

        lax.fori_loop(0, n_q, per_head, 0)

    kj = lax.broadcasted_iota(jnp.int32, (2 * BLOCK, BLOCK), 0)
    qi = lax.broadcasted_iota(jnp.int32, (2 * BLOCK, BLOCK), 1)
    dist = qi + BLOCK - kj
    in_window = (dist >= 0) & (dist < WINDOW)
    valid_first = in_window & ((pl.program_id(1) > 0) | (kj >= BLOCK))
    zeros = jnp.zeros((hd, 2 * BLOCK), BF16)
    for qb in range(SWA_QBLOCKS):
        rows = slice(qb * BLOCK, (qb + 1) * BLOCK)
        prev_rows = slice((qb - 1) * BLOCK, qb * BLOCK)
        valid = valid_first if qb == 0 else in_window
        for hkv in range(n_kv):
            def k_band(col):
                cols = slice(col * LANE, (col + 1) * LANE)
                prev = kpp_ref[:, cols] if qb == 0 else kpc_ref[prev_rows, cols]
                return [prev, kpc_ref[rows, cols]]
            k_stack = jnp.concatenate(k_band(2 * hkv) + k_band(2 * hkv + 1), axis=0)
            vrow = slice(hkv * hd, (hkv + 1) * hd)
            if qb == 0:
                vt_band = jnp.concatenate([vtp_ref[vrow, :], vtc_ref[vrow, :BLOCK]], axis=1)
            else:
                vt_band = vtc_ref[vrow, (qb - 1) * BLOCK:(qb + 1) * BLOCK]
            v_bd = jnp.concatenate([jnp.concatenate([vt_band, zeros], axis=1),
                                    jnp.concatenate([zeros, vt_band], axis=1)], axis=0)
            for pr in range(pairs_per_kv):
                pair = hkv * pairs_per_kv + pr
                cols = slice(pair * LANE, (pair + 1) * LANE)
                st = _dot_nt(k_stack, q_ref[rows, cols])
                ps, rden = [], []
                for e in range(2):
                    hq = 2 * pair + e
                    s = jnp.where(valid, st[e * 2 * BLOCK:(e + 1) * 2 * BLOCK] + bias_ref[hq], NEG)
                    sink = sink_ref[hq] * LOG2E
                    m = jnp.maximum(jnp.max(s, axis=0, keepdims=True), sink)
                    p = jnp.exp2(s - m)
                    den = jnp.sum(p, axis=0, keepdims=True) + jnp.exp2(sink - m)
                    ps.append(p.astype(BF16))
                    rden.append(jnp.broadcast_to(1.0 / den, (hd, BLOCK)))
                ot = _dot(v_bd, jnp.concatenate(ps, axis=0)) * jnp.concatenate(rden, axis=0)
                o_ref[rows, cols] = ot.T.astype(BF16)


def _swa_core(q, kpad, vt, rel_table, sinks, batch, seq):
    n, dq = q.shape
    n_q = dq // SWA_HEAD_DIM
    n_kv = n_q // SWA_GROUP
    rows = SWA_QBLOCKS * BLOCK
    steps = seq // rows
    q3 = q.reshape(batch, seq, dq)
    kp3 = kpad.reshape(batch, seq, kpad.shape[1])
    bucket = jnp.asarray(_rel_buckets_t())
    smem = pl.BlockSpec(memory_space=pltpu.SMEM)
    prev = lambda t: jnp.maximum(SWA_QBLOCKS * t - 1, 0)
    out = pl.pallas_call(
        functools.partial(_swa_kernel, n_kv=n_kv),
        grid=(batch, steps),
        in_specs=[smem, smem,
                  pl.BlockSpec((2 * BLOCK, BLOCK), lambda b, t: (0, 0)),
                  pl.BlockSpec((None, rows, dq), lambda b, t: (b, t, 0)),
                  pl.BlockSpec((None, BLOCK, kpad.shape[1]), lambda b, t: (b, prev(t), 0)),
                  pl.BlockSpec((None, rows, kpad.shape[1]), lambda b, t: (b, t, 0)),
                  pl.BlockSpec((vt.shape[0], BLOCK),
                               lambda b, t: (0, b * steps * SWA_QBLOCKS + prev(t))),
                  pl.BlockSpec((vt.shape[0], rows), lambda b, t: (0, b * steps + t))],
        out_specs=pl.BlockSpec((None, rows, dq), lambda b, t: (b, t, 0)),
        out_shape=jax.ShapeDtypeStruct((batch, seq, dq), BF16),
        scratch_shapes=[pltpu.VMEM((n_q, 2 * BLOCK, BLOCK), F32)],
        compiler_params=_params("arbitrary", "arbitrary"),
        name="swa_core",
    )(rel_table, sinks, bucket, q3, kp3, kp3, vt, vt)
    return out.reshape(n, dq)


def kernel(x, a_w_in, a_w_gk2, a_b_gk, a_onorm, a_w_out, kv_norm, w_kv, b_w_q, b_sinks, b_w_out,
           rel_table, ln_mix, ln_mlp, w_up, w_down, ln_final):
    batch, seq, d = x.shape
    n_a = a_w_in.shape[0]
    n_b = b_w_q.shape[0]
    kd = a_w_gk2.shape[2]
    vd = (a_w_in.shape[2] - GLA_GATE_RANK - 2 * kd) // 2

    depth = n_a + n_b
    assert n_a >= 1
    half = w_kv.shape[1] // 2
    assert half == LANE and 2 * SWA_HEAD_DIM == LANE

    w_glr = jnp.pad(a_w_in[:, :, 2 * kd + 2 * vd:],
                    ((0, 0), (0, 0), (0, LANE - GLA_GATE_RANK))).astype(BF16)
    w_gk2 = jnp.pad(a_w_gk2, ((0, 0), (0, LANE - GLA_GATE_RANK), (0, 0))).astype(BF16)
    w_vt = w_kv[:, half:].T.astype(BF16)
    g_mix = ln_mix.reshape(depth, 1, d)
    g_mlp = ln_mlp.reshape(depth, 1, d)
    g_kv = kv_norm.reshape(1, d)
    g_final = ln_final.reshape(1, d)
    b_gk = a_b_gk.reshape(n_a, 1, kd)

    calls = []
    for layer in range(depth):
        is_gla = layer < n_a
        j = layer if is_gla else layer - n_a
        if is_gla:
            calls.append(("inproj", layer, {"w_in": (a_w_in, j, 2 * kd + 2 * vd)}))
        needs = {"w_out": (a_w_out if is_gla else b_w_out, j, d),
                 "w_up": (w_up, layer, w_up.shape[2]), "w_down": (w_down, layer, d)}
        if n_a <= layer + 1 < depth:
            needs["w_q"] = (b_w_q, layer + 1 - n_a, b_w_q.shape[2])
        if layer + 1 == n_a:
            needs["w_k"] = (w_kv[None], 0, half)
        calls.append(("post", layer, needs))
    steps = batch * seq // ROW_TILE
    bf = {name: stack[idx, :, :cols].astype(BF16) for name, (stack, idx, cols) in calls[0][2].items()}

    h = x.reshape(batch * seq, d)
    for c, (kind, layer, _) in enumerate(calls):
        is_gla = layer < n_a
        j = layer if is_gla else layer - n_a
        nxt = calls[c + 1][2] if c + 1 < len(calls) else {}
        casts = [_Cast(stack, idx, cols, steps) for stack, idx, cols in nxt.values()]
        if kind == "inproj":
            proj, el, cast_out = _gla_inproj(h, j, layer, g_mix, bf["w_in"], w_glr, w_gk2, b_gk,
                                             kd, vd, casts)
            o = _gla_core(proj, el, a_onorm[j].reshape(1, -1), batch, seq, kd, vd)
        else:
            if not is_gla:
                o = _swa_core(q, kpad, vt, rel_table, b_sinks[j], batch, seq)
            params = [(bf["w_out"], _resident(bf["w_out"].shape)), (g_mlp, _layer(g_mlp, layer)),
                      (bf["w_up"], _resident(bf["w_up"].shape)),
                      (bf["w_down"], _resident(bf["w_down"].shape))]
            if layer == depth - 1:
                (h,), cast_out = _post(h, o, params + [(g_final, _resident(g_final.shape))],
                                       "final", casts)
            elif layer + 1 < n_a:
                (h,), cast_out = _post(h, o, params, "none", casts)
            else:
                params += [(g_mix, _layer(g_mix, layer + 1)), (bf["w_q"], _resident(bf["w_q"].shape))]
                if layer + 1 == n_a:
                    params += [(g_kv, _resident(g_kv.shape)), (bf["w_k"], _resident(bf["w_k"].shape)),
                               (w_vt, _resident(w_vt.shape))]
                    (h, q, kpad, vt), cast_out = _post(h, o, params, "kvq", casts,
                                                       bf["w_q"].shape[1], w_vt.shape[0])
                else:
                    (h, q), cast_out = _post(h, o, params, "q", casts, bf["w_q"].shape[1])
        bf = dict(zip(nxt.keys(), cast_out))
    return h.reshape(batch, seq, d)
```

```python
import functools
import math

import numpy as np
import jax
import jax.numpy as jnp
from jax import lax
from jax.experimental import pallas as pl
from jax.experimental.pallas import tpu as pltpu

F32 = jnp.float32
BF16 = jnp.bfloat16

EPS = 1e-6
NEG = -1e30
LOG2E = math.log2(math.e)

GLA_HEADS = 4
GLA_GATE_RANK = 16
GLA_GATE_NORM = 16.0
GLA_CHUNK = 64

SWA_HEAD_DIM = 64
SWA_GROUP = 8
WINDOW = 128
BLOCK = 128
REL_BUCKETS = 32
REL_MAX_DIST = 128

LANE = 128
ROW_TILE = 512
COL_CHUNK = 512
GLA_ROWS = 512
SWA_QBLOCKS = 8
VMEM_LIMIT = 56 * 1024 * 1024

_NT = (((1,), (1,)), ((), ()))
_TN = (((0,), (0,)), ((), ()))


def _rms(x, g):
    ms = jnp.mean(x * x, axis=-1, keepdims=True)
    return x * lax.rsqrt(ms + EPS) * g


def _dot(a, b):
    return jnp.dot(a, b, preferred_element_type=F32)


def _dot_nt(a, b):
    return lax.dot_general(a, b, _NT, preferred_element_type=F32)


def _dot_tn(a, b):
    return lax.dot_general(a, b, _TN, preferred_element_type=F32)


def _params(*sem):
    return pltpu.CompilerParams(dimension_semantics=sem, vmem_limit_bytes=VMEM_LIMIT)


def _resident(shape):
    zeros = (0,) * len(shape)
    return pl.BlockSpec(shape, lambda *_: zeros, pipeline_mode=pl.Buffered(1))


def _layer(stacked, layer):
    idx = (layer,) + (0,) * (stacked.ndim - 1)
    return pl.BlockSpec((None,) + stacked.shape[1:], lambda *_: idx, pipeline_mode=pl.Buffered(1))


class _Cast:
    def __init__(self, stack, layer, rows, cols, steps):
        assert rows % steps == 0 and cols % LANE == 0
        block = (rows // steps, cols)
        self.stack = stack
        self.in_spec = pl.BlockSpec((None,) + block, lambda i: (layer, i, 0))
        self.out_spec = pl.BlockSpec(block, lambda i: (i, 0))
        self.out_shape = jax.ShapeDtypeStruct((rows, cols), BF16)


def _run_casts(refs, n_in, n_cast):
    n_out = len(refs) - n_in - 2 * n_cast
    for src, dst in zip(refs[n_in:n_in + n_cast], refs[n_in + n_cast + n_out:]):
        dst[...] = src[...].astype(BF16)
    return refs[:n_in], refs[n_in + n_cast:n_in + n_cast + n_out]


def _store_q(xn, wq_ref, q_ref):
    scale = SWA_HEAD_DIM ** -0.5 * LOG2E
    for j in range(0, q_ref.shape[-1], COL_CHUNK):
        q_ref[:, j:j + COL_CHUNK] = (_dot(xn, wq_ref[:, j:j + COL_CHUNK]) * scale).astype(BF16)


def _store_kv(xn, wk_ref, wvt_ref, kp_ref, vt_ref):
    k = _dot(xn, wk_ref[...])
    kr = pltpu.roll(k, SWA_HEAD_DIM, axis=1)
    low = lax.broadcasted_iota(jnp.int32, k.shape, 1) < SWA_HEAD_DIM
    pieces = (jnp.where(low, k, 0.0), jnp.where(low, 0.0, kr),
              jnp.where(low, kr, 0.0), jnp.where(low, 0.0, k))
    for i, piece in enumerate(pieces):
        kp_ref[:, i * LANE:(i + 1) * LANE] = piece.astype(BF16)
    vt_ref[...] = _dot_nt(wvt_ref[...], xn).astype(BF16)


def _gla_inproj_kernel(*refs, kd, vd, hk, n_cast):
    (x_ref, g_ref, w_ref, wglr_ref, wgk2_ref, bgk_ref, tril_ref), (o_ref, el_ref) = _run_casts(
        refs, 7, n_cast)
    tm = x_ref.shape[0]
    c = GLA_CHUNK
    xn = _rms(x_ref[...], g_ref[...]).astype(BF16)
    glr = _dot_nt(xn, wglr_ref[...])
    z = _dot(glr.astype(BF16), wgk2_ref[...]) + bgk_ref[...]
    gk = (jnp.minimum(z, 0.0) - jnp.log1p(jnp.exp(-jnp.abs(z)))) * (LOG2E / GLA_GATE_NORM)
    hi = gk.astype(BF16)
    mid = (gk - hi.astype(F32)).astype(BF16)
    bcum = _dot(tril_ref[...], hi) + _dot(tril_ref[...], mid)
    for j in range(0, vd, COL_CHUNK):
        o_ref[:, j:j + COL_CHUNK] = _dot_nt(
            xn, w_ref[2 * kd + j:2 * kd + j + COL_CHUNK, :]).astype(BF16)
    for j in range(vd, 2 * vd, COL_CHUNK):
        gt = _dot_nt(xn, w_ref[2 * kd + j:2 * kd + j + COL_CHUNK, :])
        o_ref[:, j:j + COL_CHUNK] = (gt / (1.0 + jnp.exp(-gt))).astype(BF16)
    q = _dot_nt(xn, w_ref[:kd, :])
    o_ref[:, 2 * vd:2 * vd + kd] = ((q * (hk ** -0.5)) * jnp.exp2(bcum)).astype(BF16)
    k = _dot_nt(xn, w_ref[kd:2 * kd, :])
    kt = k * jnp.exp2(-bcum)
    o_ref[:, 2 * vd + kd:2 * vd + 2 * kd] = kt.astype(BF16)
    for i in range(tm // c):
        el = jnp.exp2(bcum[(i + 1) * c - 1:(i + 1) * c, :])
        el_ref[i:i + 1, :] = el
        o_ref[i * c:(i + 1) * c, 2 * vd + 2 * kd:2 * vd + 3 * kd] = (kt[i * c:(i + 1) * c] * el).astype(BF16)


def _gla_inproj(x, layer, mix_layer, g_mix, w_in, w_glr, w_gk2, b_gk, kd, vd, casts):
    n, d = x.shape
    n_out = 2 * vd + 3 * kd
    per_tile = ROW_TILE // GLA_CHUNK
    r = np.arange(ROW_TILE)
    tril = jnp.asarray((r[:, None] >= r[None, :]) & (r[:, None] // GLA_CHUNK == r[None, :] // GLA_CHUNK),
                       dtype=BF16)
    proj, el, *cast_out = pl.pallas_call(
        functools.partial(_gla_inproj_kernel, kd=kd, vd=vd, hk=kd // GLA_HEADS, n_cast=len(casts)),
        grid=(n // ROW_TILE,),
        in_specs=[pl.BlockSpec((ROW_TILE, d), lambda i: (i, 0)),
                  _layer(g_mix, mix_layer), _resident(w_in.shape), _layer(w_glr, layer),
                  _layer(w_gk2, layer), _layer(b_gk, layer), _resident(tril.shape)]
                 + [c.in_spec for c in casts],
        out_specs=[pl.BlockSpec((ROW_TILE, n_out), lambda i: (i, 0)),
                   pl.BlockSpec((per_tile, kd), lambda i: (i, 0))] + [c.out_spec for c in casts],
        out_shape=[jax.ShapeDtypeStruct((n, n_out), BF16),
                   jax.ShapeDtypeStruct((n // GLA_CHUNK, kd), F32)] + [c.out_shape for c in casts],
        compiler_params=_params("parallel"),
        name="gla_inproj",
    )(x, g_mix, w_in, w_glr, w_gk2, b_gk, tril, *[c.stack for c in casts])
    return proj, el, cast_out


def _gla_kernel(v_ref, sg_ref, qt_ref, kt_ref, kd_ref, el_ref, gon_ref, o_ref, st_ref, *, hk, hv):
    @pl.when(pl.program_id(1) == 0)
    def _():
        st_ref[...] = jnp.zeros_like(st_ref)

    c = GLA_CHUNK
    hc = GLA_HEADS * c
    ri = lax.broadcasted_iota(jnp.int32, (hc, hc), 0)
    ci = lax.broadcasted_iota(jnp.int32, (hc, hc), 1)
    causal = (ri >= ci) & (ri // c == ci // c)
    gon = gon_ref[...]
    for i in range(v_ref.shape[0] // c):
        rows = slice(i * c, (i + 1) * c)
        el = el_ref[i:i + 1, :]
        heads_k = [slice(h * hk, (h + 1) * hk) for h in range(GLA_HEADS)]
        heads_v = [slice(h * hv, (h + 1) * hv) for h in range(GLA_HEADS)]
        q_stack = jnp.concatenate([qt_ref[rows, ks] for ks in heads_k], axis=0)
        k_stack = jnp.concatenate([kt_ref[rows, ks] for ks in heads_k], axis=0)
        v_stack = jnp.concatenate([v_ref[rows, vs] for vs in heads_v], axis=0)
        attn = jnp.where(causal, _dot_nt(q_stack, k_stack), 0.0).astype(BF16)
        o_intra = _dot(attn, v_stack)
        for h in range(GLA_HEADS):
            ks, vs = heads_k[h], heads_v[h]
            st = st_ref[h]
            o = o_intra[h * c:(h + 1) * c] + _dot_nt(qt_ref[rows, ks], st.astype(BF16))
            st_ref[h] = st * el[:, ks] + _dot_tn(v_ref[rows, vs], kd_ref[rows, ks])
            o = o * lax.rsqrt(jnp.mean(o * o, axis=-1, keepdims=True) + EPS) * gon
            o_ref[rows, vs] = (o * sg_ref[rows, vs].astype(F32)).astype(BF16)


def _gla_core(proj, el, gon, batch, seq, kd, vd):
    n = proj.shape[0]
    hk = kd // GLA_HEADS
    hv = vd // GLA_HEADS
    proj3 = proj.reshape(batch, seq, proj.shape[1])
    el3 = el.reshape(batch, seq // GLA_CHUNK, kd)
    kblk = 2 * vd // kd
    out = pl.pallas_call(
        functools.partial(_gla_kernel, hk=hk, hv=hv),
        grid=(batch, seq // GLA_ROWS),
        in_specs=[pl.BlockSpec((None, GLA_ROWS, vd), lambda b, t: (b, t, 0)),
                  pl.BlockSpec((None, GLA_ROWS, vd), lambda b, t: (b, t, 1)),
                  pl.BlockSpec((None, GLA_ROWS, kd), lambda b, t: (b, t, kblk)),
                  pl.BlockSpec((None, GLA_ROWS, kd), lambda b, t: (b, t, kblk + 1)),
                  pl.BlockSpec((None, GLA_ROWS, kd), lambda b, t: (b, t, kblk + 2)),
                  pl.BlockSpec((None, GLA_ROWS // GLA_CHUNK, kd), lambda b, t: (b, t, 0)),
                  pl.BlockSpec((1, hv), lambda b, t: (0, 0))],
        out_specs=pl.BlockSpec((None, GLA_ROWS, vd), lambda b, t: (b, t, 0)),
        out_shape=jax.ShapeDtypeStruct((batch, seq, vd), BF16),
        scratch_shapes=[pltpu.VMEM((GLA_HEADS, hv, hk), F32)],
        compiler_params=_params("parallel", "arbitrary"),
        name="gla_core",
    )(proj3, proj3, proj3, proj3, proj3, el3, gon)
    return out.reshape(n, vd)


_POST_TAIL_INPUTS = {"none": 0, "final": 1, "q": 2, "kvq": 5}


def _post_kernel(*refs, tail, n_cast):
    ins, outs = _run_casts(refs, 6 + _POST_TAIL_INPUTS[tail], n_cast)
    h_ref, o_ref, wo_ref, g_ref, wup_ref, wdn_ref = ins[:6]
    rest = ins[6:] + outs
    h1 = h_ref[...] + _dot(o_ref[...], wo_ref[...])
    hn = _rms(h1, g_ref[...]).astype(BF16)
    acc = h1
    d_ff = wup_ref.shape[1]
    for j in range(0, d_ff, COL_CHUNK):
        u = jnp.maximum(_dot(hn, wup_ref[:, j:j + COL_CHUNK]), 0.0)
        acc = acc + _dot((u * u).astype(BF16), wdn_ref[j:j + COL_CHUNK, :])
    if tail == "none":
        (out_ref,) = rest
    elif tail == "final":
        gfin_ref, out_ref = rest
        acc = _rms(acc, gfin_ref[...])
    else:
        if tail == "kvq":
            gq_ref, wq_ref, gkv_ref, wk_ref, wvt_ref, out_ref, q_ref, kp_ref, vt_ref = rest
        else:
            gq_ref, wq_ref, out_ref, q_ref = rest
        xhat = acc * lax.rsqrt(jnp.mean(acc * acc, axis=-1, keepdims=True) + EPS)
        _store_q((xhat * gq_ref[...]).astype(BF16), wq_ref, q_ref)
        if tail == "kvq":
            _store_kv((xhat * gkv_ref[...]).astype(BF16), wk_ref, wvt_ref, kp_ref, vt_ref)
    out_ref[...] = acc


def _post(h, o, params, tail, casts, n_q_cols=0, n_vt_rows=0):
    n, d = h.shape
    assert len(params) == 4 + _POST_TAIL_INPUTS[tail]
    rows = lambda width: pl.BlockSpec((ROW_TILE, width), lambda i: (i, 0))
    out_specs = [rows(d)]
    out_shape = [jax.ShapeDtypeStruct((n, d), F32)]
    if tail in ("q", "kvq"):
        out_specs.append(rows(n_q_cols))
        out_shape.append(jax.ShapeDtypeStruct((n, n_q_cols), BF16))
    if tail == "kvq":
        out_specs += [rows(4 * LANE), pl.BlockSpec((n_vt_rows, ROW_TILE), lambda i: (0, i))]
        out_shape += [jax.ShapeDtypeStruct((n, 4 * LANE), BF16),
                      jax.ShapeDtypeStruct((n_vt_rows, n), BF16)]
    n_main = len(out_specs)
    outs = pl.pallas_call(
        functools.partial(_post_kernel, tail=tail, n_cast=len(casts)),
        grid=(n // ROW_TILE,),
        in_specs=[rows(d), rows(o.shape[1])] + [spec for _, spec in params]
                 + [c.in_spec for c in casts],
        out_specs=out_specs + [c.out_spec for c in casts],
        out_shape=out_shape + [c.out_shape for c in casts],
        compiler_params=_params("parallel"),
        name="post_mlp",
    )(h, o, *[arr for arr, _ in params], *[c.stack for c in casts])
    return outs[:n_main], outs[n_main:]


def _rel_buckets_t():
    j = np.arange(2 * BLOCK)[:, None]
    i = np.arange(BLOCK)[None, :]
    n = np.maximum(i + BLOCK - j, 0)
    max_exact = REL_BUCKETS // 2
    large = max_exact + (np.log(np.maximum(n, 1) / max_exact)
                         / np.log(REL_MAX_DIST / max_exact)
                         * (REL_BUCKETS - max_exact)).astype(np.int32)
    large = np.minimum(large, REL_BUCKETS - 1)
    return np.where(n < max_exact, n, large).astype(np.int32)


def _swa_kernel(rel_ref, sink_ref, bucket_ref, q_ref, kpp_ref, kpc_ref, vtp_ref, vtc_ref, o_ref,
                bias_ref, *, n_kv):
    n_q = bias_ref.shape[0]
    hd = SWA_HEAD_DIM
    pairs_per_kv = SWA_GROUP // 2

    @pl.when((pl.program_id(0) == 0) & (pl.program_id(1) == 0))
    def _():
        bucket = bucket_ref[...]

        def per_head(h, carry):
            def per_bucket(b, acc):
                return jnp.where(bucket == b, rel_ref[b, h] * LOG2E, acc)
            bias_ref[h] = lax.fori_loop(0, REL_BUCKETS, per_bucket,
                                        jnp.zeros(bucket.shape, F32))
            return carry

        lax.fori_loop(0, n_q, per_head, 0)

    kj = lax.broadcasted_iota(jnp.int32, (2 * BLOCK, BLOCK), 0)
    qi = lax.broadcasted_iota(jnp.int32, (2 * BLOCK, BLOCK), 1)
    dist = qi + BLOCK - kj
    in_window = (dist >= 0) & (dist < WINDOW)
    valid_first = in_window & ((pl.program_id(1) > 0) | (kj >= BLOCK))
    zeros = jnp.zeros((hd, 2 * BLOCK), BF16)
    for qb in range(SWA_QBLOCKS):
        rows = slice(qb * BLOCK, (qb + 1) * BLOCK)
        prev_rows = slice((qb - 1) * BLOCK, qb * BLOCK)
        valid = valid_first if qb == 0 else in_window
        for hkv in range(n_kv):
            def k_band(col):
                cols = slice(col * LANE, (col + 1) * LANE)
                prev = kpp_ref[:, cols] if qb == 0 else kpc_ref[prev_rows, cols]
                return [prev, kpc_ref[rows, cols]]
            k_stack = jnp.concatenate(k_band(2 * hkv) + k_band(2 * hkv + 1), axis=0)
            vrow = slice(hkv * hd, (hkv + 1) * hd)
            if qb == 0:
                vt_band = jnp.concatenate([vtp_ref[vrow, :], vtc_ref[vrow, :BLOCK]], axis=1)
            else:
                vt_band = vtc_ref[vrow, (qb - 1) * BLOCK:(qb + 1) * BLOCK]
            v_bd = jnp.concatenate([jnp.concatenate([vt_band, zeros], axis=1),
                                    jnp.concatenate([zeros, vt_band], axis=1)], axis=0)
            for pr in range(pairs_per_kv):
                pair = hkv * pairs_per_kv + pr
                cols = slice(pair * LANE, (pair + 1) * LANE)
                st = _dot_nt(k_stack, q_ref[rows, cols])
                ps, rden = [], []
                for e in range(2):
                    hq = 2 * pair + e
                    s = jnp.where(valid, st[e * 2 * BLOCK:(e + 1) * 2 * BLOCK] + bias_ref[hq], NEG)
                    sink = sink_ref[hq] * LOG2E
                    m = jnp.maximum(jnp.max(s, axis=0, keepdims=True), sink)
                    p = jnp.exp2(s - m)
                    den = jnp.sum(p, axis=0, keepdims=True) + jnp.exp2(sink - m)
                    ps.append(p.astype(BF16))
                    rden.append(jnp.broadcast_to(1.0 / den, (hd, BLOCK)))
                ot = _dot(v_bd, jnp.concatenate(ps, axis=0)) * jnp.concatenate(rden, axis=0)
                o_ref[rows, cols] = ot.T.astype(BF16)


def _swa_core(q, kpad, vt, rel_table, sinks, batch, seq):
    n, dq = q.shape
    n_q = dq // SWA_HEAD_DIM
    n_kv = n_q // SWA_GROUP
    rows = SWA_QBLOCKS * BLOCK
    steps = seq // rows
    q3 = q.reshape(batch, seq, dq)
    kp3 = kpad.reshape(batch, seq, kpad.shape[1])
    bucket = jnp.asarray(_rel_buckets_t())
    smem = pl.BlockSpec(memory_space=pltpu.SMEM)
    prev = lambda t: jnp.maximum(SWA_QBLOCKS * t - 1, 0)
    out = pl.pallas_call(
        functools.partial(_swa_kernel, n_kv=n_kv),
        grid=(batch, steps),
        in_specs=[smem, smem,
                  pl.BlockSpec((2 * BLOCK, BLOCK), lambda b, t: (0, 0)),
                  pl.BlockSpec((None, rows, dq), lambda b, t: (b, t, 0)),
                  pl.BlockSpec((None, BLOCK, kpad.shape[1]), lambda b, t: (b, prev(t), 0)),
                  pl.BlockSpec((None, rows, kpad.shape[1]), lambda b, t: (b, t, 0)),
                  pl.BlockSpec((vt.shape[0], BLOCK),
                               lambda b, t: (0, b * steps * SWA_QBLOCKS + prev(t))),
                  pl.BlockSpec((vt.shape[0], rows), lambda b, t: (0, b * steps + t))],
        out_specs=pl.BlockSpec((None, rows, dq), lambda b, t: (b, t, 0)),
        out_shape=jax.ShapeDtypeStruct((batch, seq, dq), BF16),
        scratch_shapes=[pltpu.VMEM((n_q, 2 * BLOCK, BLOCK), F32)],
        compiler_params=_params("arbitrary", "arbitrary"),
        name="swa_core",
    )(rel_table, sinks, bucket, q3, kp3, kp3, vt, vt)
    return out.reshape(n, dq)


def kernel(x, a_w_in, a_w_gk2, a_b_gk, a_onorm, a_w_out, kv_norm, w_kv, b_w_q, b_sinks, b_w_out,
           rel_table, ln_mix, ln_mlp, w_up, w_down, ln_final):
    batch, seq, d = x.shape
    n_a = a_w_in.shape[0]
    n_b = b_w_q.shape[0]
    kd = a_w_gk2.shape[2]
    vd = (a_w_in.shape[2] - GLA_GATE_RANK - 2 * kd) // 2

    depth = n_a + n_b
    assert n_a >= 1
    half = w_kv.shape[1] // 2
    assert half == LANE and 2 * SWA_HEAD_DIM == LANE

    w_in_t = jnp.swapaxes(a_w_in, 1, 2)
    n_main = 2 * kd + 2 * vd

    w_glr = jnp.pad(w_in_t[:, n_main:, :], ((0, 0), (0, LANE - GLA_GATE_RANK), (0, 0))).astype(BF16)
    w_gk2 = jnp.pad(a_w_gk2, ((0, 0), (0, LANE - GLA_GATE_RANK), (0, 0))).astype(BF16)
    w_vt = w_kv[:, half:].T.astype(BF16)
    g_mix = ln_mix.reshape(depth, 1, d)
    g_mlp = ln_mlp.reshape(depth, 1, d)
    g_kv = kv_norm.reshape(1, d)
    g_final = ln_final.reshape(1, d)
    b_gk = a_b_gk.reshape(n_a, 1, kd)

    calls = []
    for layer in range(depth):
        is_gla = layer < n_a
        j = layer if is_gla else layer - n_a
        if is_gla:
            calls.append(("inproj", layer, {"w_in": (w_in_t, j, n_main, d)}))
        needs = {"w_out": (a_w_out if is_gla else b_w_out, j, d, d),
                 "w_up": (w_up, layer, d, w_up.shape[2]), "w_down": (w_down, layer, w_down.shape[1], d)}
        if n_a <= layer + 1 < depth:
            needs["w_q"] = (b_w_q, layer + 1 - n_a, d, b_w_q.shape[2])
        if layer + 1 == n_a:
            needs["w_k"] = (w_kv[None], 0, d, half)
        calls.append(("post", layer, needs))
    steps = batch * seq // ROW_TILE
    bf = {name: stack[idx, :rows, :cols].astype(BF16)
          for name, (stack, idx, rows, cols) in calls[0][2].items()}

    h = x.reshape(batch * seq, d)
    for c, (kind, layer, _) in enumerate(calls):
        is_gla = layer < n_a
        j = layer if is_gla else layer - n_a
        nxt = calls[c + 1][2] if c + 1 < len(calls) else {}
        casts = [_Cast(stack, idx, rows, cols, steps) for stack, idx, rows, cols in nxt.values()]
        if kind == "inproj":
            proj, el, cast_out = _gla_inproj(h, j, layer, g_mix, bf["w_in"], w_glr, w_gk2, b_gk,
                                             kd, vd, casts)
            o = _gla_core(proj, el, a_onorm[j].reshape(1, -1), batch, seq, kd, vd)
        else:
            if not is_gla:
                o = _swa_core(q, kpad, vt, rel_table, b_sinks[j], batch, seq)
            params = [(bf["w_out"], _resident(bf["w_out"].shape)), (g_mlp, _layer(g_mlp, layer)),
                      (bf["w_up"], _resident(bf["w_up"].shape)),
                      (bf["w_down"], _resident(bf["w_down"].shape))]
            if layer == depth - 1:
                (h,), cast_out = _post(h, o, params + [(g_final, _resident(g_final.shape))],
                                       "final", casts)
            elif layer + 1 < n_a:
                (h,), cast_out = _post(h, o, params, "none", casts)
            else:
                params += [(g_mix, _layer(g_mix, layer + 1)), (bf["w_q"], _resident(bf["w_q"].shape))]
                if layer + 1 == n_a:
                    params += [(g_kv, _resident(g_kv.shape)), (bf["w_k"], _resident(bf["w_k"].shape)),
                               (w_vt, _resident(w_vt.shape))]
                    (h, q, kpad, vt), cast_out = _post(h, o, params, "kvq", casts,
                                                       bf["w_q"].shape[1], w_vt.shape[0])
                else:
                    (h, q), cast_out = _post(h, o, params, "q", casts, bf["w_q"].shape[1])
        bf = dict(zip(nxt.keys(), cast_out))
    return h.reshape(batch, seq, d)
```

```python
import functools
import math

import numpy as np
import jax
import jax.numpy as jnp
from jax import lax
from jax.experimental import pallas as pl
from jax.experimental.pallas import tpu as pltpu

F32 = jnp.float32
BF16 = jnp.bfloat16

EPS = 1e-6
NEG = -1e30
LOG2E = math.log2(math.e)

GLA_HEADS = 4
GLA_GATE_RANK = 16
GLA_GATE_NORM = 16.0
GLA_CHUNK = 64

SWA_HEAD_DIM = 64
SWA_GROUP = 8
WINDOW = 128
BLOCK = 128
REL_BUCKETS = 32
REL_MAX_DIST = 128

LANE = 128
ROW_TILE = 512
COL_CHUNK = 512
GLA_ROWS = 1024
SWA_QBLOCKS = 8
VMEM_LIMIT = 56 * 1024 * 1024

_NT = (((1,), (1,)), ((), ()))
_TN = (((0,), (0,)), ((), ()))


def _rms(x, g):
    ms = jnp.mean(x * x, axis=-1, keepdims=True)
    return x * lax.rsqrt(ms + EPS) * g


def _dot(a, b):
    return jnp.dot(a, b, preferred_element_type=F32)


def _dot_nt(a, b):
    return lax.dot_general(a, b, _NT, preferred_element_type=F32)


def _dot_tn(a, b):
    return lax.dot_general(a, b, _TN, preferred_element_type=F32)


def _params(*sem):
    return pltpu.CompilerParams(dimension_semantics=sem, vmem_limit_bytes=VMEM_LIMIT)


def _resident(shape):
    zeros = (0,) * len(shape)
    return pl.BlockSpec(shape, lambda *_: zeros, pipeline_mode=pl.Buffered(1))


def _layer(stacked, layer):
    idx = (layer,) + (0,) * (stacked.ndim - 1)
    return pl.BlockSpec((None,) + stacked.shape[1:], lambda *_: idx, pipeline_mode=pl.Buffered(1))


class _Cast:
    def __init__(self, stack, layer, rows, cols, steps):
        assert rows % steps == 0 and cols % LANE == 0
        block = (rows // steps, cols)
        self.stack = stack
        self.in_spec = pl.BlockSpec((None,) + block, lambda i: (layer, i, 0))
        self.out_spec = pl.BlockSpec(block, lambda i: (i, 0))
        self.out_shape = jax.ShapeDtypeStruct((rows, cols), BF16)


def _run_casts(refs, n_in, n_cast):
    n_out = len(refs) - n_in - 2 * n_cast
    for src, dst in zip(refs[n_in:n_in + n_cast], refs[n_in + n_cast + n_out:]):
        dst[...] = src[...].astype(BF16)
    return refs[:n_in], refs[n_in + n_cast:n_in + n_cast + n_out]


def _store_q(xn, wq_ref, q_ref):
    scale = SWA_HEAD_DIM ** -0.5 * LOG2E
    for j in range(0, q_ref.shape[-1], COL_CHUNK):
        q_ref[:, j:j + COL_CHUNK] = (_dot(xn, wq_ref[:, j:j + COL_CHUNK]) * scale).astype(BF16)


def _store_kv(xn, wk_ref, wvt_ref, kp_ref, vt_ref):
    k = _dot(xn, wk_ref[...])
    kr = pltpu.roll(k, SWA_HEAD_DIM, axis=1)
    low = lax.broadcasted_iota(jnp.int32, k.shape, 1) < SWA_HEAD_DIM
    pieces = (jnp.where(low, k, 0.0), jnp.where(low, 0.0, kr),
              jnp.where(low, kr, 0.0), jnp.where(low, 0.0, k))
    for i, piece in enumerate(pieces):
        kp_ref[:, i * LANE:(i + 1) * LANE] = piece.astype(BF16)
    vt_ref[...] = _dot_nt(wvt_ref[...], xn).astype(BF16)


def _gla_inproj_kernel(*refs, kd, vd, hk, n_cast):
    (x_ref, g_ref, w_ref, wglr_ref, wgk2_ref, bgk_ref, tril_ref), (o_ref, el_ref) = _run_casts(
        refs, 7, n_cast)
    tm = x_ref.shape[0]
    c = GLA_CHUNK
    xn = _rms(x_ref[...], g_ref[...]).astype(BF16)
    glr = _dot_nt(xn, wglr_ref[...])
    z = _dot(glr.astype(BF16), wgk2_ref[...]) + bgk_ref[...]
    gk = (jnp.minimum(z, 0.0) - jnp.log(1.0 + jnp.exp(-jnp.abs(z)))) * (LOG2E / GLA_GATE_NORM)
    hi = gk.astype(BF16)
    mid = (gk - hi.astype(F32)).astype(BF16)
    bcum = _dot(tril_ref[...], hi) + _dot(tril_ref[...], mid)
    for j in range(0, vd, COL_CHUNK):
        o_ref[:, j:j + COL_CHUNK] = _dot_nt(
            xn, w_ref[2 * kd + j:2 * kd + j + COL_CHUNK, :]).astype(BF16)
    for j in range(vd, 2 * vd, COL_CHUNK):
        gt = _dot_nt(xn, w_ref[2 * kd + j:2 * kd + j + COL_CHUNK, :])
        o_ref[:, j:j + COL_CHUNK] = (gt / (1.0 + jnp.exp(-gt))).astype(BF16)
    q = _dot_nt(xn, w_ref[:kd, :])
    o_ref[:, 2 * vd:2 * vd + kd] = ((q * (hk ** -0.5)) * jnp.exp2(bcum)).astype(BF16)
    k = _dot_nt(xn, w_ref[kd:2 * kd, :])
    kt = k * jnp.exp2(-bcum)
    o_ref[:, 2 * vd + kd:2 * vd + 2 * kd] = kt.astype(BF16)
    for i in range(tm // c):
        el = jnp.exp2(bcum[(i + 1) * c - 1:(i + 1) * c, :])
        el_ref[i:i + 1, :] = el
        o_ref[i * c:(i + 1) * c, 2 * vd + 2 * kd:2 * vd + 3 * kd] = (kt[i * c:(i + 1) * c] * el).astype(BF16)


def _gla_inproj(x, layer, mix_layer, g_mix, w_in, w_glr, w_gk2, b_gk, kd, vd, casts):
    n, d = x.shape
    n_out = 2 * vd + 3 * kd
    per_tile = ROW_TILE // GLA_CHUNK
    r = np.arange(ROW_TILE)
    tril = jnp.asarray((r[:, None] >= r[None, :]) & (r[:, None] // GLA_CHUNK == r[None, :] // GLA_CHUNK),
                       dtype=BF16)
    proj, el, *cast_out = pl.pallas_call(
        functools.partial(_gla_inproj_kernel, kd=kd, vd=vd, hk=kd // GLA_HEADS, n_cast=len(casts)),
        grid=(n // ROW_TILE,),
        in_specs=[pl.BlockSpec((ROW_TILE, d), lambda i: (i, 0)),
                  _layer(g_mix, mix_layer), _resident(w_in.shape), _layer(w_glr, layer),
                  _layer(w_gk2, layer), _layer(b_gk, layer), _resident(tril.shape)]
                 + [c.in_spec for c in casts],
        out_specs=[pl.BlockSpec((ROW_TILE, n_out), lambda i: (i, 0)),
                   pl.BlockSpec((per_tile, kd), lambda i: (i, 0))] + [c.out_spec for c in casts],
        out_shape=[jax.ShapeDtypeStruct((n, n_out), BF16),
                   jax.ShapeDtypeStruct((n // GLA_CHUNK, kd), F32)] + [c.out_shape for c in casts],
        compiler_params=_params("parallel"),
        name="gla_inproj",
    )(x, g_mix, w_in, w_glr, w_gk2, b_gk, tril, *[c.stack for c in casts])
    return proj, el, cast_out


def _gla_kernel(v_ref, sg_ref, qt_ref, kt_ref, kd_ref, el_ref, gon_ref, o_ref, st_ref, *, hk, hv):
    @pl.when(pl.program_id(1) == 0)
    def _():
        st_ref[...] = jnp.zeros_like(st_ref)

    c = GLA_CHUNK
    hc = GLA_HEADS * c
    ri = lax.broadcasted_iota(jnp.int32, (hc, hc), 0)
    ci = lax.broadcasted_iota(jnp.int32, (hc, hc), 1)
    causal = (ri >= ci) & (ri // c == ci // c)
    gon = gon_ref[...]
    for i in range(v_ref.shape[0] // c):
        rows = slice(i * c, (i + 1) * c)
        el = el_ref[i:i + 1, :]
        heads_k = [slice(h * hk, (h + 1) * hk) for h in range(GLA_HEADS)]
        heads_v = [slice(h * hv, (h + 1) * hv) for h in range(GLA_HEADS)]
        q_stack = jnp.concatenate([qt_ref[rows, ks] for ks in heads_k], axis=0)
        k_stack = jnp.concatenate([kt_ref[rows, ks] for ks in heads_k], axis=0)
        v_stack = jnp.concatenate([v_ref[rows, vs] for vs in heads_v], axis=0)
        attn = jnp.where(causal, _dot_nt(q_stack, k_stack), 0.0).astype(BF16)
        o_intra = _dot(attn, v_stack)
        for h in range(GLA_HEADS):
            ks, vs = heads_k[h], heads_v[h]
            st = st_ref[h]
            o = o_intra[h * c:(h + 1) * c] + _dot_nt(qt_ref[rows, ks], st.astype(BF16))
            st_ref[h] = st * el[:, ks] + _dot_tn(v_ref[rows, vs], kd_ref[rows, ks])
            o = o * lax.rsqrt(jnp.mean(o * o, axis=-1, keepdims=True) + EPS) * gon
            o_ref[rows, vs] = (o * sg_ref[rows, vs].astype(F32)).astype(BF16)


def _gla_core(proj, el, gon, batch, seq, kd, vd):
    n = proj.shape[0]
    hk = kd // GLA_HEADS
    hv = vd // GLA_HEADS
    proj3 = proj.reshape(batch, seq, proj.shape[1])
    el3 = el.reshape(batch, seq // GLA_CHUNK, kd)
    kblk = 2 * vd // kd
    out = pl.pallas_call(
        functools.partial(_gla_kernel, hk=hk, hv=hv),
        grid=(batch, seq // GLA_ROWS),
        in_specs=[pl.BlockSpec((None, GLA_ROWS, vd), lambda b, t: (b, t, 0)),
                  pl.BlockSpec((None, GLA_ROWS, vd), lambda b, t: (b, t, 1)),
                  pl.BlockSpec((None, GLA_ROWS, kd), lambda b, t: (b, t, kblk)),
                  pl.BlockSpec((None, GLA_ROWS, kd), lambda b, t: (b, t, kblk + 1)),
                  pl.BlockSpec((None, GLA_ROWS, kd), lambda b, t: (b, t, kblk + 2)),
                  pl.BlockSpec((None, GLA_ROWS // GLA_CHUNK, kd), lambda b, t: (b, t, 0)),
                  pl.BlockSpec((1, hv), lambda b, t: (0, 0))],
        out_specs=pl.BlockSpec((None, GLA_ROWS, vd), lambda b, t: (b, t, 0)),
        out_shape=jax.ShapeDtypeStruct((batch, seq, vd), BF16),
        scratch_shapes=[pltpu.VMEM((GLA_HEADS, hv, hk), F32)],
        compiler_params=_params("parallel", "arbitrary"),
        name="gla_core",
    )(proj3, proj3, proj3, proj3, proj3, el3, gon)
    return out.reshape(n, vd)


_POST_TAIL_INPUTS = {"none": 0, "final": 1, "q": 2, "kvq": 5}


def _post_kernel(*refs, tail, n_cast):
    ins, outs = _run_casts(refs, 6 + _POST_TAIL_INPUTS[tail], n_cast)
    h_ref, o_ref, wo_ref, g_ref, wup_ref, wdn_ref = ins[:6]
    rest = ins[6:] + outs
    h1 = h_ref[...] + _dot(o_ref[...], wo_ref[...])
    hn = _rms(h1, g_ref[...]).astype(BF16)
    acc = h1
    d_ff = wup_ref.shape[1]
    for j in range(0, d_ff, COL_CHUNK):
        u = jnp.maximum(_dot(hn, wup_ref[:, j:j + COL_CHUNK]), 0.0)
        acc = acc + _dot((u * u).astype(BF16), wdn_ref[j:j + COL_CHUNK, :])
    if tail == "none":
        (out_ref,) = rest
    elif tail == "final":
        gfin_ref, out_ref = rest
        acc = _rms(acc, gfin_ref[...])
    else:
        if tail == "kvq":
            gq_ref, wq_ref, gkv_ref, wk_ref, wvt_ref, out_ref, q_ref, kp_ref, vt_ref = rest
        else:
            gq_ref, wq_ref, out_ref, q_ref = rest
        xhat = acc * lax.rsqrt(jnp.mean(acc * acc, axis=-1, keepdims=True) + EPS)
        _store_q((xhat * gq_ref[...]).astype(BF16), wq_ref, q_ref)
        if tail == "kvq":
            _store_kv((xhat * gkv_ref[...]).astype(BF16), wk_ref, wvt_ref, kp_ref, vt_ref)
    out_ref[...] = acc


def _post(h, o, params, tail, casts, n_q_cols=0, n_vt_rows=0):
    n, d = h.shape
    assert len(params) == 4 + _POST_TAIL_INPUTS[tail]
    rows = lambda width: pl.BlockSpec((ROW_TILE, width), lambda i: (i, 0))
    out_specs = [rows(d)]
    out_shape = [jax.ShapeDtypeStruct((n, d), F32)]
    if tail in ("q", "kvq"):
        out_specs.append(rows(n_q_cols))
        out_shape.append(jax.ShapeDtypeStruct((n, n_q_cols), BF16))
    if tail == "kvq":
        out_specs += [rows(4 * LANE), pl.BlockSpec((n_vt_rows, ROW_TILE), lambda i: (0, i))]
        out_shape += [jax.ShapeDtypeStruct((n, 4 * LANE), BF16),
                      jax.ShapeDtypeStruct((n_vt_rows, n), BF16)]
    n_main = len(out_specs)
    outs = pl.pallas_call(
        functools.partial(_post_kernel, tail=tail, n_cast=len(casts)),
        grid=(n // ROW_TILE,),
        in_specs=[rows(d), rows(o.shape[1])] + [spec for _, spec in params]
                 + [c.in_spec for c in casts],
        out_specs=out_specs + [c.out_spec for c in casts],
        out_shape=out_shape + [c.out_shape for c in casts],
        compiler_params=_params("parallel"),
        name="post_mlp",
    )(h, o, *[arr for arr, _ in params], *[c.stack for c in casts])
    return outs[:n_main], outs[n_main:]


def _rel_buckets_t():
    j = np.arange(2 * BLOCK)[:, None]
    i = np.arange(BLOCK)[None, :]
    n = np.maximum(i + BLOCK - j, 0)
    max_exact = REL_BUCKETS // 2
    large = max_exact + (np.log(np.maximum(n, 1) / max_exact)
                         / np.log(REL_MAX_DIST / max_exact)
                         * (REL_BUCKETS - max_exact)).astype(np.int32)
    large = np.minimum(large, REL_BUCKETS - 1)
    return np.where(n < max_exact, n, large).astype(np.int32)


def _swa_kernel(rel_ref, sink_ref, bucket_ref, q_ref, kpp_ref, kpc_ref, vtp_ref, vtc_ref, o_ref,
                bias_ref, *, n_kv):
    n_q = bias_ref.shape[0]
    hd = SWA_HEAD_DIM
    pairs_per_kv = SWA_GROUP // 2

    @pl.when((pl.program_id(0) == 0) & (pl.program_id(1) == 0))
    def _():
        bucket = bucket_ref[...]

        def per_head(h, carry):
            def per_bucket(b, acc):
                return jnp.where(bucket == b, rel_ref[b, h] * LOG2E, acc)
            bias_ref[h] = lax.fori_loop(0, REL_BUCKETS, per_bucket,
                                        jnp.zeros(bucket.shape, F32))
            return carry

        lax.fori_loop(0, n_q, per_head, 0)

    kj = lax.broadcasted_iota(jnp.int32, (2 * BLOCK, BLOCK), 0)
    qi = lax.broadcasted_iota(jnp.int32, (2 * BLOCK, BLOCK), 1)
    dist = qi + BLOCK - kj
    in_window = (dist >= 0) & (dist < WINDOW)
    valid_first = in_window & ((pl.program_id(1) > 0) | (kj >= BLOCK))
    zeros = jnp.zeros((hd, 2 * BLOCK), BF16)
    for qb in range(SWA_QBLOCKS):
        rows = slice(qb * BLOCK, (qb + 1) * BLOCK)
        prev_rows = slice((qb - 1) * BLOCK, qb * BLOCK)
        valid = valid_first if qb == 0 else in_window
        for hkv in range(n_kv):
            def k_band(col):
                cols = slice(col * LANE, (col + 1) * LANE)
                prev = kpp_ref[:, cols] if qb == 0 else kpc_ref[prev_rows, cols]
                return [prev, kpc_ref[rows, cols]]
            k_stack = jnp.concatenate(k_band(2 * hkv) + k_band(2 * hkv + 1), axis=0)
            vrow = slice(hkv * hd, (hkv + 1) * hd)
            if qb == 0:
                vt_band = jnp.concatenate([vtp_ref[vrow, :], vtc_ref[vrow, :BLOCK]], axis=1)
            else:
                vt_band = vtc_ref[vrow, (qb - 1) * BLOCK:(qb + 1) * BLOCK]
            v_bd = jnp.concatenate([jnp.concatenate([vt_band, zeros], axis=1),
                                    jnp.concatenate([zeros, vt_band], axis=1)], axis=0)
            for pr in range(pairs_per_kv):
                pair = hkv * pairs_per_kv + pr
                cols = slice(pair * LANE, (pair + 1) * LANE)
                st = _dot_nt(k_stack, q_ref[rows, cols])
                ps, rden = [], []
                for e in range(2):
                    hq = 2 * pair + e
                    s = jnp.where(valid, st[e * 2 * BLOCK:(e + 1) * 2 * BLOCK] + bias_ref[hq], NEG)
                    sink = sink_ref[hq] * LOG2E
                    m = jnp.maximum(jnp.max(s, axis=0, keepdims=True), sink)
                    p = jnp.exp2(s - m)
                    den = jnp.sum(p, axis=0, keepdims=True) + jnp.exp2(sink - m)
                    ps.append(p.astype(BF16))
                    rden.append(jnp.broadcast_to(1.0 / den, (hd, BLOCK)))
                ot = _dot(v_bd, jnp.concatenate(ps, axis=0)) * jnp.concatenate(rden, axis=0)
                o_ref[rows, cols] = ot.T.astype(BF16)


def _swa_core(q, kpad, vt, rel_table, sinks, batch, seq):
    n, dq = q.shape
    n_q = dq // SWA_HEAD_DIM
    n_kv = n_q // SWA_GROUP
    rows = SWA_QBLOCKS * BLOCK
    steps = seq // rows
    q3 = q.reshape(batch, seq, dq)
    kp3 = kpad.reshape(batch, seq, kpad.shape[1])
    bucket = jnp.asarray(_rel_buckets_t())
    smem = pl.BlockSpec(memory_space=pltpu.SMEM)
    prev = lambda t: jnp.maximum(SWA_QBLOCKS * t - 1, 0)
    out = pl.pallas_call(
        functools.partial(_swa_kernel, n_kv=n_kv),
        grid=(batch, steps),
        in_specs=[smem, smem,
                  pl.BlockSpec((2 * BLOCK, BLOCK), lambda b, t: (0, 0)),
                  pl.BlockSpec((None, rows, dq), lambda b, t: (b, t, 0)),
                  pl.BlockSpec((None, BLOCK, kpad.shape[1]), lambda b, t: (b, prev(t), 0)),
                  pl.BlockSpec((None, rows, kpad.shape[1]), lambda b, t: (b, t, 0)),
                  pl.BlockSpec((vt.shape[0], BLOCK),
                               lambda b, t: (0, b * steps * SWA_QBLOCKS + prev(t))),
                  pl.BlockSpec((vt.shape[0], rows), lambda b, t: (0, b * steps + t))],
        out_specs=pl.BlockSpec((None, rows, dq), lambda b, t: (b, t, 0)),
        out_shape=jax.ShapeDtypeStruct((batch, seq, dq), BF16),
        scratch_shapes=[pltpu.VMEM((n_q, 2 * BLOCK, BLOCK), F32)],
        compiler_params=_params("arbitrary", "arbitrary"),
        name="swa_core",
    )(rel_table, sinks, bucket, q3, kp3, kp3, vt, vt)
    return out.reshape(n, dq)


def kernel(x, a_w_in, a_w_gk2, a_b_gk, a_onorm, a_w_out, kv_norm, w_kv, b_w_q, b_sinks, b_w_out,
           rel_table, ln_mix, ln_mlp, w_up, w_down, ln_final):
    batch, seq, d = x.shape
    n_a = a_w_in.shape[0]
    n_b = b_w_q.shape[0]
    kd = a_w_gk2.shape[2]
    vd = (a_w_in.shape[2] - GLA_GATE_RANK - 2 * kd) // 2

    depth = n_a + n_b
    assert n_a >= 1
    half = w_kv.shape[1] // 2
    assert half == LANE and 2 * SWA_HEAD_DIM == LANE

    w_in_t = jnp.swapaxes(a_w_in, 1, 2)
    n_main = 2 * kd + 2 * vd

    w_glr = jnp.pad(w_in_t[:, n_main:, :], ((0, 0), (0, LANE - GLA_GATE_RANK), (0, 0))).astype(BF16)
    w_gk2 = jnp.pad(a_w_gk2, ((0, 0), (0, LANE - GLA_GATE_RANK), (0, 0))).astype(BF16)
    w_vt = w_kv[:, half:].T.astype(BF16)
    g_mix = ln_mix.reshape(depth, 1, d)
    g_mlp = ln_mlp.reshape(depth, 1, d)
    g_kv = kv_norm.reshape(1, d)
    g_final = ln_final.reshape(1, d)
    b_gk = a_b_gk.reshape(n_a, 1, kd)

    calls = []
    for layer in range(depth):
        is_gla = layer < n_a
        j = layer if is_gla else layer - n_a
        if is_gla:
            calls.append(("inproj", layer, {"w_in": (w_in_t, j, n_main, d)}))
        needs = {"w_out": (a_w_out if is_gla else b_w_out, j, d, d),
                 "w_up": (w_up, layer, d, w_up.shape[2]), "w_down": (w_down, layer, w_down.shape[1], d)}
        if n_a <= layer + 1 < depth:
            needs["w_q"] = (b_w_q, layer + 1 - n_a, d, b_w_q.shape[2])
        if layer + 1 == n_a:
            needs["w_k"] = (w_kv[None], 0, d, half)
        calls.append(("post", layer, needs))
    steps = batch * seq // ROW_TILE
    bf = {name: stack[idx, :rows, :cols].astype(BF16)
          for name, (stack, idx, rows, cols) in calls[0][2].items()}

    h = x.reshape(batch * seq, d)
    for c, (kind, layer, _) in enumerate(calls):
        is_gla = layer < n_a
        j = layer if is_gla else layer - n_a
        nxt = calls[c + 1][2] if c + 1 < len(calls) else {}
        casts = [_Cast(stack, idx, rows, cols, steps) for stack, idx, rows, cols in nxt.values()]
        if kind == "inproj":
            proj, el, cast_out = _gla_inproj(h, j, layer, g_mix, bf["w_in"], w_glr, w_gk2, b_gk,
                                             kd, vd, casts)
            o = _gla_core(proj, el, a_onorm[j].reshape(1, -1), batch, seq, kd, vd)
        else:
            if not is_gla:
                o = _swa_core(q, kpad, vt, rel_table, b_sinks[j], batch, seq)
            params = [(bf["w_out"], _resident(bf["w_out"].shape)), (g_mlp, _layer(g_mlp, layer)),
                      (bf["w_up"], _resident(bf["w_up"].shape)),
                      (bf["w_down"], _resident(bf["w_down"].shape))]
            if layer == depth - 1:
                (h,), cast_out = _post(h, o, params + [(g_final, _resident(g_final.shape))],
                                       "final", casts)
            elif layer + 1 < n_a:
                (h,), cast_out = _post(h, o, params, "none", casts)
            else:
                params += [(g_mix, _layer(g_mix, layer + 1)), (bf["w_q"], _resident(bf["w_q"].shape))]
                if layer + 1 == n_a:
                    params += [(g_kv, _resident(g_kv.shape)), (bf["w_k"], _resident(bf["w_k"].shape)),
                               (w_vt, _resident(w_vt.shape))]
                    (h, q, kpad, vt), cast_out = _post(h, o, params, "kvq", casts,
                                                       bf["w_q"].shape[1], w_vt.shape[0])
                else:
                    (h, q), cast_out = _post(h, o, params, "q", casts, bf["w_q"].shape[1])
        bf = dict(zip(nxt.keys(), cast_out))
    return h.reshape(batch, seq, d)
```

```python
import functools
import math

import numpy as np
import jax
import jax.numpy as jnp
from jax import lax
from jax.experimental import pallas as pl
from jax.experimental.pallas import tpu as pltpu

F32 = jnp.float32
BF16 = jnp.bfloat16

EPS = 1e-6
NEG = -1e30
LOG2E = math.log2(math.e)

GLA_HEADS = 4
GLA_GATE_RANK = 16
GLA_GATE_NORM = 16.0
GLA_CHUNK = 64

SWA_HEAD_DIM = 64
SWA_GROUP = 8
WINDOW = 128
BLOCK = 128
REL_BUCKETS = 32
REL_MAX_DIST = 128

LANE = 128
ROW_TILE = 512
INPROJ_ROWS = 2 * ROW_TILE
COL_CHUNK = 512
GLA_ROWS = 1024
SWA_QBLOCKS = 8
VMEM_LIMIT = 56 * 1024 * 1024

_NT = (((1,), (1,)), ((), ()))
_TN = (((0,), (0,)), ((), ()))


def _rms(x, g):
    ms = jnp.mean(x * x, axis=-1, keepdims=True)
    return x * lax.rsqrt(ms + EPS) * g


def _dot(a, b):
    return jnp.dot(a, b, preferred_element_type=F32)


def _dot_nt(a, b):
    return lax.dot_general(a, b, _NT, preferred_element_type=F32)


def _dot_tn(a, b):
    return lax.dot_general(a, b, _TN, preferred_element_type=F32)


def _params(*sem):
    return pltpu.CompilerParams(dimension_semantics=sem, vmem_limit_bytes=VMEM_LIMIT)


def _resident(shape):
    zeros = (0,) * len(shape)
    return pl.BlockSpec(shape, lambda *_: zeros, pipeline_mode=pl.Buffered(1))


def _layer(stacked, layer):
    idx = (layer,) + (0,) * (stacked.ndim - 1)
    return pl.BlockSpec((None,) + stacked.shape[1:], lambda *_: idx, pipeline_mode=pl.Buffered(1))


class _Cast:
    def __init__(self, stack, layer, rows, cols, steps):
        assert rows % steps == 0 and cols % LANE == 0
        block = (rows // steps, cols)
        self.stack = stack
        self.in_spec = pl.BlockSpec((None,) + block, lambda i: (layer, i, 0))
        self.out_spec = pl.BlockSpec(block, lambda i: (i, 0))
        self.out_shape = jax.ShapeDtypeStruct((rows, cols), BF16)


def _run_casts(refs, n_in, n_cast):
    n_out = len(refs) - n_in - 2 * n_cast
    for src, dst in zip(refs[n_in:n_in + n_cast], refs[n_in + n_cast + n_out:]):
        dst[...] = src[...].astype(BF16)
    return refs[:n_in], refs[n_in + n_cast:n_in + n_cast + n_out]


def _store_q(xn, wq_ref, q_ref, rows):
    scale = SWA_HEAD_DIM ** -0.5 * LOG2E
    for j in range(0, q_ref.shape[-1], COL_CHUNK):
        q_ref[rows, j:j + COL_CHUNK] = (_dot(xn, wq_ref[:, j:j + COL_CHUNK]) * scale).astype(BF16)


def _store_kv(xn, wk_ref, wvt_ref, kp_ref, vt_ref, rows):
    k = _dot(xn, wk_ref[...])
    kr = pltpu.roll(k, SWA_HEAD_DIM, axis=1)
    low = lax.broadcasted_iota(jnp.int32, k.shape, 1) < SWA_HEAD_DIM
    pieces = (jnp.where(low, k, 0.0), jnp.where(low, 0.0, kr),
              jnp.where(low, kr, 0.0), jnp.where(low, 0.0, k))
    for i, piece in enumerate(pieces):
        kp_ref[rows, i * LANE:(i + 1) * LANE] = piece.astype(BF16)
    vt_ref[:, rows] = _dot_nt(wvt_ref[...], xn).astype(BF16)


def _gla_inproj_kernel(*refs, kd, vd, hk, n_cast):
    (x_ref, g_ref, w_ref, wglr_ref, wgk2_ref, bgk_ref, tril_ref), (o_ref, el_ref) = _run_casts(
        refs, 7, n_cast)
    tm = ROW_TILE
    c = GLA_CHUNK
    for r0 in range(0, x_ref.shape[0], tm):
        rows = slice(r0, r0 + tm)
        xn = _rms(x_ref[rows, :], g_ref[...]).astype(BF16)
        glr = _dot_nt(xn, wglr_ref[...])
        z = _dot(glr.astype(BF16), wgk2_ref[...]) + bgk_ref[...]
        gk = (jnp.minimum(z, 0.0) - jnp.log(1.0 + jnp.exp(-jnp.abs(z)))) * (LOG2E / GLA_GATE_NORM)
        hi = gk.astype(BF16)
        mid = (gk - hi.astype(F32)).astype(BF16)
        bcum = _dot(tril_ref[...], hi) + _dot(tril_ref[...], mid)
        for j in range(0, vd, COL_CHUNK):
            o_ref[rows, j:j + COL_CHUNK] = _dot_nt(
                xn, w_ref[2 * kd + j:2 * kd + j + COL_CHUNK, :]).astype(BF16)
        for j in range(vd, 2 * vd, COL_CHUNK):
            gt = _dot_nt(xn, w_ref[2 * kd + j:2 * kd + j + COL_CHUNK, :])
            o_ref[rows, j:j + COL_CHUNK] = (gt / (1.0 + jnp.exp(-gt))).astype(BF16)
        q = _dot_nt(xn, w_ref[:kd, :])
        o_ref[rows, 2 * vd:2 * vd + kd] = ((q * (hk ** -0.5)) * jnp.exp2(bcum)).astype(BF16)
        k = _dot_nt(xn, w_ref[kd:2 * kd, :])
        kt = k * jnp.exp2(-bcum)
        o_ref[rows, 2 * vd + kd:2 * vd + 2 * kd] = kt.astype(BF16)
        for i in range(tm // c):
            el = jnp.exp2(bcum[(i + 1) * c - 1:(i + 1) * c, :])
            el_ref[r0 // c + i:r0 // c + i + 1, :] = el
            o_ref[r0 + i * c:r0 + (i + 1) * c, 2 * vd + 2 * kd:2 * vd + 3 * kd] = (
                kt[i * c:(i + 1) * c] * el).astype(BF16)


def _gla_inproj(x, layer, mix_layer, g_mix, w_in, w_glr, w_gk2, b_gk, kd, vd, casts):
    n, d = x.shape
    n_out = 2 * vd + 3 * kd
    per_block = INPROJ_ROWS // GLA_CHUNK
    r = np.arange(ROW_TILE)
    tril = jnp.asarray((r[:, None] >= r[None, :]) & (r[:, None] // GLA_CHUNK == r[None, :] // GLA_CHUNK),
                       dtype=BF16)
    proj, el, *cast_out = pl.pallas_call(
        functools.partial(_gla_inproj_kernel, kd=kd, vd=vd, hk=kd // GLA_HEADS, n_cast=len(casts)),
        grid=(n // INPROJ_ROWS,),
        in_specs=[pl.BlockSpec((INPROJ_ROWS, d), lambda i: (i, 0)),
                  _layer(g_mix, mix_layer), _resident(w_in.shape), _layer(w_glr, layer),
                  _layer(w_gk2, layer), _layer(b_gk, layer), _resident(tril.shape)]
                 + [c.in_spec for c in casts],
        out_specs=[pl.BlockSpec((INPROJ_ROWS, n_out), lambda i: (i, 0)),
                   pl.BlockSpec((per_block, kd), lambda i: (i, 0))] + [c.out_spec for c in casts],
        out_shape=[jax.ShapeDtypeStruct((n, n_out), BF16),
                   jax.ShapeDtypeStruct((n // GLA_CHUNK, kd), F32)] + [c.out_shape for c in casts],
        compiler_params=_params("parallel"),
        name="gla_inproj",
    )(x, g_mix, w_in, w_glr, w_gk2, b_gk, tril, *[c.stack for c in casts])
    return proj, el, cast_out


def _gla_kernel(v_ref, sg_ref, qt_ref, kt_ref, kd_ref, el_ref, gon_ref, o_ref, st_ref, *, hk, hv):
    @pl.when(pl.program_id(1) == 0)
    def _():
        st_ref[...] = jnp.zeros_like(st_ref)

    c = GLA_CHUNK
    hc = GLA_HEADS * c
    ri = lax.broadcasted_iota(jnp.int32, (hc, hc), 0)
    ci = lax.broadcasted_iota(jnp.int32, (hc, hc), 1)
    causal = (ri >= ci) & (ri // c == ci // c)
    gon = gon_ref[...]
    for i in range(v_ref.shape[0] // c):
        rows = slice(i * c, (i + 1) * c)
        el = el_ref[i:i + 1, :]
        heads_k = [slice(h * hk, (h + 1) * hk) for h in range(GLA_HEADS)]
        heads_v = [slice(h * hv, (h + 1) * hv) for h in range(GLA_HEADS)]
        q_stack = jnp.concatenate([qt_ref[rows, ks] for ks in heads_k], axis=0)
        k_stack = jnp.concatenate([kt_ref[rows, ks] for ks in heads_k], axis=0)
        v_stack = jnp.concatenate([v_ref[rows, vs] for vs in heads_v], axis=0)
        attn = jnp.where(causal, _dot_nt(q_stack, k_stack), 0.0).astype(BF16)
        o_intra = _dot(attn, v_stack)
        for h in range(GLA_HEADS):
            ks, vs = heads_k[h], heads_v[h]
            st = st_ref[h]
            o = o_intra[h * c:(h + 1) * c] + _dot_nt(qt_ref[rows, ks], st.astype(BF16))
            st_ref[h] = st * el[:, ks] + _dot_tn(v_ref[rows, vs], kd_ref[rows, ks])
            o = o * lax.rsqrt(jnp.mean(o * o, axis=-1, keepdims=True) + EPS) * gon
            o_ref[rows, vs] = (o * sg_ref[rows, vs].astype(F32)).astype(BF16)


def _gla_core(proj, el, gon, batch, seq, kd, vd):
    n = proj.shape[0]
    hk = kd // GLA_HEADS
    hv = vd // GLA_HEADS
    proj3 = proj.reshape(batch, seq, proj.shape[1])
    el3 = el.reshape(batch, seq // GLA_CHUNK, kd)
    kblk = 2 * vd // kd
    out = pl.pallas_call(
        functools.partial(_gla_kernel, hk=hk, hv=hv),
        grid=(batch, seq // GLA_ROWS),
        in_specs=[pl.BlockSpec((None, GLA_ROWS, vd), lambda b, t: (b, t, 0)),
                  pl.BlockSpec((None, GLA_ROWS, vd), lambda b, t: (b, t, 1)),
                  pl.BlockSpec((None, GLA_ROWS, kd), lambda b, t: (b, t, kblk)),
                  pl.BlockSpec((None, GLA_ROWS, kd), lambda b, t: (b, t, kblk + 1)),
                  pl.BlockSpec((None, GLA_ROWS, kd), lambda b, t: (b, t, kblk + 2)),
                  pl.BlockSpec((None, GLA_ROWS // GLA_CHUNK, kd), lambda b, t: (b, t, 0)),
                  pl.BlockSpec((1, hv), lambda b, t: (0, 0))],
        out_specs=pl.BlockSpec((None, GLA_ROWS, vd), lambda b, t: (b, t, 0)),
        out_shape=jax.ShapeDtypeStruct((batch, seq, vd), BF16),
        scratch_shapes=[pltpu.VMEM((GLA_HEADS, hv, hk), F32)],
        compiler_params=_params("parallel", "arbitrary"),
        name="gla_core",
    )(proj3, proj3, proj3, proj3, proj3, el3, gon)
    return out.reshape(n, vd)


_POST_TAIL_INPUTS = {"none": 0, "final": 1, "q": 2, "kvq": 5}
_POST_ROWS = {"none": 2 * ROW_TILE, "final": 2 * ROW_TILE, "q": 2 * ROW_TILE, "kvq": ROW_TILE}


def _post_kernel(*refs, tail, n_cast):
    ins, outs = _run_casts(refs, 6 + _POST_TAIL_INPUTS[tail], n_cast)
    h_ref, o_ref, wo_ref, g_ref, wup_ref, wdn_ref = ins[:6]
    rest = ins[6:] + outs
    d_ff = wup_ref.shape[1]
    for r0 in range(0, h_ref.shape[0], ROW_TILE):
        rows = slice(r0, r0 + ROW_TILE)
        h1 = h_ref[rows, :] + _dot(o_ref[rows, :], wo_ref[...])
        hn = _rms(h1, g_ref[...]).astype(BF16)
        acc = h1
        for j in range(0, d_ff, COL_CHUNK):
            u = jnp.maximum(_dot(hn, wup_ref[:, j:j + COL_CHUNK]), 0.0)
            acc = acc + _dot((u * u).astype(BF16), wdn_ref[j:j + COL_CHUNK, :])
        if tail == "none":
            (out_ref,) = rest
        elif tail == "final":
            gfin_ref, out_ref = rest
            acc = _rms(acc, gfin_ref[...])
        else:
            if tail == "kvq":
                gq_ref, wq_ref, gkv_ref, wk_ref, wvt_ref, out_ref, q_ref, kp_ref, vt_ref = rest
            else:
                gq_ref, wq_ref, out_ref, q_ref = rest
            xhat = acc * lax.rsqrt(jnp.mean(acc * acc, axis=-1, keepdims=True) + EPS)
            _store_q((xhat * gq_ref[...]).astype(BF16), wq_ref, q_ref, rows)
            if tail == "kvq":
                _store_kv((xhat * gkv_ref[...]).astype(BF16), wk_ref, wvt_ref, kp_ref, vt_ref, rows)
        out_ref[rows, :] = acc


def _post(h, o, params, tail, casts, n_q_cols=0, n_vt_rows=0):
    n, d = h.shape
    assert len(params) == 4 + _POST_TAIL_INPUTS[tail]
    block_rows = _POST_ROWS[tail]
    rows = lambda width: pl.BlockSpec((block_rows, width), lambda i: (i, 0))
    out_specs = [rows(d)]
    out_shape = [jax.ShapeDtypeStruct((n, d), F32)]
    if tail in ("q", "kvq"):
        out_specs.append(rows(n_q_cols))
        out_shape.append(jax.ShapeDtypeStruct((n, n_q_cols), BF16))
    if tail == "kvq":
        out_specs += [rows(4 * LANE), pl.BlockSpec((n_vt_rows, block_rows), lambda i: (0, i))]
        out_shape += [jax.ShapeDtypeStruct((n, 4 * LANE), BF16),
                      jax.ShapeDtypeStruct((n_vt_rows, n), BF16)]
    n_main = len(out_specs)
    outs = pl.pallas_call(
        functools.partial(_post_kernel, tail=tail, n_cast=len(casts)),
        grid=(n // block_rows,),
        in_specs=[rows(d), rows(o.shape[1])] + [spec for _, spec in params]
                 + [c.in_spec for c in casts],
        out_specs=out_specs + [c.out_spec for c in casts],
        out_shape=out_shape + [c.out_shape for c in casts],
        compiler_params=_params("parallel"),
        name="post_mlp",
    )(h, o, *[arr for arr, _ in params], *[c.stack for c in casts])
    return outs[:n_main], outs[n_main:]


def _rel_buckets_t():
    j = np.arange(2 * BLOCK)[:, None]
    i = np.arange(BLOCK)[None, :]
    n = np.maximum(i + BLOCK - j, 0)
    max_exact = REL_BUCKETS // 2
    large = max_exact + (np.log(np.maximum(n, 1) / max_exact)
                         / np.log(REL_MAX_DIST / max_exact)
                         * (REL_BUCKETS - max_exact)).astype(np.int32)
    large = np.minimum(large, REL_BUCKETS - 1)
    return np.where(n < max_exact, n, large).astype(np.int32)


def _swa_kernel(rel_ref, sink_ref, bucket_ref, q_ref, kpp_ref, kpc_ref, vtp_ref, vtc_ref, o_ref,
                bias_ref, *, n_kv):
    n_q = bias_ref.shape[0]
    hd = SWA_HEAD_DIM
    pairs_per_kv = SWA_GROUP // 2

    @pl.when((pl.program_id(0) == 0) & (pl.program_id(1) == 0))
    def _():
        bucket = bucket_ref[...]

        def per_head(h, carry):
            def per_bucket(b, acc):
                return jnp.where(bucket == b, rel_ref[b, h] * LOG2E, acc)
            bias_ref[h] = lax.fori_loop(0, REL_BUCKETS, per_bucket,
                                        jnp.zeros(bucket.shape, F32))
            return carry

        lax.fori_loop(0, n_q, per_head, 0)

    kj = lax.broadcasted_iota(jnp.int32, (2 * BLOCK, BLOCK), 0)
    qi = lax.broadcasted_iota(jnp.int32, (2 * BLOCK, BLOCK), 1)
    dist = qi + BLOCK - kj
    in_window = (dist >= 0) & (dist < WINDOW)
    valid_first = in_window & ((pl.program_id(1) > 0) | (kj >= BLOCK))
    zeros = jnp.zeros((hd, 2 * BLOCK), BF16)
    for qb in range(SWA_QBLOCKS):
        rows = slice(qb * BLOCK, (qb + 1) * BLOCK)
        prev_rows = slice((qb - 1) * BLOCK, qb * BLOCK)
        valid = valid_first if qb == 0 else in_window
        for hkv in range(n_kv):
            def k_band(col):
                cols = slice(col * LANE, (col + 1) * LANE)
                prev = kpp_ref[:, cols] if qb == 0 else kpc_ref[prev_rows, cols]
                return [prev, kpc_ref[rows, cols]]
            k_stack = jnp.concatenate(k_band(2 * hkv) + k_band(2 * hkv + 1), axis=0)
            vrow = slice(hkv * hd, (hkv + 1) * hd)
            if qb == 0:
                vt_band = jnp.concatenate([vtp_ref[vrow, :], vtc_ref[vrow, :BLOCK]], axis=1)
            else:
                vt_band = vtc_ref[vrow, (qb - 1) * BLOCK:(qb + 1) * BLOCK]
            v_bd = jnp.concatenate([jnp.concatenate([vt_band, zeros], axis=1),
                                    jnp.concatenate([zeros, vt_band], axis=1)], axis=0)
            for pr in range(pairs_per_kv):
                pair = hkv * pairs_per_kv + pr
                cols = slice(pair * LANE, (pair + 1) * LANE)
                st = _dot_nt(k_stack, q_ref[rows, cols])
                ps, rden = [], []
                for e in range(2):
                    hq = 2 * pair + e
                    s = jnp.where(valid, st[e * 2 * BLOCK:(e + 1) * 2 * BLOCK] + bias_ref[hq], NEG)
                    sink = sink_ref[hq] * LOG2E
                    m = jnp.maximum(jnp.max(s, axis=0, keepdims=True), sink)
                    p = jnp.exp2(s - m)
                    den = jnp.sum(p, axis=0, keepdims=True) + jnp.exp2(sink - m)
                    ps.append(p.astype(BF16))
                    rden.append(jnp.broadcast_to(1.0 / den, (hd, BLOCK)))
                ot = _dot(v_bd, jnp.concatenate(ps, axis=0)) * jnp.concatenate(rden, axis=0)
                o_ref[rows, cols] = ot.T.astype(BF16)


def _swa_core(q, kpad, vt, rel_table, sinks, batch, seq):
    n, dq = q.shape
    n_q = dq // SWA_HEAD_DIM
    n_kv = n_q // SWA_GROUP
    rows = SWA_QBLOCKS * BLOCK
    steps = seq // rows
    q3 = q.reshape(batch, seq, dq)
    kp3 = kpad.reshape(batch, seq, kpad.shape[1])
    bucket = jnp.asarray(_rel_buckets_t())
    smem = pl.BlockSpec(memory_space=pltpu.SMEM)
    prev = lambda t: jnp.maximum(SWA_QBLOCKS * t - 1, 0)
    out = pl.pallas_call(
        functools.partial(_swa_kernel, n_kv=n_kv),
        grid=(batch, steps),
        in_specs=[smem, smem,
                  pl.BlockSpec((2 * BLOCK, BLOCK), lambda b, t: (0, 0)),
                  pl.BlockSpec((None, rows, dq), lambda b, t: (b, t, 0)),
                  pl.BlockSpec((None, BLOCK, kpad.shape[1]), lambda b, t: (b, prev(t), 0)),
                  pl.BlockSpec((None, rows, kpad.shape[1]), lambda b, t: (b, t, 0)),
                  pl.BlockSpec((vt.shape[0], BLOCK),
                               lambda b, t: (0, b * steps * SWA_QBLOCKS + prev(t))),
                  pl.BlockSpec((vt.shape[0], rows), lambda b, t: (0, b * steps + t))],
        out_specs=pl.BlockSpec((None, rows, dq), lambda b, t: (b, t, 0)),
        out_shape=jax.ShapeDtypeStruct((batch, seq, dq), BF16),
        scratch_shapes=[pltpu.VMEM((n_q, 2 * BLOCK, BLOCK), F32)],
        compiler_params=_params("arbitrary", "arbitrary"),
        name="swa_core",
    )(rel_table, sinks, bucket, q3, kp3, kp3, vt, vt)
    return out.reshape(n, dq)


def kernel(x, a_w_in, a_w_gk2, a_b_gk, a_onorm, a_w_out, kv_norm, w_kv, b_w_q, b_sinks, b_w_out,
           rel_table, ln_mix, ln_mlp, w_up, w_down, ln_final):
    batch, seq, d = x.shape
    n_a = a_w_in.shape[0]
    n_b = b_w_q.shape[0]
    kd = a_w_gk2.shape[2]
    vd = (a_w_in.shape[2] - GLA_GATE_RANK - 2 * kd) // 2

    depth = n_a + n_b
    assert n_a >= 1
    half = w_kv.shape[1] // 2
    assert half == LANE and 2 * SWA_HEAD_DIM == LANE

    w_in_t = jnp.swapaxes(a_w_in, 1, 2)
    n_main = 2 * kd + 2 * vd

    w_glr = jnp.pad(w_in_t[:, n_main:, :], ((0, 0), (0, LANE - GLA_GATE_RANK), (0, 0))).astype(BF16)
    w_gk2 = jnp.pad(a_w_gk2, ((0, 0), (0, LANE - GLA_GATE_RANK), (0, 0))).astype(BF16)
    w_vt = w_kv[:, half:].T.astype(BF16)
    g_mix = ln_mix.reshape(depth, 1, d)
    g_mlp = ln_mlp.reshape(depth, 1, d)
    g_kv = kv_norm.reshape(1, d)
    g_final = ln_final.reshape(1, d)
    b_gk = a_b_gk.reshape(n_a, 1, kd)

    calls = []
    for layer in range(depth):
        is_gla = layer < n_a
        j = layer if is_gla else layer - n_a
        if is_gla:
            calls.append(("inproj", layer, {"w_in": (w_in_t, j, n_main, d)}))
        needs = {"w_out": (a_w_out if is_gla else b_w_out, j, d, d),
                 "w_up": (w_up, layer, d, w_up.shape[2]), "w_down": (w_down, layer, w_down.shape[1], d)}
        if n_a <= layer + 1 < depth:
            needs["w_q"] = (b_w_q, layer + 1 - n_a, d, b_w_q.shape[2])
        if layer + 1 == n_a:
            needs["w_k"] = (w_kv[None], 0, d, half)
        calls.append(("post", layer, needs))
    bf = {name: stack[idx, :rows, :cols].astype(BF16)
          for name, (stack, idx, rows, cols) in calls[0][2].items()}

    h = x.reshape(batch * seq, d)
    for c, (kind, layer, _) in enumerate(calls):
        is_gla = layer < n_a
        j = layer if is_gla else layer - n_a
        nxt = calls[c + 1][2] if c + 1 < len(calls) else {}
        tail = ("final" if layer == depth - 1 else "none" if layer + 1 < n_a
                else "kvq" if layer + 1 == n_a else "q")
        call_steps = batch * seq // (INPROJ_ROWS if kind == "inproj" else _POST_ROWS[tail])
        casts = [_Cast(stack, idx, rows, cols, call_steps) for stack, idx, rows, cols in nxt.values()]
        if kind == "inproj":
            proj, el, cast_out = _gla_inproj(h, j, layer, g_mix, bf["w_in"], w_glr, w_gk2, b_gk,
                                             kd, vd, casts)
            o = _gla_core(proj, el, a_onorm[j].reshape(1, -1), batch, seq, kd, vd)
        else:
            if not is_gla:
                o = _swa_core(q, kpad, vt, rel_table, b_sinks[j], batch, seq)
            params = [(bf["w_out"], _resident(bf["w_out"].shape)), (g_mlp, _layer(g_mlp, layer)),
                      (bf["w_up"], _resident(bf["w_up"].shape)),
                      (bf["w_down"], _resident(bf["w_down"].shape))]
            if tail == "final":
                (h,), cast_out = _post(h, o, params + [(g_final, _resident(g_final.shape))], tail, casts)
            elif tail == "none":
                (h,), cast_out = _post(h, o, params, tail, casts)
            else:
                params += [(g_mix, _layer(g_mix, layer + 1)), (bf["w_q"], _resident(bf["w_q"].shape))]
                if tail == "kvq":
                    params += [(g_kv, _resident(g_kv.shape)), (bf["w_k"], _resident(bf["w_k"].shape)),
                               (w_vt, _resident(w_vt.shape))]
                    (h, q, kpad, vt), cast_out = _post(h, o, params, tail, casts,
                                                       bf["w_q"].shape[1], w_vt.shape[0])
                else:
                    (h, q), cast_out = _post(h, o, params, tail, casts, bf["w_q"].shape[1])
        bf = dict(zip(nxt.keys(), cast_out))
    return h.reshape(batch, seq, d)
```

```python
import functools
import math

import numpy as np
import jax
import jax.numpy as jnp
from jax import lax
from jax.experimental import pallas as pl
from jax.experimental.pallas import tpu as pltpu

F32 = jnp.float32
BF16 = jnp.bfloat16

EPS = 1e-6
NEG = -1e30
LOG2E = math.log2(math.e)

GLA_HEADS = 4
GLA_GATE_RANK = 16
GLA_GATE_NORM = 16.0
GLA_CHUNK = 64

SWA_HEAD_DIM = 64
SWA_GROUP = 8
WINDOW = 128
BLOCK = 128
REL_BUCKETS = 32
REL_MAX_DIST = 128

LANE = 128
ROW_TILE = 512
COL_CHUNK = 512
GLA_ROWS = 1024
SWA_QBLOCKS = 8
VMEM_CAPACITY = 64 * 1024 * 1024
PIPELINE_BUFFERS = 2
VALUE_SPILL_BYTES = 8 * 1024 * 1024

_NT = (((1,), (1,)), ((), ()))
_TN = (((0,), (0,)), ((), ()))


def _rms(x, g):
    ms = jnp.mean(x * x, axis=-1, keepdims=True)
    return x * lax.rsqrt(ms + EPS) * g


def _dot(a, b):
    return jnp.dot(a, b, preferred_element_type=F32)


def _dot_nt(a, b):
    return lax.dot_general(a, b, _NT, preferred_element_type=F32)


def _dot_tn(a, b):
    return lax.dot_general(a, b, _TN, preferred_element_type=F32)


def _call(body, name, grid, semantics, in_specs, inputs, out_specs, out_shape, scratch_shapes=()):
    windows = list(zip(in_specs, inputs)) + list(zip(out_specs, out_shape))
    vmem = VALUE_SPILL_BYTES + sum(math.prod(s.shape) * jnp.dtype(s.dtype).itemsize
                                   for s in scratch_shapes)
    for spec, arr in windows:
        if spec.block_shape is not None:
            block = math.prod(1 if b is None else b for b in spec.block_shape)
            buffers = spec.pipeline_mode.buffer_count if spec.pipeline_mode else PIPELINE_BUFFERS
            vmem += block * jnp.dtype(arr.dtype).itemsize * buffers
    assert vmem <= VMEM_CAPACITY, (name, vmem)
    return pl.pallas_call(
        body, grid=grid, in_specs=in_specs, out_specs=out_specs, out_shape=out_shape,
        scratch_shapes=list(scratch_shapes),
        compiler_params=pltpu.CompilerParams(dimension_semantics=semantics, vmem_limit_bytes=vmem),
        name=name,
    )(*inputs)


def _resident(shape):
    zeros = (0,) * len(shape)
    return pl.BlockSpec(shape, lambda *_: zeros, pipeline_mode=pl.Buffered(1))


def _layer(stacked, layer):
    idx = (layer,) + (0,) * (stacked.ndim - 1)
    return pl.BlockSpec((None,) + stacked.shape[1:], lambda *_: idx, pipeline_mode=pl.Buffered(1))


class _Cast:
    def __init__(self, stack, layer, rows, cols, steps):
        assert rows % steps == 0 and cols % LANE == 0
        block = (rows // steps, cols)
        self.stack = stack
        self.in_spec = pl.BlockSpec((None,) + block, lambda i: (layer, i, 0))
        self.out_spec = pl.BlockSpec(block, lambda i: (i, 0))
        self.out_shape = jax.ShapeDtypeStruct((rows, cols), BF16)


def _run_casts(refs, n_in, n_cast):
    n_out = len(refs) - n_in - 2 * n_cast
    for src, dst in zip(refs[n_in:n_in + n_cast], refs[n_in + n_cast + n_out:]):
        dst[...] = src[...].astype(BF16)
    return refs[:n_in], refs[n_in + n_cast:n_in + n_cast + n_out]


def _store_q(xn, wq_ref, q_ref):
    scale = SWA_HEAD_DIM ** -0.5 * LOG2E
    for j in range(0, q_ref.shape[-1], COL_CHUNK):
        q_ref[:, j:j + COL_CHUNK] = (_dot(xn, wq_ref[:, j:j + COL_CHUNK]) * scale).astype(BF16)


def _store_kv(xn, wk_ref, wvt_ref, kp_ref, vt_ref):
    k = _dot(xn, wk_ref[...])
    kr = pltpu.roll(k, SWA_HEAD_DIM, axis=1)
    low = lax.broadcasted_iota(jnp.int32, k.shape, 1) < SWA_HEAD_DIM
    pieces = (jnp.where(low, k, 0.0), jnp.where(low, 0.0, kr),
              jnp.where(low, kr, 0.0), jnp.where(low, 0.0, k))
    for i, piece in enumerate(pieces):
        kp_ref[:, i * LANE:(i + 1) * LANE] = piece.astype(BF16)
    vt_ref[...] = _dot_nt(wvt_ref[...], xn).astype(BF16)


def _gla_inproj_kernel(*refs, kd, vd, hk, n_cast):
    (x_ref, g_ref, w_ref, wglr_ref, wgk2_ref, bgk_ref, tril_ref), (o_ref, el_ref) = _run_casts(
        refs, 7, n_cast)
    tm = x_ref.shape[0]
    c = GLA_CHUNK
    xn = _rms(x_ref[...], g_ref[...]).astype(BF16)
    glr = _dot_nt(xn, wglr_ref[...])
    z = _dot(glr.astype(BF16), wgk2_ref[...]) + bgk_ref[...]
    gk = (jnp.minimum(z, 0.0) - jnp.log(1.0 + jnp.exp(-jnp.abs(z)))) * (LOG2E / GLA_GATE_NORM)
    hi = gk.astype(BF16)
    mid = (gk - hi.astype(F32)).astype(BF16)
    bcum = _dot(tril_ref[...], hi) + _dot(tril_ref[...], mid)
    for j in range(0, vd, COL_CHUNK):
        o_ref[:, j:j + COL_CHUNK] = _dot_nt(
            xn, w_ref[2 * kd + j:2 * kd + j + COL_CHUNK, :]).astype(BF16)
    for j in range(vd, 2 * vd, COL_CHUNK):
        gt = _dot_nt(xn, w_ref[2 * kd + j:2 * kd + j + COL_CHUNK, :])
        o_ref[:, j:j + COL_CHUNK] = (gt / (1.0 + jnp.exp(-gt))).astype(BF16)
    q = _dot_nt(xn, w_ref[:kd, :])
    o_ref[:, 2 * vd:2 * vd + kd] = ((q * (hk ** -0.5)) * jnp.exp2(bcum)).astype(BF16)
    k = _dot_nt(xn, w_ref[kd:2 * kd, :])
    kt = k * jnp.exp2(-bcum)
    o_ref[:, 2 * vd + kd:2 * vd + 2 * kd] = kt.astype(BF16)
    for i in range(tm // c):
        el = jnp.exp2(bcum[(i + 1) * c - 1:(i + 1) * c, :])
        el_ref[i:i + 1, :] = el
        o_ref[i * c:(i + 1) * c, 2 * vd + 2 * kd:2 * vd + 3 * kd] = (kt[i * c:(i + 1) * c] * el).astype(BF16)


def _gla_inproj(x, layer, mix_layer, g_mix, w_in, w_glr, w_gk2, b_gk, kd, vd, casts):
    n, d = x.shape
    n_out = 2 * vd + 3 * kd
    per_tile = ROW_TILE // GLA_CHUNK
    r = np.arange(ROW_TILE)
    tril = jnp.asarray((r[:, None] >= r[None, :]) & (r[:, None] // GLA_CHUNK == r[None, :] // GLA_CHUNK),
                       dtype=BF16)
    proj, el, *cast_out = _call(
        functools.partial(_gla_inproj_kernel, kd=kd, vd=vd, hk=kd // GLA_HEADS, n_cast=len(casts)),
        "gla_inproj", (n // ROW_TILE,), ("parallel",),
        in_specs=[pl.BlockSpec((ROW_TILE, d), lambda i: (i, 0)),
                  _layer(g_mix, mix_layer), _resident(w_in.shape), _layer(w_glr, layer),
                  _layer(w_gk2, layer), _layer(b_gk, layer), _resident(tril.shape)]
                 + [c.in_spec for c in casts],
        inputs=[x, g_mix, w_in, w_glr, w_gk2, b_gk, tril] + [c.stack for c in casts],
        out_specs=[pl.BlockSpec((ROW_TILE, n_out), lambda i: (i, 0)),
                   pl.BlockSpec((per_tile, kd), lambda i: (i, 0))] + [c.out_spec for c in casts],
        out_shape=[jax.ShapeDtypeStruct((n, n_out), BF16),
                   jax.ShapeDtypeStruct((n // GLA_CHUNK, kd), F32)] + [c.out_shape for c in casts])
    return proj, el, cast_out


def _gla_kernel(v_ref, sg_ref, qt_ref, kt_ref, kd_ref, el_ref, gon_ref, o_ref, st_ref, *, hk, hv):
    @pl.when(pl.program_id(1) == 0)
    def _():
        st_ref[...] = jnp.zeros_like(st_ref)

    c = GLA_CHUNK
    hc = GLA_HEADS * c
    ri = lax.broadcasted_iota(jnp.int32, (hc, hc), 0)
    ci = lax.broadcasted_iota(jnp.int32, (hc, hc), 1)
    causal = (ri >= ci) & (ri // c == ci // c)
    gon = gon_ref[...]
    for i in range(v_ref.shape[0] // c):
        rows = slice(i * c, (i + 1) * c)
        el = el_ref[i:i + 1, :]
        heads_k = [slice(h * hk, (h + 1) * hk) for h in range(GLA_HEADS)]
        heads_v = [slice(h * hv, (h + 1) * hv) for h in range(GLA_HEADS)]
        q_stack = jnp.concatenate([qt_ref[rows, ks] for ks in heads_k], axis=0)
        k_stack = jnp.concatenate([kt_ref[rows, ks] for ks in heads_k], axis=0)
        v_stack = jnp.concatenate([v_ref[rows, vs] for vs in heads_v], axis=0)
        attn = jnp.where(causal, _dot_nt(q_stack, k_stack), 0.0).astype(BF16)
        o_intra = _dot(attn, v_stack)
        for h in range(GLA_HEADS):
            ks, vs = heads_k[h], heads_v[h]
            st = st_ref[h]
            o = o_intra[h * c:(h + 1) * c] + _dot_nt(qt_ref[rows, ks], st.astype(BF16))
            st_ref[h] = st * el[:, ks] + _dot_tn(v_ref[rows, vs], kd_ref[rows, ks])
            o = o * lax.rsqrt(jnp.mean(o * o, axis=-1, keepdims=True) + EPS) * gon
            o_ref[rows, vs] = (o * sg_ref[rows, vs].astype(F32)).astype(BF16)


def _gla_core(proj, el, gon, batch, seq, kd, vd):
    n = proj.shape[0]
    hk = kd // GLA_HEADS
    hv = vd // GLA_HEADS
    proj3 = proj.reshape(batch, seq, proj.shape[1])
    el3 = el.reshape(batch, seq // GLA_CHUNK, kd)
    kblk = 2 * vd // kd
    (out,) = _call(
        functools.partial(_gla_kernel, hk=hk, hv=hv),
        "gla_core", (batch, seq // GLA_ROWS), ("parallel", "arbitrary"),
        in_specs=[pl.BlockSpec((None, GLA_ROWS, vd), lambda b, t: (b, t, 0)),
                  pl.BlockSpec((None, GLA_ROWS, vd), lambda b, t: (b, t, 1)),
                  pl.BlockSpec((None, GLA_ROWS, kd), lambda b, t: (b, t, kblk)),
                  pl.BlockSpec((None, GLA_ROWS, kd), lambda b, t: (b, t, kblk + 1)),
                  pl.BlockSpec((None, GLA_ROWS, kd), lambda b, t: (b, t, kblk + 2)),
                  pl.BlockSpec((None, GLA_ROWS // GLA_CHUNK, kd), lambda b, t: (b, t, 0)),
                  pl.BlockSpec((1, hv), lambda b, t: (0, 0))],
        inputs=[proj3, proj3, proj3, proj3, proj3, el3, gon],
        out_specs=[pl.BlockSpec((None, GLA_ROWS, vd), lambda b, t: (b, t, 0))],
        out_shape=[jax.ShapeDtypeStruct((batch, seq, vd), BF16)],
        scratch_shapes=[pltpu.VMEM((GLA_HEADS, hv, hk), F32)])
    return out.reshape(n, vd)


_POST_TAIL_INPUTS = {"none": 0, "final": 1, "q": 2, "kvq": 5}


def _post_kernel(*refs, tail, n_cast):
    ins, outs = _run_casts(refs, 6 + _POST_TAIL_INPUTS[tail], n_cast)
    h_ref, o_ref, wo_ref, g_ref, wup_ref, wdn_ref = ins[:6]
    rest = ins[6:] + outs
    h1 = h_ref[...] + _dot(o_ref[...], wo_ref[...])
    hn = _rms(h1, g_ref[...]).astype(BF16)
    acc = h1
    d_ff = wup_ref.shape[1]
    for j in range(0, d_ff, COL_CHUNK):
        u = jnp.maximum(_dot(hn, wup_ref[:, j:j + COL_CHUNK]), 0.0)
        acc = acc + _dot((u * u).astype(BF16), wdn_ref[j:j + COL_CHUNK, :])
    if tail == "none":
        (out_ref,) = rest
    elif tail == "final":
        gfin_ref, out_ref = rest
        acc = _rms(acc, gfin_ref[...])
    else:
        if tail == "kvq":
            gq_ref, wq_ref, gkv_ref, wk_ref, wvt_ref, out_ref, q_ref, kp_ref, vt_ref = rest
        else:
            gq_ref, wq_ref, out_ref, q_ref = rest
        xhat = acc * lax.rsqrt(jnp.mean(acc * acc, axis=-1, keepdims=True) + EPS)
        _store_q((xhat * gq_ref[...]).astype(BF16), wq_ref, q_ref)
        if tail == "kvq":
            _store_kv((xhat * gkv_ref[...]).astype(BF16), wk_ref, wvt_ref, kp_ref, vt_ref)
    out_ref[...] = acc


def _post(h, o, params, tail, casts, n_q_cols=0, n_vt_rows=0):
    n, d = h.shape
    assert len(params) == 4 + _POST_TAIL_INPUTS[tail]
    rows = lambda width: pl.BlockSpec((ROW_TILE, width), lambda i: (i, 0))
    out_specs = [rows(d)]
    out_shape = [jax.ShapeDtypeStruct((n, d), F32)]
    if tail in ("q", "kvq"):
        out_specs.append(rows(n_q_cols))
        out_shape.append(jax.ShapeDtypeStruct((n, n_q_cols), BF16))
    if tail == "kvq":
        kp_cols = 2 * LANE * (n_vt_rows // SWA_HEAD_DIM)
        out_specs += [rows(kp_cols), pl.BlockSpec((n_vt_rows, ROW_TILE), lambda i: (0, i))]
        out_shape += [jax.ShapeDtypeStruct((n, kp_cols), BF16),
                      jax.ShapeDtypeStruct((n_vt_rows, n), BF16)]
    n_main = len(out_specs)
    outs = _call(
        functools.partial(_post_kernel, tail=tail, n_cast=len(casts)),
        "post_mlp", (n // ROW_TILE,), ("parallel",),
        in_specs=[rows(d), rows(o.shape[1])] + [spec for _, spec in params]
                 + [c.in_spec for c in casts],
        inputs=[h, o] + [arr for arr, _ in params] + [c.stack for c in casts],
        out_specs=out_specs + [c.out_spec for c in casts],
        out_shape=out_shape + [c.out_shape for c in casts])
    return outs[:n_main], outs[n_main:]


def _rel_buckets_t():
    j = np.arange(2 * BLOCK)[:, None]
    i = np.arange(BLOCK)[None, :]
    n = np.maximum(i + BLOCK - j, 0)
    max_exact = REL_BUCKETS // 2
    large = max_exact + (np.log(np.maximum(n, 1) / max_exact)
                         / np.log(REL_MAX_DIST / max_exact)
                         * (REL_BUCKETS - max_exact)).astype(np.int32)
    large = np.minimum(large, REL_BUCKETS - 1)
    return np.where(n < max_exact, n, large).astype(np.int32)


def _swa_kernel(rel_ref, sink_ref, bucket_ref, q_ref, kpp_ref, kpc_ref, vtp_ref, vtc_ref, o_ref,
                bias_ref, *, n_kv):
    n_q = bias_ref.shape[0]
    hd = SWA_HEAD_DIM
    pairs_per_kv = SWA_GROUP // 2

    @pl.when((pl.program_id(0) == 0) & (pl.program_id(1) == 0))
    def _():
        bucket = bucket_ref[...]

        def per_head(h, carry):
            def per_bucket(b, acc):
                return jnp.where(bucket == b, rel_ref[b, h] * LOG2E, acc)
            bias_ref[h] = lax.fori_loop(0, REL_BUCKETS, per_bucket,
                                        jnp.zeros(bucket.shape, F32))
            return carry

        lax.fori_loop(0, n_q, per_head, 0)

    kj = lax.broadcasted_iota(jnp.int32, (2 * BLOCK, BLOCK), 0)
    qi = lax.broadcasted_iota(jnp.int32, (2 * BLOCK, BLOCK), 1)
    dist = qi + BLOCK - kj
    in_window = (dist >= 0) & (dist < WINDOW)
    valid_first = in_window & ((pl.program_id(1) > 0) | (kj >= BLOCK))
    zeros = jnp.zeros((hd, 2 * BLOCK), BF16)
    for qb in range(SWA_QBLOCKS):
        rows = slice(qb * BLOCK, (qb + 1) * BLOCK)
        prev_rows = slice((qb - 1) * BLOCK, qb * BLOCK)
        valid = valid_first if qb == 0 else in_window
        for hkv in range(n_kv):
            def k_band(col):
                cols = slice(col * LANE, (col + 1) * LANE)
                prev = kpp_ref[:, cols] if qb == 0 else kpc_ref[prev_rows, cols]
                return [prev, kpc_ref[rows, cols]]
            k_stack = jnp.concatenate(k_band(2 * hkv) + k_band(2 * hkv + 1), axis=0)
            vrow = slice(hkv * hd, (hkv + 1) * hd)
            if qb == 0:
                vt_band = jnp.concatenate([vtp_ref[vrow, :], vtc_ref[vrow, :BLOCK]], axis=1)
            else:
                vt_band = vtc_ref[vrow, (qb - 1) * BLOCK:(qb + 1) * BLOCK]
            v_bd = jnp.concatenate([jnp.concatenate([vt_band, zeros], axis=1),
                                    jnp.concatenate([zeros, vt_band], axis=1)], axis=0)
            for pr in range(pairs_per_kv):
                pair = hkv * pairs_per_kv + pr
                cols = slice(pair * LANE, (pair + 1) * LANE)
                st = _dot_nt(k_stack, q_ref[rows, cols])
                ps, rden = [], []
                for e in range(2):
                    hq = 2 * pair + e
                    s = jnp.where(valid, st[e * 2 * BLOCK:(e + 1) * 2 * BLOCK] + bias_ref[hq], NEG)
                    sink = sink_ref[hq] * LOG2E
                    m = jnp.maximum(jnp.max(s, axis=0, keepdims=True), sink)
                    p = jnp.exp2(s - m)
                    den = jnp.sum(p, axis=0, keepdims=True) + jnp.exp2(sink - m)
                    ps.append(p.astype(BF16))
                    rden.append(jnp.broadcast_to(1.0 / den, (hd, BLOCK)))
                ot = _dot(v_bd, jnp.concatenate(ps, axis=0)) * jnp.concatenate(rden, axis=0)
                o_ref[rows, cols] = ot.T.astype(BF16)


def _swa_core(q, kpad, vt, rel_table, sinks, batch, seq):
    n, dq = q.shape
    n_q = dq // SWA_HEAD_DIM
    n_kv = n_q // SWA_GROUP
    rows = SWA_QBLOCKS * BLOCK
    steps = seq // rows
    q3 = q.reshape(batch, seq, dq)
    kp3 = kpad.reshape(batch, seq, kpad.shape[1])
    bucket = jnp.asarray(_rel_buckets_t())
    smem = pl.BlockSpec(memory_space=pltpu.SMEM)
    prev = lambda t: jnp.maximum(SWA_QBLOCKS * t - 1, 0)
    (out,) = _call(
        functools.partial(_swa_kernel, n_kv=n_kv),
        "swa_core", (batch, steps), ("arbitrary", "arbitrary"),
        in_specs=[smem, smem,
                  pl.BlockSpec((2 * BLOCK, BLOCK), lambda b, t: (0, 0)),
                  pl.BlockSpec((None, rows, dq), lambda b, t: (b, t, 0)),
                  pl.BlockSpec((None, BLOCK, kpad.shape[1]), lambda b, t: (b, prev(t), 0)),
                  pl.BlockSpec((None, rows, kpad.shape[1]), lambda b, t: (b, t, 0)),
                  pl.BlockSpec((vt.shape[0], BLOCK),
                               lambda b, t: (0, b * steps * SWA_QBLOCKS + prev(t))),
                  pl.BlockSpec((vt.shape[0], rows), lambda b, t: (0, b * steps + t))],
        inputs=[rel_table, sinks, bucket, q3, kp3, kp3, vt, vt],
        out_specs=[pl.BlockSpec((None, rows, dq), lambda b, t: (b, t, 0))],
        out_shape=[jax.ShapeDtypeStruct((batch, seq, dq), BF16)],
        scratch_shapes=[pltpu.VMEM((n_q, 2 * BLOCK, BLOCK), F32)])
    return out.reshape(n, dq)


def kernel(x, a_w_in, a_w_gk2, a_b_gk, a_onorm, a_w_out, kv_norm, w_kv, b_w_q, b_sinks, b_w_out,
           rel_table, ln_mix, ln_mlp, w_up, w_down, ln_final):
    batch, seq, d = x.shape
    n_a = a_w_in.shape[0]
    n_b = b_w_q.shape[0]
    kd = a_w_gk2.shape[2]
    vd = (a_w_in.shape[2] - GLA_GATE_RANK - 2 * kd) // 2

    depth = n_a + n_b
    assert n_a >= 1
    half = w_kv.shape[1] // 2
    assert half == LANE and 2 * SWA_HEAD_DIM == LANE

    w_in_t = jnp.swapaxes(a_w_in, 1, 2)
    n_main = 2 * kd + 2 * vd

    w_glr = jnp.pad(w_in_t[:, n_main:, :], ((0, 0), (0, LANE - GLA_GATE_RANK), (0, 0))).astype(BF16)
    w_gk2 = jnp.pad(a_w_gk2, ((0, 0), (0, LANE - GLA_GATE_RANK), (0, 0))).astype(BF16)
    w_vt = w_kv[:, half:].T.astype(BF16)
    g_mix = ln_mix.reshape(depth, 1, d)
    g_mlp = ln_mlp.reshape(depth, 1, d)
    g_kv = kv_norm.reshape(1, d)
    g_final = ln_final.reshape(1, d)
    b_gk = a_b_gk.reshape(n_a, 1, kd)

    calls = []
    for layer in range(depth):
        is_gla = layer < n_a
        j = layer if is_gla else layer - n_a
        if is_gla:
            calls.append(("inproj", layer, {"w_in": (w_in_t, j, n_main, d)}))
        needs = {"w_out": (a_w_out if is_gla else b_w_out, j, d, d),
                 "w_up": (w_up, layer, d, w_up.shape[2]), "w_down": (w_down, layer, w_down.shape[1], d)}
        if n_a <= layer + 1 < depth:
            needs["w_q"] = (b_w_q, layer + 1 - n_a, d, b_w_q.shape[2])
        if layer + 1 == n_a:
            needs["w_k"] = (w_kv[None], 0, d, half)
        calls.append(("post", layer, needs))
    steps = batch * seq // ROW_TILE
    bf = {name: stack[idx, :rows, :cols].astype(BF16)
          for name, (stack, idx, rows, cols) in calls[0][2].items()}

    h = x.reshape(batch * seq, d)
    for c, (kind, layer, _) in enumerate(calls):
        is_gla = layer < n_a
        j = layer if is_gla else layer - n_a
        nxt = calls[c + 1][2] if c + 1 < len(calls) else {}
        casts = [_Cast(stack, idx, rows, cols, steps) for stack, idx, rows, cols in nxt.values()]
        if kind == "inproj":
            proj, el, cast_out = _gla_inproj(h, j, layer, g_mix, bf["w_in"], w_glr, w_gk2, b_gk,
                                             kd, vd, casts)
            o = _gla_core(proj, el, a_onorm[j].reshape(1, -1), batch, seq, kd, vd)
        else:
            if not is_gla:
                o = _swa_core(q, kpad, vt, rel_table, b_sinks[j], batch, seq)
            params = [(bf["w_out"], _resident(bf["w_out"].shape)), (g_mlp, _layer(g_mlp, layer)),
                      (bf["w_up"], _resident(bf["w_up"].shape)),
                      (bf["w_down"], _resident(bf["w_down"].shape))]
            if layer == depth - 1:
                (h,), cast_out = _post(h, o, params + [(g_final, _resident(g_final.shape))],
                                       "final", casts)
            elif layer + 1 < n_a:
                (h,), cast_out = _post(h, o, params, "none", casts)
            else:
                params += [(g_mix, _layer(g_mix, layer + 1)), (bf["w_q"], _resident(bf["w_q"].shape))]
                if layer + 1 == n_a:
                    params += [(g_kv, _resident(g_kv.shape)), (bf["w_k"], _resident(bf["w_k"].shape)),
                               (w_vt, _resident(w_vt.shape))]
                    (h, q, kpad, vt), cast_out = _post(h, o, params, "kvq", casts,
                                                       bf["w_q"].shape[1], w_vt.shape[0])
                else:
                    (h, q), cast_out = _post(h, o, params, "q", casts, bf["w_q"].shape[1])
        bf = dict(zip(nxt.keys(), cast_out))
    return h.reshape(batch, seq, d)
```

```python
import functools
import math

import numpy as np
import jax
import jax.numpy as jnp
from jax import lax
from jax.experimental import pallas as pl
from jax.experimental.pallas import tpu as pltpu

F32 = jnp.float32
BF16 = jnp.bfloat16

EPS = 1e-6
NEG = -1e30
LOG2E = math.log2(math.e)

GLA_HEADS = 4
GLA_GATE_RANK = 16
GLA_GATE_NORM = 16.0
GLA_CHUNK = 64

SWA_HEAD_DIM = 64
SWA_GROUP = 8
WINDOW = 128
BLOCK = 128
REL_BUCKETS = 32
REL_MAX_DIST = 128

LANE = 128
SUBLANES = 8
ROW_TILE = 512
COL_CHUNK = 512
GLA_ROWS = 1024
SWA_QBLOCKS = 8
VMEM_LIMIT = 56 * 1024 * 1024

_NT = (((1,), (1,)), ((), ()))
_TN = (((0,), (0,)), ((), ()))


def _rms(x, g):
    ms = jnp.mean(x * x, axis=-1, keepdims=True)
    return x * lax.rsqrt(ms + EPS) * g


def _dot(a, b):
    return jnp.dot(a, b, preferred_element_type=F32)


def _dot_nt(a, b):
    return lax.dot_general(a, b, _NT, preferred_element_type=F32)


def _dot_tn(a, b):
    return lax.dot_general(a, b, _TN, preferred_element_type=F32)


def _params(*sem):
    return pltpu.CompilerParams(dimension_semantics=sem, vmem_limit_bytes=VMEM_LIMIT)


def _resident(shape):
    zeros = (0,) * len(shape)
    return pl.BlockSpec(shape, lambda *_: zeros, pipeline_mode=pl.Buffered(1))


def _layer(stacked, layer):
    idx = (layer,) + (0,) * (stacked.ndim - 1)
    return pl.BlockSpec((None,) + stacked.shape[1:], lambda *_: idx, pipeline_mode=pl.Buffered(1))


class _Cast:
    def __init__(self, stack, layer, rows, cols, steps):
        assert rows % steps == 0 and cols % LANE == 0
        block = (rows // steps, cols)
        self.stack = stack
        self.in_spec = pl.BlockSpec((None,) + block, lambda i: (layer, i, 0))
        self.out_spec = pl.BlockSpec(block, lambda i: (i, 0))
        self.out_shape = jax.ShapeDtypeStruct((rows, cols), BF16)


def _run_casts(refs, n_in, n_cast):
    n_out = len(refs) - n_in - 2 * n_cast
    for src, dst in zip(refs[n_in:n_in + n_cast], refs[n_in + n_cast + n_out:]):
        dst[...] = src[...].astype(BF16)
    return refs[:n_in], refs[n_in + n_cast:n_in + n_cast + n_out]


def _store_q(xn, wq_ref, q_ref):
    scale = SWA_HEAD_DIM ** -0.5 * LOG2E
    for j in range(0, q_ref.shape[-1], COL_CHUNK):
        q_ref[:, j:j + COL_CHUNK] = (_dot(xn, wq_ref[:, j:j + COL_CHUNK]) * scale).astype(BF16)


def _store_kv(xn, wk_ref, wvt_ref, kp_ref, vt_ref):
    k = _dot(xn, wk_ref[...])
    kr = pltpu.roll(k, SWA_HEAD_DIM, axis=1)
    low = lax.broadcasted_iota(jnp.int32, k.shape, 1) < SWA_HEAD_DIM
    pieces = (jnp.where(low, k, 0.0), jnp.where(low, 0.0, kr),
              jnp.where(low, kr, 0.0), jnp.where(low, 0.0, k))
    for i, piece in enumerate(pieces):
        kp_ref[:, i * LANE:(i + 1) * LANE] = piece.astype(BF16)
    vt_ref[...] = _dot_nt(wvt_ref[...], xn).astype(BF16)


def _gla_inproj_kernel(*refs, kd, vd, hk, n_cast):
    (x_ref, g_ref, w_ref, wglr_ref, wgk2_ref, bgk_ref), (o_ref, el_ref) = _run_casts(refs, 6, n_cast)
    tm = x_ref.shape[0]
    c = GLA_CHUNK
    xn = _rms(x_ref[...], g_ref[...]).astype(BF16)
    glr = _dot_nt(xn, wglr_ref[...])
    z = _dot(glr.astype(BF16), wgk2_ref[...]) + bgk_ref[...]
    for j in range(0, vd, COL_CHUNK):
        o_ref[:, j:j + COL_CHUNK] = _dot_nt(
            xn, w_ref[2 * kd + j:2 * kd + j + COL_CHUNK, :]).astype(BF16)
    for j in range(vd, 2 * vd, COL_CHUNK):
        gt = _dot_nt(xn, w_ref[2 * kd + j:2 * kd + j + COL_CHUNK, :])
        o_ref[:, j:j + COL_CHUNK] = (gt / (1.0 + jnp.exp(-gt))).astype(BF16)
    gk = (jnp.minimum(z, 0.0) - jnp.log(1.0 + jnp.exp(-jnp.abs(z)))) * (LOG2E / GLA_GATE_NORM)
    groups = c // SUBLANES
    g4 = gk.reshape(tm // c, groups, SUBLANES, kd)
    sub = lax.broadcasted_iota(jnp.int32, g4.shape, 2)
    shift = 1
    while shift < SUBLANES:
        g4 = g4 + jnp.where(sub >= shift, pltpu.roll(g4, shift, axis=2), 0.0)
        shift *= 2
    pieces, carry = [], None
    for g in range(groups):
        piece = g4[:, g] if carry is None else g4[:, g] + carry
        pieces.append(piece)
        carry = piece[:, SUBLANES - 1:, :]
    bcum = jnp.stack(pieces, axis=1).reshape(tm, kd)
    q = _dot_nt(xn, w_ref[:kd, :])
    o_ref[:, 2 * vd:2 * vd + kd] = ((q * (hk ** -0.5)) * jnp.exp2(bcum)).astype(BF16)
    k = _dot_nt(xn, w_ref[kd:2 * kd, :])
    kt = k * jnp.exp2(-bcum)
    o_ref[:, 2 * vd + kd:2 * vd + 2 * kd] = kt.astype(BF16)
    for i in range(tm // c):
        el = jnp.exp2(bcum[(i + 1) * c - 1:(i + 1) * c, :])
        el_ref[i:i + 1, :] = el
        o_ref[i * c:(i + 1) * c, 2 * vd + 2 * kd:2 * vd + 3 * kd] = (kt[i * c:(i + 1) * c] * el).astype(BF16)


def _gla_inproj(x, layer, mix_layer, g_mix, w_in, w_glr, w_gk2, b_gk, kd, vd, casts):
    n, d = x.shape
    n_out = 2 * vd + 3 * kd
    per_tile = ROW_TILE // GLA_CHUNK
    proj, el, *cast_out = pl.pallas_call(
        functools.partial(_gla_inproj_kernel, kd=kd, vd=vd, hk=kd // GLA_HEADS, n_cast=len(casts)),
        grid=(n // ROW_TILE,),
        in_specs=[pl.BlockSpec((ROW_TILE, d), lambda i: (i, 0)),
                  _layer(g_mix, mix_layer), _resident(w_in.shape), _layer(w_glr, layer),
                  _layer(w_gk2, layer), _layer(b_gk, layer)]
                 + [c.in_spec for c in casts],
        out_specs=[pl.BlockSpec((ROW_TILE, n_out), lambda i: (i, 0)),
                   pl.BlockSpec((per_tile, kd), lambda i: (i, 0))] + [c.out_spec for c in casts],
        out_shape=[jax.ShapeDtypeStruct((n, n_out), BF16),
                   jax.ShapeDtypeStruct((n // GLA_CHUNK, kd), F32)] + [c.out_shape for c in casts],
        compiler_params=_params("parallel"),
        name="gla_inproj",
    )(x, g_mix, w_in, w_glr, w_gk2, b_gk, *[c.stack for c in casts])
    return proj, el, cast_out


def _gla_kernel(v_ref, sg_ref, qt_ref, kt_ref, kd_ref, el_ref, gon_ref, o_ref, st_ref, *, hk, hv):
    @pl.when(pl.program_id(1) == 0)
    def _():
        st_ref[...] = jnp.zeros_like(st_ref)

    c = GLA_CHUNK
    hc = GLA_HEADS * c
    ri = lax.broadcasted_iota(jnp.int32, (hc, hc), 0)
    ci = lax.broadcasted_iota(jnp.int32, (hc, hc), 1)
    causal = (ri >= ci) & (ri // c == ci // c)
    gon = gon_ref[...]
    for i in range(v_ref.shape[0] // c):
        rows = slice(i * c, (i + 1) * c)
        el = el_ref[i:i + 1, :]
        heads_k = [slice(h * hk, (h + 1) * hk) for h in range(GLA_HEADS)]
        heads_v = [slice(h * hv, (h + 1) * hv) for h in range(GLA_HEADS)]
        q_stack = jnp.concatenate([qt_ref[rows, ks] for ks in heads_k], axis=0)
        k_stack = jnp.concatenate([kt_ref[rows, ks] for ks in heads_k], axis=0)
        v_stack = jnp.concatenate([v_ref[rows, vs] for vs in heads_v], axis=0)
        attn = jnp.where(causal, _dot_nt(q_stack, k_stack), 0.0).astype(BF16)
        o_intra = _dot(attn, v_stack)
        for h in range(GLA_HEADS):
            ks, vs = heads_k[h], heads_v[h]
            st = st_ref[h]
            o = o_intra[h * c:(h + 1) * c] + _dot_nt(qt_ref[rows, ks], st.astype(BF16))
            st_ref[h] = st * el[:, ks] + _dot_tn(v_ref[rows, vs], kd_ref[rows, ks])
            o = o * lax.rsqrt(jnp.mean(o * o, axis=-1, keepdims=True) + EPS) * gon
            o_ref[rows, vs] = (o * sg_ref[rows, vs].astype(F32)).astype(BF16)


def _gla_core(proj, el, gon, batch, seq, kd, vd):
    n = proj.shape[0]
    hk = kd // GLA_HEADS
    hv = vd // GLA_HEADS
    proj3 = proj.reshape(batch, seq, proj.shape[1])
    el3 = el.reshape(batch, seq // GLA_CHUNK, kd)
    kblk = 2 * vd // kd
    out = pl.pallas_call(
        functools.partial(_gla_kernel, hk=hk, hv=hv),
        grid=(batch, seq // GLA_ROWS),
        in_specs=[pl.BlockSpec((None, GLA_ROWS, vd), lambda b, t: (b, t, 0)),
                  pl.BlockSpec((None, GLA_ROWS, vd), lambda b, t: (b, t, 1)),
                  pl.BlockSpec((None, GLA_ROWS, kd), lambda b, t: (b, t, kblk)),
                  pl.BlockSpec((None, GLA_ROWS, kd), lambda b, t: (b, t, kblk + 1)),
                  pl.BlockSpec((None, GLA_ROWS, kd), lambda b, t: (b, t, kblk + 2)),
                  pl.BlockSpec((None, GLA_ROWS // GLA_CHUNK, kd), lambda b, t: (b, t, 0)),
                  pl.BlockSpec((1, hv), lambda b, t: (0, 0))],
        out_specs=pl.BlockSpec((None, GLA_ROWS, vd), lambda b, t: (b, t, 0)),
        out_shape=jax.ShapeDtypeStruct((batch, seq, vd), BF16),
        scratch_shapes=[pltpu.VMEM((GLA_HEADS, hv, hk), F32)],
        compiler_params=_params("parallel", "arbitrary"),
        name="gla_core",
    )(proj3, proj3, proj3, proj3, proj3, el3, gon)
    return out.reshape(n, vd)


_POST_TAIL_INPUTS = {"none": 0, "final": 1, "q": 2, "kvq": 5}


def _post_kernel(*refs, tail, n_cast):
    ins, outs = _run_casts(refs, 6 + _POST_TAIL_INPUTS[tail], n_cast)
    h_ref, o_ref, wo_ref, g_ref, wup_ref, wdn_ref = ins[:6]
    rest = ins[6:] + outs
    h1 = h_ref[...] + _dot(o_ref[...], wo_ref[...])
    hn = _rms(h1, g_ref[...]).astype(BF16)
    acc = h1
    d_ff = wup_ref.shape[1]
    for j in range(0, d_ff, COL_CHUNK):
        u = jnp.maximum(_dot(hn, wup_ref[:, j:j + COL_CHUNK]), 0.0)
        acc = acc + _dot((u * u).astype(BF16), wdn_ref[j:j + COL_CHUNK, :])
    if tail == "none":
        (out_ref,) = rest
    elif tail == "final":
        gfin_ref, out_ref = rest
        acc = _rms(acc, gfin_ref[...])
    else:
        if tail == "kvq":
            gq_ref, wq_ref, gkv_ref, wk_ref, wvt_ref, out_ref, q_ref, kp_ref, vt_ref = rest
        else:
            gq_ref, wq_ref, out_ref, q_ref = rest
        xhat = acc * lax.rsqrt(jnp.mean(acc * acc, axis=-1, keepdims=True) + EPS)
        _store_q((xhat * gq_ref[...]).astype(BF16), wq_ref, q_ref)
        if tail == "kvq":
            _store_kv((xhat * gkv_ref[...]).astype(BF16), wk_ref, wvt_ref, kp_ref, vt_ref)
    out_ref[...] = acc


def _post(h, o, params, tail, casts, n_q_cols=0, n_vt_rows=0):
    n, d = h.shape
    assert len(params) == 4 + _POST_TAIL_INPUTS[tail]
    rows = lambda width: pl.BlockSpec((ROW_TILE, width), lambda i: (i, 0))
    out_specs = [rows(d)]
    out_shape = [jax.ShapeDtypeStruct((n, d), F32)]
    if tail in ("q", "kvq"):
        out_specs.append(rows(n_q_cols))
        out_shape.append(jax.ShapeDtypeStruct((n, n_q_cols), BF16))
    if tail == "kvq":
        out_specs += [rows(4 * LANE), pl.BlockSpec((n_vt_rows, ROW_TILE), lambda i: (0, i))]
        out_shape += [jax.ShapeDtypeStruct((n, 4 * LANE), BF16),
                      jax.ShapeDtypeStruct((n_vt_rows, n), BF16)]
    n_main = len(out_specs)
    outs = pl.pallas_call(
        functools.partial(_post_kernel, tail=tail, n_cast=len(casts)),
        grid=(n // ROW_TILE,),
        in_specs=[rows(d), rows(o.shape[1])] + [spec for _, spec in params]
                 + [c.in_spec for c in casts],
        out_specs=out_specs + [c.out_spec for c in casts],
        out_shape=out_shape + [c.out_shape for c in casts],
        compiler_params=_params("parallel"),
        name="post_mlp",
    )(h, o, *[arr for arr, _ in params], *[c.stack for c in casts])
    return outs[:n_main], outs[n_main:]


def _rel_buckets_t():
    j = np.arange(2 * BLOCK)[:, None]
    i = np.arange(BLOCK)[None, :]
    n = np.maximum(i + BLOCK - j, 0)
    max_exact = REL_BUCKETS // 2
    large = max_exact + (np.log(np.maximum(n, 1) / max_exact)
                         / np.log(REL_MAX_DIST / max_exact)
                         * (REL_BUCKETS - max_exact)).astype(np.int32)
    large = np.minimum(large, REL_BUCKETS - 1)
    return np.where(n < max_exact, n, large).astype(np.int32)


def _swa_kernel(rel_ref, sink_ref, bucket_ref, q_ref, kpp_ref, kpc_ref, vtp_ref, vtc_ref, o_ref,
                bias_ref, *, n_kv):
    n_q = bias_ref.shape[0]
    hd = SWA_HEAD_DIM
    pairs_per_kv = SWA_GROUP // 2

    @pl.when((pl.program_id(0) == 0) & (pl.program_id(1) == 0))
    def _():
        bucket = bucket_ref[...]

        def per_head(h, carry):
            def per_bucket(b, acc):
                return jnp.where(bucket == b, rel_ref[b, h] * LOG2E, acc)
            bias_ref[h] = lax.fori_loop(0, REL_BUCKETS, per_bucket,
                                        jnp.zeros(bucket.shape, F32))
            return carry

        lax.fori_loop(0, n_q, per_head, 0)

    kj = lax.broadcasted_iota(jnp.int32, (2 * BLOCK, BLOCK), 0)
    qi = lax.broadcasted_iota(jnp.int32, (2 * BLOCK, BLOCK), 1)
    dist = qi + BLOCK - kj
    in_window = (dist >= 0) & (dist < WINDOW)
    valid_first = in_window & ((pl.program_id(1) > 0) | (kj >= BLOCK))
    zeros = jnp.zeros((hd, 2 * BLOCK), BF16)
    for qb in range(SWA_QBLOCKS):
        rows = slice(qb * BLOCK, (qb + 1) * BLOCK)
        prev_rows = slice((qb - 1) * BLOCK, qb * BLOCK)
        valid = valid_first if qb == 0 else in_window
        for hkv in range(n_kv):
            def k_band(col):
                cols = slice(col * LANE, (col + 1) * LANE)
                prev = kpp_ref[:, cols] if qb == 0 else kpc_ref[prev_rows, cols]
                return [prev, kpc_ref[rows, cols]]
            k_stack = jnp.concatenate(k_band(2 * hkv) + k_band(2 * hkv + 1), axis=0)
            vrow = slice(hkv * hd, (hkv + 1) * hd)
            if qb == 0:
                vt_band = jnp.concatenate([vtp_ref[vrow, :], vtc_ref[vrow, :BLOCK]], axis=1)
            else:
                vt_band = vtc_ref[vrow, (qb - 1) * BLOCK:(qb + 1) * BLOCK]
            v_bd = jnp.concatenate([jnp.concatenate([vt_band, zeros], axis=1),
                                    jnp.concatenate([zeros, vt_band], axis=1)], axis=0)
            for pr in range(pairs_per_kv):
                pair = hkv * pairs_per_kv + pr
                cols = slice(pair * LANE, (pair + 1) * LANE)
                st = _dot_nt(k_stack, q_ref[rows, cols])
                ps, rden = [], []
                for e in range(2):
                    hq = 2 * pair + e
                    s = jnp.where(valid, st[e * 2 * BLOCK:(e + 1) * 2 * BLOCK] + bias_ref[hq], NEG)
                    sink = sink_ref[hq] * LOG2E
                    m = jnp.maximum(jnp.max(s, axis=0, keepdims=True), sink)
                    p = jnp.exp2(s - m)
                    den = jnp.sum(p, axis=0, keepdims=True) + jnp.exp2(sink - m)
                    ps.append(p.astype(BF16))
                    rden.append(jnp.broadcast_to(1.0 / den, (hd, BLOCK)))
                ot = _dot(v_bd, jnp.concatenate(ps, axis=0)) * jnp.concatenate(rden, axis=0)
                o_ref[rows, cols] = ot.T.astype(BF16)


def _swa_core(q, kpad, vt, rel_table, sinks, batch, seq):
    n, dq = q.shape
    n_q = dq // SWA_HEAD_DIM
    n_kv = n_q // SWA_GROUP
    rows = SWA_QBLOCKS * BLOCK
    steps = seq // rows
    q3 = q.reshape(batch, seq, dq)
    kp3 = kpad.reshape(batch, seq, kpad.shape[1])
    bucket = jnp.asarray(_rel_buckets_t())
    smem = pl.BlockSpec(memory_space=pltpu.SMEM)
    prev = lambda t: jnp.maximum(SWA_QBLOCKS * t - 1, 0)
    out = pl.pallas_call(
        functools.partial(_swa_kernel, n_kv=n_kv),
        grid=(batch, steps),
        in_specs=[smem, smem,
                  pl.BlockSpec((2 * BLOCK, BLOCK), lambda b, t: (0, 0)),
                  pl.BlockSpec((None, rows, dq), lambda b, t: (b, t, 0)),
                  pl.BlockSpec((None, BLOCK, kpad.shape[1]), lambda b, t: (b, prev(t), 0)),
                  pl.BlockSpec((None, rows, kpad.shape[1]), lambda b, t: (b, t, 0)),
                  pl.BlockSpec((vt.shape[0], BLOCK),
                               lambda b, t: (0, b * steps * SWA_QBLOCKS + prev(t))),
                  pl.BlockSpec((vt.shape[0], rows), lambda b, t: (0, b * steps + t))],
        out_specs=pl.BlockSpec((None, rows, dq), lambda b, t: (b, t, 0)),
        out_shape=jax.ShapeDtypeStruct((batch, seq, dq), BF16),
        scratch_shapes=[pltpu.VMEM((n_q, 2 * BLOCK, BLOCK), F32)],
        compiler_params=_params("arbitrary", "arbitrary"),
        name="swa_core",
    )(rel_table, sinks, bucket, q3, kp3, kp3, vt, vt)
    return out.reshape(n, dq)


def kernel(x, a_w_in, a_w_gk2, a_b_gk, a_onorm, a_w_out, kv_norm, w_kv, b_w_q, b_sinks, b_w_out,
           rel_table, ln_mix, ln_mlp, w_up, w_down, ln_final):
    batch, seq, d = x.shape
    n_a = a_w_in.shape[0]
    n_b = b_w_q.shape[0]
    kd = a_w_gk2.shape[2]
    vd = (a_w_in.shape[2] - GLA_GATE_RANK - 2 * kd) // 2

    depth = n_a + n_b
    assert n_a >= 1
    half = w_kv.shape[1] // 2
    assert half == LANE and 2 * SWA_HEAD_DIM == LANE

    w_in_t = jnp.swapaxes(a_w_in, 1, 2)
    n_main = 2 * kd + 2 * vd

    w_glr = jnp.pad(w_in_t[:, n_main:, :], ((0, 0), (0, LANE - GLA_GATE_RANK), (0, 0))).astype(BF16)
    w_gk2 = jnp.pad(a_w_gk2, ((0, 0), (0, LANE - GLA_GATE_RANK), (0, 0))).astype(BF16)
    w_vt = w_kv[:, half:].T.astype(BF16)
    g_mix = ln_mix.reshape(depth, 1, d)
    g_mlp = ln_mlp.reshape(depth, 1, d)
    g_kv = kv_norm.reshape(1, d)
    g_final = ln_final.reshape(1, d)
    b_gk = a_b_gk.reshape(n_a, 1, kd)

    calls = []
    for layer in range(depth):
        is_gla = layer < n_a
        j = layer if is_gla else layer - n_a
        if is_gla:
            calls.append(("inproj", layer, {"w_in": (w_in_t, j, n_main, d)}))
        needs = {"w_out": (a_w_out if is_gla else b_w_out, j, d, d),
                 "w_up": (w_up, layer, d, w_up.shape[2]), "w_down": (w_down, layer, w_down.shape[1], d)}
        if n_a <= layer + 1 < depth:
            needs["w_q"] = (b_w_q, layer + 1 - n_a, d, b_w_q.shape[2])
        if layer + 1 == n_a:
            needs["w_k"] = (w_kv[None], 0, d, half)
        calls.append(("post", layer, needs))
    steps = batch * seq // ROW_TILE
    bf = {name: stack[idx, :rows, :cols].astype(BF16)
          for name, (stack, idx, rows, cols) in calls[0][2].items()}

    h = x.reshape(batch * seq, d)
    for c, (kind, layer, _) in enumerate(calls):
        is_gla = layer < n_a
        j = layer if is_gla else layer - n_a
        nxt = calls[c + 1][2] if c + 1 < len(calls) else {}
        casts = [_Cast(stack, idx, rows, cols, steps) for stack, idx, rows, cols in nxt.values()]
        if kind == "inproj":
            proj, el, cast_out = _gla_inproj(h, j, layer, g_mix, bf["w_in"], w_glr, w_gk2, b_gk,
                                             kd, vd, casts)
            o = _gla_core(proj, el, a_onorm[j].reshape(1, -1), batch, seq, kd, vd)
        else:
            if not is_gla:
                o = _swa_core(q, kpad, vt, rel_table, b_sinks[j], batch, seq)
            params = [(bf["w_out"], _resident(bf["w_out"].shape)), (g_mlp, _layer(g_mlp, layer)),
                      (bf["w_up"], _resident(bf["w_up"].shape)),
                      (bf["w_down"], _resident(bf["w_down"].shape))]
            if layer == depth - 1:
                (h,), cast_out = _post(h, o, params + [(g_final, _resident(g_final.shape))],
                                       "final", casts)
            elif layer + 1 < n_a:
                (h,), cast_out = _post(h, o, params, "none", casts)
            else:
                params += [(g_mix, _layer(g_mix, layer + 1)), (bf["w_q"], _resident(bf["w_q"].shape))]
                if layer + 1 == n_a:
                    params += [(g_kv, _resident(g_kv.shape)), (bf["w_k"], _resident(bf["w_k"].shape)),
                               (w_vt, _resident(w_vt.shape))]
                    (h, q, kpad, vt), cast_out = _post(h, o, params, "kvq", casts,
                                                       bf["w_q"].shape[1], w_vt.shape[0])
                else:
                    (h, q), cast_out = _post(h, o, params, "q", casts, bf["w_q"].shape[1])
        bf = dict(zip(nxt.keys(), cast_out))
    return h.reshape(batch, seq, d)
```

```python
import functools
import math

import numpy as np
import jax
import jax.numpy as jnp
from jax import lax
from jax.experimental import pallas as pl
from jax.experimental.pallas import tpu as pltpu

F32 = jnp.float32
BF16 = jnp.bfloat16

EPS = 1e-6
NEG = -1e30
LOG2E = math.log2(math.e)

GLA_HEADS = 4
GLA_GATE_RANK = 16
GLA_GATE_NORM = 16.0
GLA_CHUNK = 64

SWA_HEAD_DIM = 64
SWA_GROUP = 8
WINDOW = 128
BLOCK = 128
REL_BUCKETS = 32
REL_MAX_DIST = 128

LANE = 128
SUBLANES = 8
ROW_TILE = 512
COL_CHUNK = 512
GLA_ROWS = 1024
SWA_QBLOCKS = 8
VMEM_LIMIT = 56 * 1024 * 1024

_NT = (((1,), (1,)), ((), ()))
_TN = (((0,), (0,)), ((), ()))


def _rms(x, g):
    ms = jnp.mean(x * x, axis=-1, keepdims=True)
    return x * lax.rsqrt(ms + EPS) * g


def _dot(a, b):
    return jnp.dot(a, b, preferred_element_type=F32)


def _dot_nt(a, b):
    return lax.dot_general(a, b, _NT, preferred_element_type=F32)


def _dot_tn(a, b):
    return lax.dot_general(a, b, _TN, preferred_element_type=F32)


def _params(*sem):
    return pltpu.CompilerParams(dimension_semantics=sem, vmem_limit_bytes=VMEM_LIMIT)


def _resident(shape):
    zeros = (0,) * len(shape)
    return pl.BlockSpec(shape, lambda *_: zeros, pipeline_mode=pl.Buffered(1))


def _layer(stacked, layer):
    idx = (layer,) + (0,) * (stacked.ndim - 1)
    return pl.BlockSpec((None,) + stacked.shape[1:], lambda *_: idx, pipeline_mode=pl.Buffered(1))


class _Cast:
    def __init__(self, stack, layer, rows, cols, steps):
        assert rows % steps == 0 and cols % LANE == 0
        block = (rows // steps, cols)
        self.stack = stack
        self.in_spec = pl.BlockSpec((None,) + block, lambda i: (layer, i, 0))
        self.out_spec = pl.BlockSpec(block, lambda i: (i, 0))
        self.out_shape = jax.ShapeDtypeStruct((rows, cols), BF16)


def _run_casts(refs, n_in, n_cast):
    n_out = len(refs) - n_in - 2 * n_cast
    for src, dst in zip(refs[n_in:n_in + n_cast], refs[n_in + n_cast + n_out:]):
        dst[...] = src[...].astype(BF16)
    return refs[:n_in], refs[n_in + n_cast:n_in + n_cast + n_out]


def _store_q(xn, wq_ref, q_ref):
    scale = SWA_HEAD_DIM ** -0.5 * LOG2E
    for j in range(0, q_ref.shape[-1], COL_CHUNK):
        q_ref[:, j:j + COL_CHUNK] = (_dot(xn, wq_ref[:, j:j + COL_CHUNK]) * scale).astype(BF16)


def _store_kv(xn, wk_ref, wvt_ref, kp_ref, vt_ref):
    k = _dot(xn, wk_ref[...])
    kr = pltpu.roll(k, SWA_HEAD_DIM, axis=1)
    low = lax.broadcasted_iota(jnp.int32, k.shape, 1) < SWA_HEAD_DIM
    pieces = (jnp.where(low, k, 0.0), jnp.where(low, 0.0, kr),
              jnp.where(low, kr, 0.0), jnp.where(low, 0.0, k))
    for i, piece in enumerate(pieces):
        kp_ref[:, i * LANE:(i + 1) * LANE] = piece.astype(BF16)
    vt_ref[...] = _dot_nt(wvt_ref[...], xn).astype(BF16)


def _gla_inproj_kernel(*refs, kd, vd, hk, n_cast):
    (x_ref, g_ref, w_ref, wglr_ref, wgk2_ref, bgk_ref), (o_ref, el_ref) = _run_casts(refs, 6, n_cast)
    tm = x_ref.shape[0]
    c = GLA_CHUNK
    xn = _rms(x_ref[...], g_ref[...]).astype(BF16)
    glr = _dot_nt(xn, wglr_ref[...])
    z = _dot(glr.astype(BF16), wgk2_ref[...]) + bgk_ref[...]
    for j in range(0, vd, COL_CHUNK):
        o_ref[:, j:j + COL_CHUNK] = _dot_nt(
            xn, w_ref[2 * kd + j:2 * kd + j + COL_CHUNK, :]).astype(BF16)
    for j in range(vd, 2 * vd, COL_CHUNK):
        gt = _dot_nt(xn, w_ref[2 * kd + j:2 * kd + j + COL_CHUNK, :])
        o_ref[:, j:j + COL_CHUNK] = (gt / (1.0 + jnp.exp(-gt))).astype(BF16)
    gk = (jnp.minimum(z, 0.0) - jnp.log(1.0 + jnp.exp(-jnp.abs(z)))) * (LOG2E / GLA_GATE_NORM)
    groups = c // SUBLANES
    g4 = gk.reshape(tm // c, groups, SUBLANES, kd)
    sub = lax.broadcasted_iota(jnp.int32, g4.shape, 2)
    shift = 1
    while shift < SUBLANES:
        g4 = g4 + jnp.where(sub >= shift, pltpu.roll(g4, shift, axis=2), 0.0)
        shift *= 2
    pieces, carry = [], None
    for g in range(groups):
        piece = g4[:, g] if carry is None else g4[:, g] + carry
        pieces.append(piece)
        carry = piece[:, SUBLANES - 1:, :]
    bcum = jnp.stack(pieces, axis=1).reshape(tm, kd)
    q = _dot_nt(xn, w_ref[:kd, :])
    o_ref[:, 2 * vd:2 * vd + kd] = ((q * (hk ** -0.5)) * jnp.exp2(bcum)).astype(BF16)
    k = _dot_nt(xn, w_ref[kd:2 * kd, :])
    kt = k * jnp.exp2(-bcum)
    o_ref[:, 2 * vd + kd:2 * vd + 2 * kd] = kt.astype(BF16)
    for i in range(tm // c):
        el = jnp.exp2(bcum[(i + 1) * c - 1:(i + 1) * c, :])
        el_ref[i:i + 1, :] = el
        o_ref[i * c:(i + 1) * c, 2 * vd + 2 * kd:2 * vd + 3 * kd] = (kt[i * c:(i + 1) * c] * el).astype(BF16)


def _gla_inproj(x, layer, mix_layer, g_mix, w_in, w_glr, w_gk2, b_gk, kd, vd, casts):
    n, d = x.shape
    n_out = 2 * vd + 3 * kd
    per_tile = ROW_TILE // GLA_CHUNK
    proj, el, *cast_out = pl.pallas_call(
        functools.partial(_gla_inproj_kernel, kd=kd, vd=vd, hk=kd // GLA_HEADS, n_cast=len(casts)),
        grid=(n // ROW_TILE,),
        in_specs=[pl.BlockSpec((ROW_TILE, d), lambda i: (i, 0)),
                  _layer(g_mix, mix_layer), _resident(w_in.shape), _layer(w_glr, layer),
                  _layer(w_gk2, layer), _layer(b_gk, layer)]
                 + [c.in_spec for c in casts],
        out_specs=[pl.BlockSpec((ROW_TILE, n_out), lambda i: (i, 0)),
                   pl.BlockSpec((per_tile, kd), lambda i: (i, 0))] + [c.out_spec for c in casts],
        out_shape=[jax.ShapeDtypeStruct((n, n_out), BF16),
                   jax.ShapeDtypeStruct((n // GLA_CHUNK, kd), F32)] + [c.out_shape for c in casts],
        compiler_params=_params("parallel"),
        name="gla_inproj",
    )(x, g_mix, w_in, w_glr, w_gk2, b_gk, *[c.stack for c in casts])
    return proj, el, cast_out


def _gla_kernel(v_ref, sg_ref, qt_ref, kt_ref, kd_ref, el_ref, gon_ref, o_ref, st_ref, *, hk, hv):
    @pl.when(pl.program_id(1) == 0)
    def _():
        st_ref[...] = jnp.zeros_like(st_ref)

    c = GLA_CHUNK
    hc = GLA_HEADS * c
    ri = lax.broadcasted_iota(jnp.int32, (hc, hc), 0)
    ci = lax.broadcasted_iota(jnp.int32, (hc, hc), 1)
    causal = (ri >= ci) & (ri // c == ci // c)
    gon = gon_ref[...]
    for i in range(v_ref.shape[0] // c):
        rows = slice(i * c, (i + 1) * c)
        el = el_ref[i:i + 1, :]
        heads_k = [slice(h * hk, (h + 1) * hk) for h in range(GLA_HEADS)]
        heads_v = [slice(h * hv, (h + 1) * hv) for h in range(GLA_HEADS)]
        q_stack = jnp.concatenate([qt_ref[rows, ks] for ks in heads_k], axis=0)
        k_stack = jnp.concatenate([kt_ref[rows, ks] for ks in heads_k], axis=0)
        v_stack = jnp.concatenate([v_ref[rows, vs] for vs in heads_v], axis=0)
        attn = jnp.where(causal, _dot_nt(q_stack, k_stack), 0.0).astype(BF16)
        o_intra = _dot(attn, v_stack)
        for h in range(GLA_HEADS):
            ks, vs = heads_k[h], heads_v[h]
            st = st_ref[h]
            o = o_intra[h * c:(h + 1) * c] + _dot_nt(qt_ref[rows, ks], st.astype(BF16))
            st_ref[h] = st * el[:, ks] + _dot_tn(v_ref[rows, vs], kd_ref[rows, ks])
            o = o * lax.rsqrt(jnp.mean(o * o, axis=-1, keepdims=True) + EPS) * gon
            o_ref[rows, vs] = (o * sg_ref[rows, vs].astype(F32)).astype(BF16)


def _gla_core(proj, el, gon, batch, seq, kd, vd):
    n = proj.shape[0]
    hk = kd // GLA_HEADS
    hv = vd // GLA_HEADS
    proj3 = proj.reshape(batch, seq, proj.shape[1])
    el3 = el.reshape(batch, seq // GLA_CHUNK, kd)
    kblk = 2 * vd // kd
    out = pl.pallas_call(
        functools.partial(_gla_kernel, hk=hk, hv=hv),
        grid=(batch, seq // GLA_ROWS),
        in_specs=[pl.BlockSpec((None, GLA_ROWS, vd), lambda b, t: (b, t, 0)),
                  pl.BlockSpec((None, GLA_ROWS, vd), lambda b, t: (b, t, 1)),
                  pl.BlockSpec((None, GLA_ROWS, kd), lambda b, t: (b, t, kblk)),
                  pl.BlockSpec((None, GLA_ROWS, kd), lambda b, t: (b, t, kblk + 1)),
                  pl.BlockSpec((None, GLA_ROWS, kd), lambda b, t: (b, t, kblk + 2)),
                  pl.BlockSpec((None, GLA_ROWS // GLA_CHUNK, kd), lambda b, t: (b, t, 0)),
                  pl.BlockSpec((1, hv), lambda b, t: (0, 0))],
        out_specs=pl.BlockSpec((None, GLA_ROWS, vd), lambda b, t: (b, t, 0)),
        out_shape=jax.ShapeDtypeStruct((batch, seq, vd), BF16),
        scratch_shapes=[pltpu.VMEM((GLA_HEADS, hv, hk), F32)],
        compiler_params=_params("parallel", "arbitrary"),
        name="gla_core",
    )(proj3, proj3, proj3, proj3, proj3, el3, gon)
    return out.reshape(n, vd)


_POST_TAIL_INPUTS = {"none": 0, "final": 1, "q": 2, "kvq": 5}


def _post_kernel(*refs, tail, n_cast):
    ins, outs = _run_casts(refs, 6 + _POST_TAIL_INPUTS[tail], n_cast)
    h_ref, o_ref, wo_ref, g_ref, wup_ref, wdn_ref = ins[:6]
    rest = ins[6:] + outs
    h1 = h_ref[...] + _dot(o_ref[...], wo_ref[...])
    hn = _rms(h1, g_ref[...]).astype(BF16)
    acc = h1
    d_ff = wup_ref.shape[1]
    for j in range(0, d_ff, COL_CHUNK):
        u = jnp.maximum(_dot(hn, wup_ref[:, j:j + COL_CHUNK]), 0.0)
        acc = acc + _dot((u * u).astype(BF16), wdn_ref[j:j + COL_CHUNK, :])
    if tail == "none":
        (out_ref,) = rest
    elif tail == "final":
        gfin_ref, out_ref = rest
        acc = _rms(acc, gfin_ref[...])
    else:
        if tail == "kvq":
            gq_ref, wq_ref, gkv_ref, wk_ref, wvt_ref, out_ref, q_ref, kp_ref, vt_ref = rest
        else:
            gq_ref, wq_ref, out_ref, q_ref = rest
        xhat = acc * lax.rsqrt(jnp.mean(acc * acc, axis=-1, keepdims=True) + EPS)
        _store_q((xhat * gq_ref[...]).astype(BF16), wq_ref, q_ref)
        if tail == "kvq":
            _store_kv((xhat * gkv_ref[...]).astype(BF16), wk_ref, wvt_ref, kp_ref, vt_ref)
    out_ref[...] = acc


def _post(h, o, params, tail, casts, n_q_cols=0, n_vt_rows=0):
    n, d = h.shape
    assert len(params) == 4 + _POST_TAIL_INPUTS[tail]
    rows = lambda width: pl.BlockSpec((ROW_TILE, width), lambda i: (i, 0))
    out_specs = [rows(d)]
    out_shape = [jax.ShapeDtypeStruct((n, d), F32)]
    if tail in ("q", "kvq"):
        out_specs.append(rows(n_q_cols))
        out_shape.append(jax.ShapeDtypeStruct((n, n_q_cols), BF16))
    if tail == "kvq":
        out_specs += [rows(4 * LANE), pl.BlockSpec((n_vt_rows, ROW_TILE), lambda i: (0, i))]
        out_shape += [jax.ShapeDtypeStruct((n, 4 * LANE), BF16),
                      jax.ShapeDtypeStruct((n_vt_rows, n), BF16)]
    n_main = len(out_specs)
    outs = pl.pallas_call(
        functools.partial(_post_kernel, tail=tail, n_cast=len(casts)),
        grid=(n // ROW_TILE,),
        in_specs=[rows(d), rows(o.shape[1])] + [spec for _, spec in params]
                 + [c.in_spec for c in casts],
        out_specs=out_specs + [c.out_spec for c in casts],
        out_shape=out_shape + [c.out_shape for c in casts],
        compiler_params=_params("parallel"),
        name="post_mlp",
    )(h, o, *[arr for arr, _ in params], *[c.stack for c in casts])
    return outs[:n_main], outs[n_main:]


def _rel_buckets_merged():
    r = np.arange(BLOCK)[:, None]
    q = np.arange(BLOCK)[None, :]
    n = np.where(r > q, q + BLOCK - r, q - r)
    max_exact = REL_BUCKETS // 2
    large = max_exact + (np.log(np.maximum(n, 1) / max_exact)
                         / np.log(REL_MAX_DIST / max_exact)
                         * (REL_BUCKETS - max_exact)).astype(np.int32)
    large = np.minimum(large, REL_BUCKETS - 1)
    return np.where(n < max_exact, n, large).astype(np.int32)


def _swa_kernel(rel_ref, sink_ref, bucket_ref, q_ref, kpp_ref, kpc_ref, vtp_ref, vtc_ref, o_ref,
                bias_ref, *, n_kv):
    n_q = bias_ref.shape[0]
    hd = SWA_HEAD_DIM
    pairs_per_kv = SWA_GROUP // 2
    neg2 = NEG * LOG2E

    @pl.when((pl.program_id(0) == 0) & (pl.program_id(1) == 0))
    def _():
        bucket = bucket_ref[...]

        def per_head(h, carry):
            def per_bucket(b, acc):
                return jnp.where(bucket == b, rel_ref[b, h] * LOG2E, acc)
            bias_ref[h] = lax.fori_loop(0, REL_BUCKETS, per_bucket,
                                        jnp.zeros(bucket.shape, F32))
            return carry

        lax.fori_loop(0, n_q, per_head, 0)

    key_r = lax.broadcasted_iota(jnp.int32, (BLOCK, BLOCK), 0)
    qry = lax.broadcasted_iota(jnp.int32, (BLOCK, BLOCK), 1)
    lower = key_r > qry
    seq_start = pl.program_id(1) == 0
    zeros = jnp.zeros((hd, 2 * BLOCK), BF16)
    for qb in range(SWA_QBLOCKS):
        rows = slice(qb * BLOCK, (qb + 1) * BLOCK)
        prev_rows = slice((qb - 1) * BLOCK, qb * BLOCK)
        for hkv in range(n_kv):
            def k_band(col):
                cols = slice(col * LANE, (col + 1) * LANE)
                prev = kpp_ref[:, cols] if qb == 0 else kpc_ref[prev_rows, cols]
                return [prev, kpc_ref[rows, cols]]
            k_stack = jnp.concatenate(k_band(2 * hkv) + k_band(2 * hkv + 1), axis=0)
            vrow = slice(hkv * hd, (hkv + 1) * hd)
            if qb == 0:
                vt_prev = jnp.where(seq_start, jnp.zeros((hd, BLOCK), BF16), vtp_ref[vrow, :])
                vt_band = jnp.concatenate([vt_prev, vtc_ref[vrow, :BLOCK]], axis=1)
            else:
                vt_band = vtc_ref[vrow, (qb - 1) * BLOCK:(qb + 1) * BLOCK]
            v_bd = jnp.concatenate([jnp.concatenate([vt_band, zeros], axis=1),
                                    jnp.concatenate([zeros, vt_band], axis=1)], axis=0)
            for pr in range(pairs_per_kv):
                pair = hkv * pairs_per_kv + pr
                cols = slice(pair * LANE, (pair + 1) * LANE)
                st = _dot_nt(k_stack, q_ref[rows, cols])
                ps, rden = [], []
                for e in range(2):
                    hq = 2 * pair + e
                    base = e * 2 * BLOCK
                    s = jnp.where(lower, st[base:base + BLOCK], st[base + BLOCK:base + 2 * BLOCK])
                    s = s + bias_ref[hq]
                    if qb == 0:
                        s = jnp.where(seq_start & lower, neg2, s)
                    sink = sink_ref[hq] * LOG2E
                    m = jnp.maximum(jnp.max(s, axis=0, keepdims=True), jnp.maximum(sink, neg2))
                    p = jnp.exp2(s - m)
                    pm = jnp.exp2(neg2 - m)
                    den = jnp.sum(p, axis=0, keepdims=True) + BLOCK * pm + jnp.exp2(sink - m)
                    pm_tile = jnp.broadcast_to(pm, p.shape)
                    ps += [jnp.where(lower, p, pm_tile).astype(BF16),
                           jnp.where(lower, pm_tile, p).astype(BF16)]
                    rden.append(jnp.broadcast_to(1.0 / den, (hd, BLOCK)))
                ot = _dot(v_bd, jnp.concatenate(ps, axis=0)) * jnp.concatenate(rden, axis=0)
                o_ref[rows, cols] = ot.T.astype(BF16)


def _swa_core(q, kpad, vt, rel_table, sinks, batch, seq):
    n, dq = q.shape
    n_q = dq // SWA_HEAD_DIM
    n_kv = n_q // SWA_GROUP
    rows = SWA_QBLOCKS * BLOCK
    steps = seq // rows
    q3 = q.reshape(batch, seq, dq)
    kp3 = kpad.reshape(batch, seq, kpad.shape[1])
    assert WINDOW == BLOCK
    bucket = jnp.asarray(_rel_buckets_merged())
    smem = pl.BlockSpec(memory_space=pltpu.SMEM)
    prev = lambda t: jnp.maximum(SWA_QBLOCKS * t - 1, 0)
    out = pl.pallas_call(
        functools.partial(_swa_kernel, n_kv=n_kv),
        grid=(batch, steps),
        in_specs=[smem, smem,
                  pl.BlockSpec(bucket.shape, lambda b, t: (0, 0)),
                  pl.BlockSpec((None, rows, dq), lambda b, t: (b, t, 0)),
                  pl.BlockSpec((None, BLOCK, kpad.shape[1]), lambda b, t: (b, prev(t), 0)),
                  pl.BlockSpec((None, rows, kpad.shape[1]), lambda b, t: (b, t, 0)),
                  pl.BlockSpec((vt.shape[0], BLOCK),
                               lambda b, t: (0, b * steps * SWA_QBLOCKS + prev(t))),
                  pl.BlockSpec((vt.shape[0], rows), lambda b, t: (0, b * steps + t))],
        out_specs=pl.BlockSpec((None, rows, dq), lambda b, t: (b, t, 0)),
        out_shape=jax.ShapeDtypeStruct((batch, seq, dq), BF16),
        scratch_shapes=[pltpu.VMEM((n_q,) + bucket.shape, F32)],
        compiler_params=_params("arbitrary", "arbitrary"),
        name="swa_core",
    )(rel_table, sinks, bucket, q3, kp3, kp3, vt, vt)
    return out.reshape(n, dq)


def kernel(x, a_w_in, a_w_gk2, a_b_gk, a_onorm, a_w_out, kv_norm, w_kv, b_w_q, b_sinks, b_w_out,
           rel_table, ln_mix, ln_mlp, w_up, w_down, ln_final):
    batch, seq, d = x.shape
    n_a = a_w_in.shape[0]
    n_b = b_w_q.shape[0]
    kd = a_w_gk2.shape[2]
    vd = (a_w_in.shape[2] - GLA_GATE_RANK - 2 * kd) // 2

    depth = n_a + n_b
    assert n_a >= 1
    half = w_kv.shape[1] // 2
    assert half == LANE and 2 * SWA_HEAD_DIM == LANE

    w_in_t = jnp.swapaxes(a_w_in, 1, 2)
    n_main = 2 * kd + 2 * vd

    w_glr = jnp.pad(w_in_t[:, n_main:, :], ((0, 0), (0, LANE - GLA_GATE_RANK), (0, 0))).astype(BF16)
    w_gk2 = jnp.pad(a_w_gk2, ((0, 0), (0, LANE - GLA_GATE_RANK), (0, 0))).astype(BF16)
    w_vt = w_kv[:, half:].T.astype(BF16)
    g_mix = ln_mix.reshape(depth, 1, d)
    g_mlp = ln_mlp.reshape(depth, 1, d)
    g_kv = kv_norm.reshape(1, d)
    g_final = ln_final.reshape(1, d)
    b_gk = a_b_gk.reshape(n_a, 1, kd)

    calls = []
    for layer in range(depth):
        is_gla = layer < n_a
        j = layer if is_gla else layer - n_a
        if is_gla:
            calls.append(("inproj", layer, {"w_in": (w_in_t, j, n_main, d)}))
        needs = {"w_out": (a_w_out if is_gla else b_w_out, j, d, d),
                 "w_up": (w_up, layer, d, w_up.shape[2]), "w_down": (w_down, layer, w_down.shape[1], d)}
        if n_a <= layer + 1 < depth:
            needs["w_q"] = (b_w_q, layer + 1 - n_a, d, b_w_q.shape[2])
        if layer + 1 == n_a:
            needs["w_k"] = (w_kv[None], 0, d, half)
        calls.append(("post", layer, needs))
    steps = batch * seq // ROW_TILE
    bf = {name: stack[idx, :rows, :cols].astype(BF16)
          for name, (stack, idx, rows, cols) in calls[0][2].items()}

    h = x.reshape(batch * seq, d)
    for c, (kind, layer, _) in enumerate(calls):
        is_gla = layer < n_a
        j = layer if is_gla else layer - n_a
        nxt = calls[c + 1][2] if c + 1 < len(calls) else {}
        casts = [_Cast(stack, idx, rows, cols, steps) for stack, idx, rows, cols in nxt.values()]
        if kind == "inproj":
            proj, el, cast_out = _gla_inproj(h, j, layer, g_mix, bf["w_in"], w_glr, w_gk2, b_gk,
                                             kd, vd, casts)
            o = _gla_core(proj, el, a_onorm[j].reshape(1, -1), batch, seq, kd, vd)
        else:
            if not is_gla:
                o = _swa_core(q, kpad, vt, rel_table, b_sinks[j], batch, seq)
            params = [(bf["w_out"], _resident(bf["w_out"].shape)), (g_mlp, _layer(g_mlp, layer)),
                      (bf["w_up"], _resident(bf["w_up"].shape)),
                      (bf["w_down"], _resident(bf["w_down"].shape))]
            if layer == depth - 1:
                (h,), cast_out = _post(h, o, params + [(g_final, _resident(g_final.shape))],
                                       "final", casts)
            elif layer + 1 < n_a:
                (h,), cast_out = _post(h, o, params, "none", casts)
            else:
                params += [(g_mix, _layer(g_mix, layer + 1)), (bf["w_q"], _resident(bf["w_q"].shape))]
                if layer + 1 == n_a:
                    params += [(g_kv, _resident(g_kv.shape)), (bf["w_k"], _resident(bf["w_k"].shape)),
                               (w_vt, _resident(w_vt.shape))]
                    (h, q, kpad, vt), cast_out = _post(h, o, params, "kvq", casts,
                                                       bf["w_q"].shape[1], w_vt.shape[0])
                else:
                    (h, q), cast_out = _post(h, o, params, "q", casts, bf["w_q"].shape[1])
        bf = dict(zip(nxt.keys(), cast_out))
    return h.reshape(batch, seq, d)
```

```python
import functools
import math

import numpy as np
import jax
import jax.numpy as jnp
from jax import lax
from jax.experimental import pallas as pl
from jax.experimental.pallas import tpu as pltpu

F32 = jnp.float32
BF16 = jnp.bfloat16

EPS = 1e-6
NEG = -1e30
LOG2E = math.log2(math.e)

GLA_HEADS = 4
GLA_GATE_RANK = 16
GLA_GATE_NORM = 16.0
GLA_CHUNK = 64

SWA_HEAD_DIM = 64
SWA_GROUP = 8
WINDOW = 128
BLOCK = 128
REL_BUCKETS = 32
REL_MAX_DIST = 128

LANE = 128
SUBLANES = 8
ROW_TILE = 512
COL_CHUNK = 512
GLA_ROWS = 1024
SWA_QBLOCKS = 8
VMEM_LIMIT = 56 * 1024 * 1024

_NT = (((1,), (1,)), ((), ()))
_TN = (((0,), (0,)), ((), ()))


def _rms(x, g):
    ms = jnp.mean(x * x, axis=-1, keepdims=True)
    return x * lax.rsqrt(ms + EPS) * g


def _dot(a, b):
    return jnp.dot(a, b, preferred_element_type=F32)


def _dot_nt(a, b):
    return lax.dot_general(a, b, _NT, preferred_element_type=F32)


def _dot_tn(a, b):
    return lax.dot_general(a, b, _TN, preferred_element_type=F32)


def _params(*sem):
    return pltpu.CompilerParams(dimension_semantics=sem, vmem_limit_bytes=VMEM_LIMIT)


def _resident(shape):
    zeros = (0,) * len(shape)
    return pl.BlockSpec(shape, lambda *_: zeros, pipeline_mode=pl.Buffered(1))


def _layer(stacked, layer):
    idx = (layer,) + (0,) * (stacked.ndim - 1)
    return pl.BlockSpec((None,) + stacked.shape[1:], lambda *_: idx, pipeline_mode=pl.Buffered(1))


class _Cast:
    def __init__(self, stack, layer, rows, cols, steps):
        assert rows % steps == 0 and cols % LANE == 0
        block = (rows // steps, cols)
        self.stack = stack
        self.in_spec = pl.BlockSpec((None,) + block, lambda i: (layer, i, 0))
        self.out_spec = pl.BlockSpec(block, lambda i: (i, 0))
        self.out_shape = jax.ShapeDtypeStruct((rows, cols), BF16)


def _run_casts(refs, n_in, n_cast):
    n_out = len(refs) - n_in - 2 * n_cast
    for src, dst in zip(refs[n_in:n_in + n_cast], refs[n_in + n_cast + n_out:]):
        dst[...] = src[...].astype(BF16)
    return refs[:n_in], refs[n_in + n_cast:n_in + n_cast + n_out]


def _store_q(xn, wq_ref, q_ref):
    scale = SWA_HEAD_DIM ** -0.5 * LOG2E
    for j in range(0, q_ref.shape[-1], COL_CHUNK):
        q_ref[:, j:j + COL_CHUNK] = (_dot(xn, wq_ref[:, j:j + COL_CHUNK]) * scale).astype(BF16)


def _store_kv(xn, wk_ref, wvt_ref, kp_ref, vt_ref):
    k = _dot(xn, wk_ref[...])
    kr = pltpu.roll(k, SWA_HEAD_DIM, axis=1)
    low = lax.broadcasted_iota(jnp.int32, k.shape, 1) < SWA_HEAD_DIM
    pieces = (jnp.where(low, k, 0.0), jnp.where(low, 0.0, kr),
              jnp.where(low, kr, 0.0), jnp.where(low, 0.0, k))
    for i, piece in enumerate(pieces):
        kp_ref[:, i * LANE:(i + 1) * LANE] = piece.astype(BF16)
    vt_ref[...] = _dot_nt(wvt_ref[...], xn).astype(BF16)


def _gla_inproj_kernel(*refs, kd, vd, hk, n_cast):
    *refs, w_ref = refs
    (x_ref, g_ref, w32_ref, wglr_ref, wgk2_ref, bgk_ref), (o_ref, el_ref) = _run_casts(refs, 6, n_cast)

    @pl.when(pl.program_id(0) == 0)
    def _():
        w_ref[...] = w32_ref[:w_ref.shape[0], :].astype(BF16)

    tm = x_ref.shape[0]
    c = GLA_CHUNK
    xn = _rms(x_ref[...], g_ref[...]).astype(BF16)
    glr = _dot_nt(xn, wglr_ref[...])
    z = _dot(glr.astype(BF16), wgk2_ref[...]) + bgk_ref[...]
    for j in range(0, vd, COL_CHUNK):
        o_ref[:, j:j + COL_CHUNK] = _dot_nt(
            xn, w_ref[2 * kd + j:2 * kd + j + COL_CHUNK, :]).astype(BF16)
    for j in range(vd, 2 * vd, COL_CHUNK):
        gt = _dot_nt(xn, w_ref[2 * kd + j:2 * kd + j + COL_CHUNK, :])
        o_ref[:, j:j + COL_CHUNK] = (gt / (1.0 + jnp.exp(-gt))).astype(BF16)
    gk = (jnp.minimum(z, 0.0) - jnp.log(1.0 + jnp.exp(-jnp.abs(z)))) * (LOG2E / GLA_GATE_NORM)
    groups = c // SUBLANES
    g4 = gk.reshape(tm // c, groups, SUBLANES, kd)
    sub = lax.broadcasted_iota(jnp.int32, g4.shape, 2)
    shift = 1
    while shift < SUBLANES:
        g4 = g4 + jnp.where(sub >= shift, pltpu.roll(g4, shift, axis=2), 0.0)
        shift *= 2
    pieces, carry = [], None
    for g in range(groups):
        piece = g4[:, g] if carry is None else g4[:, g] + carry
        pieces.append(piece)
        carry = piece[:, SUBLANES - 1:, :]
    bcum = jnp.stack(pieces, axis=1).reshape(tm, kd)
    q = _dot_nt(xn, w_ref[:kd, :])
    o_ref[:, 2 * vd:2 * vd + kd] = ((q * (hk ** -0.5)) * jnp.exp2(bcum)).astype(BF16)
    k = _dot_nt(xn, w_ref[kd:2 * kd, :])
    kt = k * jnp.exp2(-bcum)
    o_ref[:, 2 * vd + kd:2 * vd + 2 * kd] = kt.astype(BF16)
    for i in range(tm // c):
        el = jnp.exp2(bcum[(i + 1) * c - 1:(i + 1) * c, :])
        el_ref[i:i + 1, :] = el
        o_ref[i * c:(i + 1) * c, 2 * vd + 2 * kd:2 * vd + 3 * kd] = (kt[i * c:(i + 1) * c] * el).astype(BF16)


def _gla_inproj(x, layer, mix_layer, g_mix, w_in_t, w_glr, w_gk2, b_gk, kd, vd, casts):
    n, d = x.shape
    n_out = 2 * vd + 3 * kd
    per_tile = ROW_TILE // GLA_CHUNK
    proj, el, *cast_out = pl.pallas_call(
        functools.partial(_gla_inproj_kernel, kd=kd, vd=vd, hk=kd // GLA_HEADS, n_cast=len(casts)),
        grid=(n // ROW_TILE,),
        in_specs=[pl.BlockSpec((ROW_TILE, d), lambda i: (i, 0)),
                  _layer(g_mix, mix_layer), _layer(w_in_t, layer), _layer(w_glr, layer),
                  _layer(w_gk2, layer), _layer(b_gk, layer)]
                 + [c.in_spec for c in casts],
        out_specs=[pl.BlockSpec((ROW_TILE, n_out), lambda i: (i, 0)),
                   pl.BlockSpec((per_tile, kd), lambda i: (i, 0))] + [c.out_spec for c in casts],
        out_shape=[jax.ShapeDtypeStruct((n, n_out), BF16),
                   jax.ShapeDtypeStruct((n // GLA_CHUNK, kd), F32)] + [c.out_shape for c in casts],
        scratch_shapes=[pltpu.VMEM((2 * kd + 2 * vd, d), BF16)],
        compiler_params=_params("arbitrary"),
        name="gla_inproj",
    )(x, g_mix, w_in_t, w_glr, w_gk2, b_gk, *[c.stack for c in casts])
    return proj, el, cast_out


def _gla_kernel(v_ref, sg_ref, qt_ref, kt_ref, kd_ref, el_ref, gon_ref, o_ref, st_ref, *, hk, hv):
    @pl.when(pl.program_id(1) == 0)
    def _():
        st_ref[...] = jnp.zeros_like(st_ref)

    c = GLA_CHUNK
    hc = GLA_HEADS * c
    ri = lax.broadcasted_iota(jnp.int32, (hc, hc), 0)
    ci = lax.broadcasted_iota(jnp.int32, (hc, hc), 1)
    causal = (ri >= ci) & (ri // c == ci // c)
    gon = gon_ref[...]
    for i in range(v_ref.shape[0] // c):
        rows = slice(i * c, (i + 1) * c)
        el = el_ref[i:i + 1, :]
        heads_k = [slice(h * hk, (h + 1) * hk) for h in range(GLA_HEADS)]
        heads_v = [slice(h * hv, (h + 1) * hv) for h in range(GLA_HEADS)]
        q_stack = jnp.concatenate([qt_ref[rows, ks] for ks in heads_k], axis=0)
        k_stack = jnp.concatenate([kt_ref[rows, ks] for ks in heads_k], axis=0)
        v_stack = jnp.concatenate([v_ref[rows, vs] for vs in heads_v], axis=0)
        attn = jnp.where(causal, _dot_nt(q_stack, k_stack), 0.0).astype(BF16)
        o_intra = _dot(attn, v_stack)
        for h in range(GLA_HEADS):
            ks, vs = heads_k[h], heads_v[h]
            st = st_ref[h]
            o = o_intra[h * c:(h + 1) * c] + _dot_nt(qt_ref[rows, ks], st.astype(BF16))
            st_ref[h] = st * el[:, ks] + _dot_tn(v_ref[rows, vs], kd_ref[rows, ks])
            o = o * lax.rsqrt(jnp.mean(o * o, axis=-1, keepdims=True) + EPS) * gon
            o_ref[rows, vs] = (o * sg_ref[rows, vs].astype(F32)).astype(BF16)


def _gla_core(proj, el, gon, batch, seq, kd, vd):
    n = proj.shape[0]
    hk = kd // GLA_HEADS
    hv = vd // GLA_HEADS
    proj3 = proj.reshape(batch, seq, proj.shape[1])
    el3 = el.reshape(batch, seq // GLA_CHUNK, kd)
    kblk = 2 * vd // kd
    out = pl.pallas_call(
        functools.partial(_gla_kernel, hk=hk, hv=hv),
        grid=(batch, seq // GLA_ROWS),
        in_specs=[pl.BlockSpec((None, GLA_ROWS, vd), lambda b, t: (b, t, 0)),
                  pl.BlockSpec((None, GLA_ROWS, vd), lambda b, t: (b, t, 1)),
                  pl.BlockSpec((None, GLA_ROWS, kd), lambda b, t: (b, t, kblk)),
                  pl.BlockSpec((None, GLA_ROWS, kd), lambda b, t: (b, t, kblk + 1)),
                  pl.BlockSpec((None, GLA_ROWS, kd), lambda b, t: (b, t, kblk + 2)),
                  pl.BlockSpec((None, GLA_ROWS // GLA_CHUNK, kd), lambda b, t: (b, t, 0)),
                  pl.BlockSpec((1, hv), lambda b, t: (0, 0))],
        out_specs=pl.BlockSpec((None, GLA_ROWS, vd), lambda b, t: (b, t, 0)),
        out_shape=jax.ShapeDtypeStruct((batch, seq, vd), BF16),
        scratch_shapes=[pltpu.VMEM((GLA_HEADS, hv, hk), F32)],
        compiler_params=_params("parallel", "arbitrary"),
        name="gla_core",
    )(proj3, proj3, proj3, proj3, proj3, el3, gon)
    return out.reshape(n, vd)


_POST_TAIL_INPUTS = {"none": 0, "final": 1, "q": 2, "kvq": 5}


def _post_kernel(*refs, tail, n_cast):
    ins, outs = _run_casts(refs, 6 + _POST_TAIL_INPUTS[tail], n_cast)
    h_ref, o_ref, wo_ref, g_ref, wup_ref, wdn_ref = ins[:6]
    rest = ins[6:] + outs
    h1 = h_ref[...] + _dot(o_ref[...], wo_ref[...])
    hn = _rms(h1, g_ref[...]).astype(BF16)
    acc = h1
    d_ff = wup_ref.shape[1]
    for j in range(0, d_ff, COL_CHUNK):
        u = jnp.maximum(_dot(hn, wup_ref[:, j:j + COL_CHUNK]), 0.0)
        acc = acc + _dot((u * u).astype(BF16), wdn_ref[j:j + COL_CHUNK, :])
    if tail == "none":
        (out_ref,) = rest
    elif tail == "final":
        gfin_ref, out_ref = rest
        acc = _rms(acc, gfin_ref[...])
    else:
        if tail == "kvq":
            gq_ref, wq_ref, gkv_ref, wk_ref, wvt_ref, out_ref, q_ref, kp_ref, vt_ref = rest
        else:
            gq_ref, wq_ref, out_ref, q_ref = rest
        xhat = acc * lax.rsqrt(jnp.mean(acc * acc, axis=-1, keepdims=True) + EPS)
        _store_q((xhat * gq_ref[...]).astype(BF16), wq_ref, q_ref)
        if tail == "kvq":
            _store_kv((xhat * gkv_ref[...]).astype(BF16), wk_ref, wvt_ref, kp_ref, vt_ref)
    out_ref[...] = acc


def _post(h, o, params, tail, casts, n_q_cols=0, n_vt_rows=0):
    n, d = h.shape
    assert len(params) == 4 + _POST_TAIL_INPUTS[tail]
    rows = lambda width: pl.BlockSpec((ROW_TILE, width), lambda i: (i, 0))
    out_specs = [rows(d)]
    out_shape = [jax.ShapeDtypeStruct((n, d), F32)]
    if tail in ("q", "kvq"):
        out_specs.append(rows(n_q_cols))
        out_shape.append(jax.ShapeDtypeStruct((n, n_q_cols), BF16))
    if tail == "kvq":
        out_specs += [rows(4 * LANE), pl.BlockSpec((n_vt_rows, ROW_TILE), lambda i: (0, i))]
        out_shape += [jax.ShapeDtypeStruct((n, 4 * LANE), BF16),
                      jax.ShapeDtypeStruct((n_vt_rows, n), BF16)]
    n_main = len(out_specs)
    outs = pl.pallas_call(
        functools.partial(_post_kernel, tail=tail, n_cast=len(casts)),
        grid=(n // ROW_TILE,),
        in_specs=[rows(d), rows(o.shape[1])] + [spec for _, spec in params]
                 + [c.in_spec for c in casts],
        out_specs=out_specs + [c.out_spec for c in casts],
        out_shape=out_shape + [c.out_shape for c in casts],
        compiler_params=_params("parallel"),
        name="post_mlp",
    )(h, o, *[arr for arr, _ in params], *[c.stack for c in casts])
    return outs[:n_main], outs[n_main:]


def _rel_buckets_merged():
    r = np.arange(BLOCK)[:, None]
    q = np.arange(BLOCK)[None, :]
    n = np.where(r > q, q + BLOCK - r, q - r)
    max_exact = REL_BUCKETS // 2
    large = max_exact + (np.log(np.maximum(n, 1) / max_exact)
                         / np.log(REL_MAX_DIST / max_exact)
                         * (REL_BUCKETS - max_exact)).astype(np.int32)
    large = np.minimum(large, REL_BUCKETS - 1)
    return np.where(n < max_exact, n, large).astype(np.int32)


def _swa_kernel(rel_ref, sink_ref, bucket_ref, q_ref, kpp_ref, kpc_ref, vtp_ref, vtc_ref, o_ref,
                bias_ref, *, n_kv):
    n_q = bias_ref.shape[0]
    hd = SWA_HEAD_DIM
    pairs_per_kv = SWA_GROUP // 2
    neg2 = NEG * LOG2E

    @pl.when((pl.program_id(0) == 0) & (pl.program_id(1) == 0))
    def _():
        bucket = bucket_ref[...]

        def per_head(h, carry):
            def per_bucket(b, acc):
                return jnp.where(bucket == b, rel_ref[b, h] * LOG2E, acc)
            bias_ref[h] = lax.fori_loop(0, REL_BUCKETS, per_bucket,
                                        jnp.zeros(bucket.shape, F32))
            return carry

        lax.fori_loop(0, n_q, per_head, 0)

    key_r = lax.broadcasted_iota(jnp.int32, (BLOCK, BLOCK), 0)
    qry = lax.broadcasted_iota(jnp.int32, (BLOCK, BLOCK), 1)
    lower = key_r > qry
    seq_start = pl.program_id(1) == 0
    zeros = jnp.zeros((hd, 2 * BLOCK), BF16)
    for qb in range(SWA_QBLOCKS):
        rows = slice(qb * BLOCK, (qb + 1) * BLOCK)
        prev_rows = slice((qb - 1) * BLOCK, qb * BLOCK)
        for hkv in range(n_kv):
            def k_band(col):
                cols = slice(col * LANE, (col + 1) * LANE)
                prev = kpp_ref[:, cols] if qb == 0 else kpc_ref[prev_rows, cols]
                return [prev, kpc_ref[rows, cols]]
            k_stack = jnp.concatenate(k_band(2 * hkv) + k_band(2 * hkv + 1), axis=0)
            vrow = slice(hkv * hd, (hkv + 1) * hd)
            if qb == 0:
                vt_prev = jnp.where(seq_start, jnp.zeros((hd, BLOCK), BF16), vtp_ref[vrow, :])
                vt_band = jnp.concatenate([vt_prev, vtc_ref[vrow, :BLOCK]], axis=1)
            else:
                vt_band = vtc_ref[vrow, (qb - 1) * BLOCK:(qb + 1) * BLOCK]
            v_bd = jnp.concatenate([jnp.concatenate([vt_band, zeros], axis=1),
                                    jnp.concatenate([zeros, vt_band], axis=1)], axis=0)
            for pr in range(pairs_per_kv):
                pair = hkv * pairs_per_kv + pr
                cols = slice(pair * LANE, (pair + 1) * LANE)
                st = _dot_nt(k_stack, q_ref[rows, cols])
                ps, rden = [], []
                for e in range(2):
                    hq = 2 * pair + e
                    base = e * 2 * BLOCK
                    s = jnp.where(lower, st[base:base + BLOCK], st[base + BLOCK:base + 2 * BLOCK])
                    s = s + bias_ref[hq]
                    if qb == 0:
                        s = jnp.where(seq_start & lower, neg2, s)
                    sink = sink_ref[hq] * LOG2E
                    m = jnp.maximum(jnp.max(s, axis=0, keepdims=True), jnp.maximum(sink, neg2))
                    p = jnp.exp2(s - m)
                    pm = jnp.exp2(neg2 - m)
                    den = jnp.sum(p, axis=0, keepdims=True) + BLOCK * pm + jnp.exp2(sink - m)
                    pm_tile = jnp.broadcast_to(pm, p.shape)
                    ps += [jnp.where(lower, p, pm_tile).astype(BF16),
                           jnp.where(lower, pm_tile, p).astype(BF16)]
                    rden.append(jnp.broadcast_to(1.0 / den, (hd, BLOCK)))
                ot = _dot(v_bd, jnp.concatenate(ps, axis=0)) * jnp.concatenate(rden, axis=0)
                o_ref[rows, cols] = ot.T.astype(BF16)


def _swa_core(q, kpad, vt, rel_table, sinks, batch, seq):
    n, dq = q.shape
    n_q = dq // SWA_HEAD_DIM
    n_kv = n_q // SWA_GROUP
    rows = SWA_QBLOCKS * BLOCK
    steps = seq // rows
    q3 = q.reshape(batch, seq, dq)
    kp3 = kpad.reshape(batch, seq, kpad.shape[1])
    assert WINDOW == BLOCK
    bucket = jnp.asarray(_rel_buckets_merged())
    smem = pl.BlockSpec(memory_space=pltpu.SMEM)
    prev = lambda t: jnp.maximum(SWA_QBLOCKS * t - 1, 0)
    out = pl.pallas_call(
        functools.partial(_swa_kernel, n_kv=n_kv),
        grid=(batch, steps),
        in_specs=[smem, smem,
                  pl.BlockSpec(bucket.shape, lambda b, t: (0, 0)),
                  pl.BlockSpec((None, rows, dq), lambda b, t: (b, t, 0)),
                  pl.BlockSpec((None, BLOCK, kpad.shape[1]), lambda b, t: (b, prev(t), 0)),
                  pl.BlockSpec((None, rows, kpad.shape[1]), lambda b, t: (b, t, 0)),
                  pl.BlockSpec((vt.shape[0], BLOCK),
                               lambda b, t: (0, b * steps * SWA_QBLOCKS + prev(t))),
                  pl.BlockSpec((vt.shape[0], rows), lambda b, t: (0, b * steps + t))],
        out_specs=pl.BlockSpec((None, rows, dq), lambda b, t: (b, t, 0)),
        out_shape=jax.ShapeDtypeStruct((batch, seq, dq), BF16),
        scratch_shapes=[pltpu.VMEM((n_q,) + bucket.shape, F32)],
        compiler_params=_params("arbitrary", "arbitrary"),
        name="swa_core",
    )(rel_table, sinks, bucket, q3, kp3, kp3, vt, vt)
    return out.reshape(n, dq)


def kernel(x, a_w_in, a_w_gk2, a_b_gk, a_onorm, a_w_out, kv_norm, w_kv, b_w_q, b_sinks, b_w_out,
           rel_table, ln_mix, ln_mlp, w_up, w_down, ln_final):
    batch, seq, d = x.shape
    n_a = a_w_in.shape[0]
    n_b = b_w_q.shape[0]
    kd = a_w_gk2.shape[2]
    vd = (a_w_in.shape[2] - GLA_GATE_RANK - 2 * kd) // 2

    depth = n_a + n_b
    assert n_a >= 1
    half = w_kv.shape[1] // 2
    assert half == LANE and 2 * SWA_HEAD_DIM == LANE

    w_in_t = jnp.swapaxes(a_w_in, 1, 2)
    n_main = 2 * kd + 2 * vd

    w_glr = jnp.pad(w_in_t[:, n_main:, :], ((0, 0), (0, LANE - GLA_GATE_RANK), (0, 0))).astype(BF16)
    w_gk2 = jnp.pad(a_w_gk2, ((0, 0), (0, LANE - GLA_GATE_RANK), (0, 0))).astype(BF16)
    w_vt = w_kv[:, half:].T.astype(BF16)
    g_mix = ln_mix.reshape(depth, 1, d)
    g_mlp = ln_mlp.reshape(depth, 1, d)
    g_kv = kv_norm.reshape(1, d)
    g_final = ln_final.reshape(1, d)
    b_gk = a_b_gk.reshape(n_a, 1, kd)

    calls = []
    for layer in range(depth):
        is_gla = layer < n_a
        j = layer if is_gla else layer - n_a
        if is_gla:
            calls.append(("inproj", layer, {}))
        needs = {"w_out": (a_w_out if is_gla else b_w_out, j, d, d),
                 "w_up": (w_up, layer, d, w_up.shape[2]), "w_down": (w_down, layer, w_down.shape[1], d)}
        if n_a <= layer + 1 < depth:
            needs["w_q"] = (b_w_q, layer + 1 - n_a, d, b_w_q.shape[2])
        if layer + 1 == n_a:
            needs["w_k"] = (w_kv[None], 0, d, half)
        calls.append(("post", layer, needs))
    steps = batch * seq // ROW_TILE
    bf = {name: stack[idx, :rows, :cols].astype(BF16)
          for name, (stack, idx, rows, cols) in calls[0][2].items()}

    h = x.reshape(batch * seq, d)
    for c, (kind, layer, _) in enumerate(calls):
        is_gla = layer < n_a
        j = layer if is_gla else layer - n_a
        nxt = calls[c + 1][2] if c + 1 < len(calls) else {}
        casts = [_Cast(stack, idx, rows, cols, steps) for stack, idx, rows, cols in nxt.values()]
        if kind == "inproj":
            proj, el, cast_out = _gla_inproj(h, j, layer, g_mix, w_in_t, w_glr, w_gk2, b_gk,
                                             kd, vd, casts)
            o = _gla_core(proj, el, a_onorm[j].reshape(1, -1), batch, seq, kd, vd)
        else:
            if not is_gla:
                o = _swa_core(q, kpad, vt, rel_table, b_sinks[j], batch, seq)
            params = [(bf["w_out"], _resident(bf["w_out"].shape)), (g_mlp, _layer(g_mlp, layer)),
                      (bf["w_up"], _resident(bf["w_up"].shape)),
                      (bf["w_down"], _resident(bf["w_down"].shape))]
            if layer == depth - 1:
                (h,), cast_out = _post(h, o, params + [(g_final, _resident(g_final.shape))],
                                       "final", casts)
            elif layer + 1 < n_a:
                (h,), cast_out = _post(h, o, params, "none", casts)
            else:
                params += [(g_mix, _layer(g_mix, layer + 1)), (bf["w_q"], _resident(bf["w_q"].shape))]
                if layer + 1 == n_a:
                    params += [(g_kv, _resident(g_kv.shape)), (bf["w_k"], _resident(bf["w_k"].shape)),
                               (w_vt, _resident(w_vt.shape))]
                    (h, q, kpad, vt), cast_out = _post(h, o, params, "kvq", casts,
                                                       bf["w_q"].shape[1], w_vt.shape[0])
                else:
                    (h, q), cast_out = _post(h, o, params, "q", casts, bf["w_q"].shape[1])
        bf = dict(zip(nxt.keys(), cast_out))
    return h.reshape(batch, seq, d)
```

```python
import functools
import math

import numpy as np
import jax
import jax.numpy as jnp
from jax import lax
from jax.experimental import pallas as pl
from jax.experimental.pallas import tpu as pltpu

F32 = jnp.float32
BF16 = jnp.bfloat16

EPS = 1e-6
NEG = -1e30
LOG2E = math.log2(math.e)

GLA_HEADS = 4
GLA_GATE_RANK = 16
GLA_GATE_NORM = 16.0
GLA_CHUNK = 64

SWA_HEAD_DIM = 64
SWA_GROUP = 8
WINDOW = 128
BLOCK = 128
REL_BUCKETS = 32
REL_MAX_DIST = 128

LANE = 128
SUBLANES = 8
ROW_TILE = 512
COL_CHUNK = 512
GLA_ROWS = 2048
SWA_QBLOCKS = 16
VMEM_LIMIT = 56 * 1024 * 1024

_NT = (((1,), (1,)), ((), ()))
_TN = (((0,), (0,)), ((), ()))


def _rms(x, g):
    ms = jnp.mean(x * x, axis=-1, keepdims=True)
    return x * lax.rsqrt(ms + EPS) * g


def _dot(a, b):
    return jnp.dot(a, b, preferred_element_type=F32)


def _dot_nt(a, b):
    return lax.dot_general(a, b, _NT, preferred_element_type=F32)


def _dot_tn(a, b):
    return lax.dot_general(a, b, _TN, preferred_element_type=F32)


def _params(*sem):
    return pltpu.CompilerParams(dimension_semantics=sem, vmem_limit_bytes=VMEM_LIMIT)


def _resident(shape):
    zeros = (0,) * len(shape)
    return pl.BlockSpec(shape, lambda *_: zeros, pipeline_mode=pl.Buffered(1))


def _layer(stacked, layer):
    idx = (layer,) + (0,) * (stacked.ndim - 1)
    return pl.BlockSpec((None,) + stacked.shape[1:], lambda *_: idx, pipeline_mode=pl.Buffered(1))


class _Cast:
    def __init__(self, stack, layer, rows, cols, steps):
        assert rows % steps == 0 and cols % LANE == 0
        block = (rows // steps, cols)
        self.stack = stack
        self.in_spec = pl.BlockSpec((None,) + block, lambda i: (layer, i, 0))
        self.out_spec = pl.BlockSpec(block, lambda i: (i, 0))
        self.out_shape = jax.ShapeDtypeStruct((rows, cols), BF16)


def _run_casts(refs, n_in, n_cast):
    n_out = len(refs) - n_in - 2 * n_cast
    for src, dst in zip(refs[n_in:n_in + n_cast], refs[n_in + n_cast + n_out:]):
        dst[...] = src[...].astype(BF16)
    return refs[:n_in], refs[n_in + n_cast:n_in + n_cast + n_out]


def _store_q(xn, wq_ref, q_ref):
    scale = SWA_HEAD_DIM ** -0.5 * LOG2E
    for j in range(0, q_ref.shape[-1], COL_CHUNK):
        q_ref[:, j:j + COL_CHUNK] = (_dot(xn, wq_ref[:, j:j + COL_CHUNK]) * scale).astype(BF16)


def _store_kv(xn, wk_ref, wvt_ref, kp_ref, vt_ref):
    k = _dot(xn, wk_ref[...])
    kr = pltpu.roll(k, SWA_HEAD_DIM, axis=1)
    low = lax.broadcasted_iota(jnp.int32, k.shape, 1) < SWA_HEAD_DIM
    pieces = (jnp.where(low, k, 0.0), jnp.where(low, 0.0, kr),
              jnp.where(low, kr, 0.0), jnp.where(low, 0.0, k))
    for i, piece in enumerate(pieces):
        kp_ref[:, i * LANE:(i + 1) * LANE] = piece.astype(BF16)
    vt_ref[...] = _dot_nt(wvt_ref[...], xn).astype(BF16)


def _gla_inproj_kernel(*refs, kd, vd, hk, n_cast):
    *refs, w_ref = refs
    (x_ref, g_ref, w32_ref, wglr_ref, wgk2_ref, bgk_ref), (o_ref, el_ref) = _run_casts(refs, 6, n_cast)

    @pl.when(pl.program_id(0) == 0)
    def _():
        w_ref[...] = w32_ref[:w_ref.shape[0], :].astype(BF16)

    tm = x_ref.shape[0]
    c = GLA_CHUNK
    xn = _rms(x_ref[...], g_ref[...]).astype(BF16)
    glr = _dot_nt(xn, wglr_ref[...])
    z = _dot(glr.astype(BF16), wgk2_ref[...]) + bgk_ref[...]
    for j in range(0, vd, COL_CHUNK):
        o_ref[:, j:j + COL_CHUNK] = _dot_nt(
            xn, w_ref[2 * kd + j:2 * kd + j + COL_CHUNK, :]).astype(BF16)
    for j in range(vd, 2 * vd, COL_CHUNK):
        gt = _dot_nt(xn, w_ref[2 * kd + j:2 * kd + j + COL_CHUNK, :])
        o_ref[:, j:j + COL_CHUNK] = (gt / (1.0 + jnp.exp(-gt))).astype(BF16)
    gk = (jnp.minimum(z, 0.0) - jnp.log(1.0 + jnp.exp(-jnp.abs(z)))) * (LOG2E / GLA_GATE_NORM)
    groups = c // SUBLANES
    g4 = gk.reshape(tm // c, groups, SUBLANES, kd)
    sub = lax.broadcasted_iota(jnp.int32, g4.shape, 2)
    shift = 1
    while shift < SUBLANES:
        g4 = g4 + jnp.where(sub >= shift, pltpu.roll(g4, shift, axis=2), 0.0)
        shift *= 2
    pieces, carry = [], None
    for g in range(groups):
        piece = g4[:, g] if carry is None else g4[:, g] + carry
        pieces.append(piece)
        carry = piece[:, SUBLANES - 1:, :]
    bcum = jnp.stack(pieces, axis=1).reshape(tm, kd)
    q = _dot_nt(xn, w_ref[:kd, :])
    o_ref[:, 2 * vd:2 * vd + kd] = ((q * (hk ** -0.5)) * jnp.exp2(bcum)).astype(BF16)
    k = _dot_nt(xn, w_ref[kd:2 * kd, :])
    kt = k * jnp.exp2(-bcum)
    o_ref[:, 2 * vd + kd:2 * vd + 2 * kd] = kt.astype(BF16)
    for i in range(tm // c):
        el = jnp.exp2(bcum[(i + 1) * c - 1:(i + 1) * c, :])
        el_ref[i:i + 1, :] = el
        o_ref[i * c:(i + 1) * c, 2 * vd + 2 * kd:2 * vd + 3 * kd] = (kt[i * c:(i + 1) * c] * el).astype(BF16)


def _gla_inproj(x, layer, mix_layer, g_mix, w_in_t, w_glr, w_gk2, b_gk, kd, vd, casts):
    n, d = x.shape
    n_out = 2 * vd + 3 * kd
    per_tile = ROW_TILE // GLA_CHUNK
    proj, el, *cast_out = pl.pallas_call(
        functools.partial(_gla_inproj_kernel, kd=kd, vd=vd, hk=kd // GLA_HEADS, n_cast=len(casts)),
        grid=(n // ROW_TILE,),
        in_specs=[pl.BlockSpec((ROW_TILE, d), lambda i: (i, 0)),
                  _layer(g_mix, mix_layer), _layer(w_in_t, layer), _layer(w_glr, layer),
                  _layer(w_gk2, layer), _layer(b_gk, layer)]
                 + [c.in_spec for c in casts],
        out_specs=[pl.BlockSpec((ROW_TILE, n_out), lambda i: (i, 0)),
                   pl.BlockSpec((per_tile, kd), lambda i: (i, 0))] + [c.out_spec for c in casts],
        out_shape=[jax.ShapeDtypeStruct((n, n_out), BF16),
                   jax.ShapeDtypeStruct((n // GLA_CHUNK, kd), F32)] + [c.out_shape for c in casts],
        scratch_shapes=[pltpu.VMEM((2 * kd + 2 * vd, d), BF16)],
        compiler_params=_params("arbitrary"),
        name="gla_inproj",
    )(x, g_mix, w_in_t, w_glr, w_gk2, b_gk, *[c.stack for c in casts])
    return proj, el, cast_out


def _gla_kernel(v_ref, sg_ref, qt_ref, kt_ref, kd_ref, el_ref, gon_ref, o_ref, st_ref, *, hk, hv):
    @pl.when(pl.program_id(1) == 0)
    def _():
        st_ref[...] = jnp.zeros_like(st_ref)

    c = GLA_CHUNK
    hc = GLA_HEADS * c
    ri = lax.broadcasted_iota(jnp.int32, (hc, hc), 0)
    ci = lax.broadcasted_iota(jnp.int32, (hc, hc), 1)
    causal = (ri >= ci) & (ri // c == ci // c)
    gon = gon_ref[...]
    for i in range(v_ref.shape[0] // c):
        rows = slice(i * c, (i + 1) * c)
        el = el_ref[i:i + 1, :]
        heads_k = [slice(h * hk, (h + 1) * hk) for h in range(GLA_HEADS)]
        heads_v = [slice(h * hv, (h + 1) * hv) for h in range(GLA_HEADS)]
        q_stack = jnp.concatenate([qt_ref[rows, ks] for ks in heads_k], axis=0)
        k_stack = jnp.concatenate([kt_ref[rows, ks] for ks in heads_k], axis=0)
        v_stack = jnp.concatenate([v_ref[rows, vs] for vs in heads_v], axis=0)
        attn = jnp.where(causal, _dot_nt(q_stack, k_stack), 0.0).astype(BF16)
        o_intra = _dot(attn, v_stack)
        for h in range(GLA_HEADS):
            ks, vs = heads_k[h], heads_v[h]
            st = st_ref[h]
            o = o_intra[h * c:(h + 1) * c] + _dot_nt(qt_ref[rows, ks], st.astype(BF16))
            st_ref[h] = st * el[:, ks] + _dot_tn(v_ref[rows, vs], kd_ref[rows, ks])
            o = o * lax.rsqrt(jnp.mean(o * o, axis=-1, keepdims=True) + EPS) * gon
            o_ref[rows, vs] = (o * sg_ref[rows, vs].astype(F32)).astype(BF16)


def _gla_core(proj, el, gon, batch, seq, kd, vd):
    n = proj.shape[0]
    hk = kd // GLA_HEADS
    hv = vd // GLA_HEADS
    proj3 = proj.reshape(batch, seq, proj.shape[1])
    el3 = el.reshape(batch, seq // GLA_CHUNK, kd)
    kblk = 2 * vd // kd
    out = pl.pallas_call(
        functools.partial(_gla_kernel, hk=hk, hv=hv),
        grid=(batch, seq // GLA_ROWS),
        in_specs=[pl.BlockSpec((None, GLA_ROWS, vd), lambda b, t: (b, t, 0)),
                  pl.BlockSpec((None, GLA_ROWS, vd), lambda b, t: (b, t, 1)),
                  pl.BlockSpec((None, GLA_ROWS, kd), lambda b, t: (b, t, kblk)),
                  pl.BlockSpec((None, GLA_ROWS, kd), lambda b, t: (b, t, kblk + 1)),
                  pl.BlockSpec((None, GLA_ROWS, kd), lambda b, t: (b, t, kblk + 2)),
                  pl.BlockSpec((None, GLA_ROWS // GLA_CHUNK, kd), lambda b, t: (b, t, 0)),
                  pl.BlockSpec((1, hv), lambda b, t: (0, 0))],
        out_specs=pl.BlockSpec((None, GLA_ROWS, vd), lambda b, t: (b, t, 0)),
        out_shape=jax.ShapeDtypeStruct((batch, seq, vd), BF16),
        scratch_shapes=[pltpu.VMEM((GLA_HEADS, hv, hk), F32)],
        compiler_params=_params("parallel", "arbitrary"),
        name="gla_core",
    )(proj3, proj3, proj3, proj3, proj3, el3, gon)
    return out.reshape(n, vd)


_POST_TAIL_INPUTS = {"none": 0, "final": 1, "q": 2, "kvq": 5}


def _post_kernel(*refs, tail, n_cast):
    ins, outs = _run_casts(refs, 6 + _POST_TAIL_INPUTS[tail], n_cast)
    h_ref, o_ref, wo_ref, g_ref, wup_ref, wdn_ref = ins[:6]
    rest = ins[6:] + outs
    h1 = h_ref[...] + _dot(o_ref[...], wo_ref[...])
    hn = _rms(h1, g_ref[...]).astype(BF16)
    acc = h1
    d_ff = wup_ref.shape[1]
    for j in range(0, d_ff, COL_CHUNK):
        u = jnp.maximum(_dot(hn, wup_ref[:, j:j + COL_CHUNK]), 0.0)
        acc = acc + _dot((u * u).astype(BF16), wdn_ref[j:j + COL_CHUNK, :])
    if tail == "none":
        (out_ref,) = rest
    elif tail == "final":
        gfin_ref, out_ref = rest
        acc = _rms(acc, gfin_ref[...])
    else:
        if tail == "kvq":
            gq_ref, wq_ref, gkv_ref, wk_ref, wvt_ref, out_ref, q_ref, kp_ref, vt_ref = rest
        else:
            gq_ref, wq_ref, out_ref, q_ref = rest
        xhat = acc * lax.rsqrt(jnp.mean(acc * acc, axis=-1, keepdims=True) + EPS)
        _store_q((xhat * gq_ref[...]).astype(BF16), wq_ref, q_ref)
        if tail == "kvq":
            _store_kv((xhat * gkv_ref[...]).astype(BF16), wk_ref, wvt_ref, kp_ref, vt_ref)
    out_ref[...] = acc


def _post(h, o, params, tail, casts, n_q_cols=0, n_vt_rows=0):
    n, d = h.shape
    assert len(params) == 4 + _POST_TAIL_INPUTS[tail]
    rows = lambda width: pl.BlockSpec((ROW_TILE, width), lambda i: (i, 0))
    out_specs = [rows(d)]
    out_shape = [jax.ShapeDtypeStruct((n, d), F32)]
    if tail in ("q", "kvq"):
        out_specs.append(rows(n_q_cols))
        out_shape.append(jax.ShapeDtypeStruct((n, n_q_cols), BF16))
    if tail == "kvq":
        out_specs += [rows(4 * LANE), pl.BlockSpec((n_vt_rows, ROW_TILE), lambda i: (0, i))]
        out_shape += [jax.ShapeDtypeStruct((n, 4 * LANE), BF16),
                      jax.ShapeDtypeStruct((n_vt_rows, n), BF16)]
    n_main = len(out_specs)
    outs = pl.pallas_call(
        functools.partial(_post_kernel, tail=tail, n_cast=len(casts)),
        grid=(n // ROW_TILE,),
        in_specs=[rows(d), rows(o.shape[1])] + [spec for _, spec in params]
                 + [c.in_spec for c in casts],
        out_specs=out_specs + [c.out_spec for c in casts],
        out_shape=out_shape + [c.out_shape for c in casts],
        compiler_params=_params("parallel"),
        name="post_mlp",
    )(h, o, *[arr for arr, _ in params], *[c.stack for c in casts])
    return outs[:n_main], outs[n_main:]


def _rel_buckets_merged():
    r = np.arange(BLOCK)[:, None]
    q = np.arange(BLOCK)[None, :]
    n = np.where(r > q, q + BLOCK - r, q - r)
    max_exact = REL_BUCKETS // 2
    large = max_exact + (np.log(np.maximum(n, 1) / max_exact)
                         / np.log(REL_MAX_DIST / max_exact)
                         * (REL_BUCKETS - max_exact)).astype(np.int32)
    large = np.minimum(large, REL_BUCKETS - 1)
    return np.where(n < max_exact, n, large).astype(np.int32)


def _swa_kernel(rel_ref, sink_ref, bucket_ref, q_ref, kpp_ref, kpc_ref, vtp_ref, vtc_ref, o_ref,
                bias_ref, *, n_kv):
    n_q = bias_ref.shape[0]
    hd = SWA_HEAD_DIM
    pairs_per_kv = SWA_GROUP // 2
    neg2 = NEG * LOG2E

    @pl.when((pl.program_id(0) == 0) & (pl.program_id(1) == 0))
    def _():
        bucket = bucket_ref[...]

        def per_head(h, carry):
            def per_bucket(b, acc):
                return jnp.where(bucket == b, rel_ref[b, h] * LOG2E, acc)
            bias_ref[h] = lax.fori_loop(0, REL_BUCKETS, per_bucket,
                                        jnp.zeros(bucket.shape, F32))
            return carry

        lax.fori_loop(0, n_q, per_head, 0)

    key_r = lax.broadcasted_iota(jnp.int32, (BLOCK, BLOCK), 0)
    qry = lax.broadcasted_iota(jnp.int32, (BLOCK, BLOCK), 1)
    lower = key_r > qry
    seq_start = pl.program_id(1) == 0
    zeros = jnp.zeros((hd, 2 * BLOCK), BF16)
    for qb in range(SWA_QBLOCKS):
        rows = slice(qb * BLOCK, (qb + 1) * BLOCK)
        prev_rows = slice((qb - 1) * BLOCK, qb * BLOCK)
        for hkv in range(n_kv):
            def k_band(col):
                cols = slice(col * LANE, (col + 1) * LANE)
                prev = kpp_ref[:, cols] if qb == 0 else kpc_ref[prev_rows, cols]
                return [prev, kpc_ref[rows, cols]]
            k_stack = jnp.concatenate(k_band(2 * hkv) + k_band(2 * hkv + 1), axis=0)
            vrow = slice(hkv * hd, (hkv + 1) * hd)
            if qb == 0:
                vt_prev = jnp.where(seq_start, jnp.zeros((hd, BLOCK), BF16), vtp_ref[vrow, :])
                vt_band = jnp.concatenate([vt_prev, vtc_ref[vrow, :BLOCK]], axis=1)
            else:
                vt_band = vtc_ref[vrow, (qb - 1) * BLOCK:(qb + 1) * BLOCK]
            v_bd = jnp.concatenate([jnp.concatenate([vt_band, zeros], axis=1),
                                    jnp.concatenate([zeros, vt_band], axis=1)], axis=0)
            for pr in range(pairs_per_kv):
                pair = hkv * pairs_per_kv + pr
                cols = slice(pair * LANE, (pair + 1) * LANE)
                st = _dot_nt(k_stack, q_ref[rows, cols])
                ps, rden = [], []
                for e in range(2):
                    hq = 2 * pair + e
                    base = e * 2 * BLOCK
                    s = jnp.where(lower, st[base:base + BLOCK], st[base + BLOCK:base + 2 * BLOCK])
                    s = s + bias_ref[hq]
                    if qb == 0:
                        s = jnp.where(seq_start & lower, neg2, s)
                    sink = sink_ref[hq] * LOG2E
                    m = jnp.maximum(jnp.max(s, axis=0, keepdims=True), jnp.maximum(sink, neg2))
                    p = jnp.exp2(s - m)
                    pm = jnp.exp2(neg2 - m)
                    den = jnp.sum(p, axis=0, keepdims=True) + BLOCK * pm + jnp.exp2(sink - m)
                    pm_tile = jnp.broadcast_to(pm, p.shape)
                    ps += [jnp.where(lower, p, pm_tile).astype(BF16),
                           jnp.where(lower, pm_tile, p).astype(BF16)]
                    rden.append(jnp.broadcast_to(1.0 / den, (hd, BLOCK)))
                ot = _dot(v_bd, jnp.concatenate(ps, axis=0)) * jnp.concatenate(rden, axis=0)
                o_ref[rows, cols] = ot.T.astype(BF16)


def _swa_core(q, kpad, vt, rel_table, sinks, batch, seq):
    n, dq = q.shape
    n_q = dq // SWA_HEAD_DIM
    n_kv = n_q // SWA_GROUP
    rows = SWA_QBLOCKS * BLOCK
    steps = seq // rows
    q3 = q.reshape(batch, seq, dq)
    kp3 = kpad.reshape(batch, seq, kpad.shape[1])
    assert WINDOW == BLOCK
    bucket = jnp.asarray(_rel_buckets_merged())
    smem = pl.BlockSpec(memory_space=pltpu.SMEM)
    prev = lambda t: jnp.maximum(SWA_QBLOCKS * t - 1, 0)
    out = pl.pallas_call(
        functools.partial(_swa_kernel, n_kv=n_kv),
        grid=(batch, steps),
        in_specs=[smem, smem,
                  pl.BlockSpec(bucket.shape, lambda b, t: (0, 0)),
                  pl.BlockSpec((None, rows, dq), lambda b, t: (b, t, 0)),
                  pl.BlockSpec((None, BLOCK, kpad.shape[1]), lambda b, t: (b, prev(t), 0)),
                  pl.BlockSpec((None, rows, kpad.shape[1]), lambda b, t: (b, t, 0)),
                  pl.BlockSpec((vt.shape[0], BLOCK),
                               lambda b, t: (0, b * steps * SWA_QBLOCKS + prev(t))),
                  pl.BlockSpec((vt.shape[0], rows), lambda b, t: (0, b * steps + t))],
        out_specs=pl.BlockSpec((None, rows, dq), lambda b, t: (b, t, 0)),
        out_shape=jax.ShapeDtypeStruct((batch, seq, dq), BF16),
        scratch_shapes=[pltpu.VMEM((n_q,) + bucket.shape, F32)],
        compiler_params=_params("arbitrary", "arbitrary"),
        name="swa_core",
    )(rel_table, sinks, bucket, q3, kp3, kp3, vt, vt)
    return out.reshape(n, dq)


def kernel(x, a_w_in, a_w_gk2, a_b_gk, a_onorm, a_w_out, kv_norm, w_kv, b_w_q, b_sinks, b_w_out,
           rel_table, ln_mix, ln_mlp, w_up, w_down, ln_final):
    batch, seq, d = x.shape
    n_a = a_w_in.shape[0]
    n_b = b_w_q.shape[0]
    kd = a_w_gk2.shape[2]
    vd = (a_w_in.shape[2] - GLA_GATE_RANK - 2 * kd) // 2

    depth = n_a + n_b
    assert n_a >= 1
    half = w_kv.shape[1] // 2
    assert half == LANE and 2 * SWA_HEAD_DIM == LANE

    w_in_t = jnp.swapaxes(a_w_in, 1, 2)
    n_main = 2 * kd + 2 * vd

    w_glr = jnp.pad(w_in_t[:, n_main:, :], ((0, 0), (0, LANE - GLA_GATE_RANK), (0, 0))).astype(BF16)
    w_gk2 = jnp.pad(a_w_gk2, ((0, 0), (0, LANE - GLA_GATE_RANK), (0, 0))).astype(BF16)
    w_vt = w_kv[:, half:].T.astype(BF16)
    g_mix = ln_mix.reshape(depth, 1, d)
    g_mlp = ln_mlp.reshape(depth, 1, d)
    g_kv = kv_norm.reshape(1, d)
    g_final = ln_final.reshape(1, d)
    b_gk = a_b_gk.reshape(n_a, 1, kd)

    calls = []
    for layer in range(depth):
        is_gla = layer < n_a
        j = layer if is_gla else layer - n_a
        if is_gla:
            calls.append(("inproj", layer, {}))
        needs = {"w_out": (a_w_out if is_gla else b_w_out, j, d, d),
                 "w_up": (w_up, layer, d, w_up.shape[2]), "w_down": (w_down, layer, w_down.shape[1], d)}
        if n_a <= layer + 1 < depth:
            needs["w_q"] = (b_w_q, layer + 1 - n_a, d, b_w_q.shape[2])
        if layer + 1 == n_a:
            needs["w_k"] = (w_kv[None], 0, d, half)
        calls.append(("post", layer, needs))
    steps = batch * seq // ROW_TILE
    bf = {name: stack[idx, :rows, :cols].astype(BF16)
          for name, (stack, idx, rows, cols) in calls[0][2].items()}

    h = x.reshape(batch * seq, d)
    for c, (kind, layer, _) in enumerate(calls):
        is_gla = layer < n_a
        j = layer if is_gla else layer - n_a
        nxt = calls[c + 1][2] if c + 1 < len(calls) else {}
        casts = [_Cast(stack, idx, rows, cols, steps) for stack, idx, rows, cols in nxt.values()]
        if kind == "inproj":
            proj, el, cast_out = _gla_inproj(h, j, layer, g_mix, w_in_t, w_glr, w_gk2, b_gk,
                                             kd, vd, casts)
            o = _gla_core(proj, el, a_onorm[j].reshape(1, -1), batch, seq, kd, vd)
        else:
            if not is_gla:
                o = _swa_core(q, kpad, vt, rel_table, b_sinks[j], batch, seq)
            params = [(bf["w_out"], _resident(bf["w_out"].shape)), (g_mlp, _layer(g_mlp, layer)),
                      (bf["w_up"], _resident(bf["w_up"].shape)),
                      (bf["w_down"], _resident(bf["w_down"].shape))]
            if layer == depth - 1:
                (h,), cast_out = _post(h, o, params + [(g_final, _resident(g_final.shape))],
                                       "final", casts)
            elif layer + 1 < n_a:
                (h,), cast_out = _post(h, o, params, "none", casts)
            else:
                params += [(g_mix, _layer(g_mix, layer + 1)), (bf["w_q"], _resident(bf["w_q"].shape))]
                if layer + 1 == n_a:
                    params += [(g_kv, _resident(g_kv.shape)), (bf["w_k"], _resident(bf["w_k"].shape)),
                               (w_vt, _resident(w_vt.shape))]
                    (h, q, kpad, vt), cast_out = _post(h, o, params, "kvq", casts,
                                                       bf["w_q"].shape[1], w_vt.shape[0])
                else:
                    (h, q), cast_out = _post(h, o, params, "q", casts, bf["w_q"].shape[1])
        bf = dict(zip(nxt.keys(), cast_out))
    return h.reshape(batch, seq, d)
```

```python
import functools
import math

import numpy as np
import jax
import jax.numpy as jnp
from jax import lax
from jax.experimental import pallas as pl
from jax.experimental.pallas import tpu as pltpu

F32 = jnp.float32
BF16 = jnp.bfloat16

EPS = 1e-6
NEG = -1e30
LOG2E = math.log2(math.e)

GLA_HEADS = 4
GLA_GATE_RANK = 16
GLA_GATE_NORM = 16.0
GLA_CHUNK = 64

SWA_HEAD_DIM = 64
SWA_GROUP = 8
WINDOW = 128
BLOCK = 128
REL_BUCKETS = 32
REL_MAX_DIST = 128

LANE = 128
SUBLANES = 8
ROW_TILE = 512
COL_CHUNK = 512
GLA_ROWS = 2048
SWA_QBLOCKS = 16
VMEM_LIMIT = 56 * 1024 * 1024

_NT = (((1,), (1,)), ((), ()))
_TN = (((0,), (0,)), ((), ()))


def _rms(x, g):
    ms = jnp.mean(x * x, axis=-1, keepdims=True)
    return x * lax.rsqrt(ms + EPS) * g


def _dot(a, b):
    return jnp.dot(a, b, preferred_element_type=F32)


def _dot_nt(a, b):
    return lax.dot_general(a, b, _NT, preferred_element_type=F32)


def _dot_tn(a, b):
    return lax.dot_general(a, b, _TN, preferred_element_type=F32)


def _params(*sem):
    return pltpu.CompilerParams(dimension_semantics=sem, vmem_limit_bytes=VMEM_LIMIT)


def _resident(shape):
    zeros = (0,) * len(shape)
    return pl.BlockSpec(shape, lambda *_: zeros, pipeline_mode=pl.Buffered(1))


def _layer(stacked, layer):
    idx = (layer,) + (0,) * (stacked.ndim - 1)
    return pl.BlockSpec((None,) + stacked.shape[1:], lambda *_: idx, pipeline_mode=pl.Buffered(1))


class _Cast:
    def __init__(self, stack, layer, rows, cols, steps):
        assert rows % steps == 0 and cols % LANE == 0
        block = (rows // steps, cols)
        self.stack = stack
        self.in_spec = pl.BlockSpec((None,) + block, lambda i: (layer, i, 0))
        self.out_spec = pl.BlockSpec(block, lambda i: (i, 0))
        self.out_shape = jax.ShapeDtypeStruct((rows, cols), BF16)


def _run_casts(refs, n_in, n_cast):
    n_out = len(refs) - n_in - 2 * n_cast
    for src, dst in zip(refs[n_in:n_in + n_cast], refs[n_in + n_cast + n_out:]):
        dst[...] = src[...].astype(BF16)
    return refs[:n_in], refs[n_in + n_cast:n_in + n_cast + n_out]


def _store_q(xn, wq_ref, q_ref):
    scale = SWA_HEAD_DIM ** -0.5 * LOG2E
    for j in range(0, q_ref.shape[-1], COL_CHUNK):
        q_ref[:, j:j + COL_CHUNK] = (_dot(xn, wq_ref[:, j:j + COL_CHUNK]) * scale).astype(BF16)


def _store_kv(xn, wk_ref, wvt_ref, kp_ref, vt_ref):
    k = _dot(xn, wk_ref[...])
    kr = pltpu.roll(k, SWA_HEAD_DIM, axis=1)
    low = lax.broadcasted_iota(jnp.int32, k.shape, 1) < SWA_HEAD_DIM
    pieces = (jnp.where(low, k, 0.0), jnp.where(low, 0.0, kr),
              jnp.where(low, kr, 0.0), jnp.where(low, 0.0, k))
    for i, piece in enumerate(pieces):
        kp_ref[:, i * LANE:(i + 1) * LANE] = piece.astype(BF16)
    vt_ref[...] = _dot_nt(wvt_ref[...], xn).astype(BF16)


def _gla_inproj_kernel(*refs, kd, vd, hk, n_cast):
    *refs, w_ref = refs
    (x_ref, g_ref, w32_ref, wglr_ref, wgk2_ref, bgk_ref), (o_ref, el_ref) = _run_casts(refs, 6, n_cast)

    @pl.when(pl.program_id(0) == 0)
    def _():
        w_ref[...] = w32_ref[:w_ref.shape[0], :].astype(BF16)

    tm = x_ref.shape[0]
    c = GLA_CHUNK
    xn = _rms(x_ref[...], g_ref[...]).astype(BF16)
    glr = _dot_nt(xn, wglr_ref[...])
    z = _dot(glr.astype(BF16), wgk2_ref[...]) + bgk_ref[...]
    for j in range(0, vd, COL_CHUNK):
        o_ref[:, j:j + COL_CHUNK] = _dot_nt(
            xn, w_ref[2 * kd + j:2 * kd + j + COL_CHUNK, :]).astype(BF16)
    for j in range(vd, 2 * vd, COL_CHUNK):
        gt = _dot_nt(xn, w_ref[2 * kd + j:2 * kd + j + COL_CHUNK, :])
        o_ref[:, j:j + COL_CHUNK] = (gt / (1.0 + jnp.exp(-gt))).astype(BF16)
    gk = (jnp.minimum(z, 0.0) - jnp.log(1.0 + jnp.exp(-jnp.abs(z)))) * (LOG2E / GLA_GATE_NORM)
    groups = c // SUBLANES
    g4 = gk.reshape(tm // c, groups, SUBLANES, kd)
    sub = lax.broadcasted_iota(jnp.int32, g4.shape, 2)
    shift = 1
    while shift < SUBLANES:
        g4 = g4 + jnp.where(sub >= shift, pltpu.roll(g4, shift, axis=2), 0.0)
        shift *= 2
    pieces, carry = [], None
    for g in range(groups):
        piece = g4[:, g] if carry is None else g4[:, g] + carry
        pieces.append(piece)
        carry = piece[:, SUBLANES - 1:, :]
    bcum = jnp.stack(pieces, axis=1).reshape(tm, kd)
    q = _dot_nt(xn, w_ref[:kd, :])
    o_ref[:, 2 * vd:2 * vd + kd] = ((q * (hk ** -0.5)) * jnp.exp2(bcum)).astype(BF16)
    k = _dot_nt(xn, w_ref[kd:2 * kd, :])
    kt = k * jnp.exp2(-bcum)
    o_ref[:, 2 * vd + kd:2 * vd + 2 * kd] = kt.astype(BF16)
    for i in range(tm // c):
        el = jnp.exp2(bcum[(i + 1) * c - 1:(i + 1) * c, :])
        el_ref[i:i + 1, :] = el
        o_ref[i * c:(i + 1) * c, 2 * vd + 2 * kd:2 * vd + 3 * kd] = (kt[i * c:(i + 1) * c] * el).astype(BF16)


def _gla_inproj(x, layer, mix_layer, g_mix, w_in_t, w_glr, w_gk2, b_gk, kd, vd, casts):
    n, d = x.shape
    n_out = 2 * vd + 3 * kd
    per_tile = ROW_TILE // GLA_CHUNK
    proj, el, *cast_out = pl.pallas_call(
        functools.partial(_gla_inproj_kernel, kd=kd, vd=vd, hk=kd // GLA_HEADS, n_cast=len(casts)),
        grid=(n // ROW_TILE,),
        in_specs=[pl.BlockSpec((ROW_TILE, d), lambda i: (i, 0)),
                  _layer(g_mix, mix_layer), _layer(w_in_t, layer), _layer(w_glr, layer),
                  _layer(w_gk2, layer), _layer(b_gk, layer)]
                 + [c.in_spec for c in casts],
        out_specs=[pl.BlockSpec((ROW_TILE, n_out), lambda i: (i, 0)),
                   pl.BlockSpec((per_tile, kd), lambda i: (i, 0))] + [c.out_spec for c in casts],
        out_shape=[jax.ShapeDtypeStruct((n, n_out), BF16),
                   jax.ShapeDtypeStruct((n // GLA_CHUNK, kd), F32)] + [c.out_shape for c in casts],
        scratch_shapes=[pltpu.VMEM((2 * kd + 2 * vd, d), BF16)],
        compiler_params=_params("arbitrary"),
        name="gla_inproj",
    )(x, g_mix, w_in_t, w_glr, w_gk2, b_gk, *[c.stack for c in casts])
    return proj, el, cast_out


def _gla_kernel(v_ref, sg_ref, qt_ref, kt_ref, kd_ref, el_ref, gon_ref, o_ref, st_ref, *, hk, hv):
    @pl.when(pl.program_id(1) == 0)
    def _():
        st_ref[...] = jnp.zeros_like(st_ref)

    c = GLA_CHUNK
    hc = GLA_HEADS * c
    ri = lax.broadcasted_iota(jnp.int32, (hc, hc), 0)
    ci = lax.broadcasted_iota(jnp.int32, (hc, hc), 1)
    causal = (ri >= ci) & (ri // c == ci // c)
    gon = gon_ref[...]
    for i in range(v_ref.shape[0] // c):
        rows = slice(i * c, (i + 1) * c)
        el = el_ref[i:i + 1, :]
        heads_k = [slice(h * hk, (h + 1) * hk) for h in range(GLA_HEADS)]
        heads_v = [slice(h * hv, (h + 1) * hv) for h in range(GLA_HEADS)]
        q_stack = jnp.concatenate([qt_ref[rows, ks] for ks in heads_k], axis=0)
        k_stack = jnp.concatenate([kt_ref[rows, ks] for ks in heads_k], axis=0)
        v_stack = jnp.concatenate([v_ref[rows, vs] for vs in heads_v], axis=0)
        attn = jnp.where(causal, _dot_nt(q_stack, k_stack), 0.0).astype(BF16)
        o_intra = _dot(attn, v_stack)
        for h in range(GLA_HEADS):
            ks, vs = heads_k[h], heads_v[h]
            st = st_ref[h]
            o = o_intra[h * c:(h + 1) * c] + _dot_nt(qt_ref[rows, ks], st.astype(BF16))
            st_ref[h] = st * el[:, ks] + _dot_tn(v_ref[rows, vs], kd_ref[rows, ks])
            o = o * lax.rsqrt(jnp.mean(o * o, axis=-1, keepdims=True) + EPS) * gon
            o_ref[rows, vs] = (o * sg_ref[rows, vs].astype(F32)).astype(BF16)


def _gla_core(proj, el, gon, batch, seq, kd, vd):
    n = proj.shape[0]
    hk = kd // GLA_HEADS
    hv = vd // GLA_HEADS
    proj3 = proj.reshape(batch, seq, proj.shape[1])
    el3 = el.reshape(batch, seq // GLA_CHUNK, kd)
    kblk = 2 * vd // kd
    out = pl.pallas_call(
        functools.partial(_gla_kernel, hk=hk, hv=hv),
        grid=(batch, seq // GLA_ROWS),
        in_specs=[pl.BlockSpec((None, GLA_ROWS, vd), lambda b, t: (b, t, 0)),
                  pl.BlockSpec((None, GLA_ROWS, vd), lambda b, t: (b, t, 1)),
                  pl.BlockSpec((None, GLA_ROWS, kd), lambda b, t: (b, t, kblk)),
                  pl.BlockSpec((None, GLA_ROWS, kd), lambda b, t: (b, t, kblk + 1)),
                  pl.BlockSpec((None, GLA_ROWS, kd), lambda b, t: (b, t, kblk + 2)),
                  pl.BlockSpec((None, GLA_ROWS // GLA_CHUNK, kd), lambda b, t: (b, t, 0)),
                  pl.BlockSpec((1, hv), lambda b, t: (0, 0))],
        out_specs=pl.BlockSpec((None, GLA_ROWS, vd), lambda b, t: (b, t, 0)),
        out_shape=jax.ShapeDtypeStruct((batch, seq, vd), BF16),
        scratch_shapes=[pltpu.VMEM((GLA_HEADS, hv, hk), F32)],
        compiler_params=_params("parallel", "arbitrary"),
        name="gla_core",
    )(proj3, proj3, proj3, proj3, proj3, el3, gon)
    return out.reshape(n, vd)


_POST_TAIL_INPUTS = {"none": 0, "final": 1, "q": 2, "kvq": 5}


def _post_kernel(*refs, tail, n_cast):
    *refs, wup_ref, wdn_ref, sem = refs
    ins, outs = _run_casts(refs, 6 + _POST_TAIL_INPUTS[tail], n_cast)
    h_ref, o_ref, wo_ref, g_ref, wup_hbm, wdn_hbm = ins[:6]
    rest = ins[6:] + outs
    first = pl.program_id(0) == 0
    pl.when(first)(functools.partial(_post_tile, h_ref, o_ref, wo_ref, g_ref, wup_hbm, wdn_hbm,
                                     wup_ref, wdn_ref, sem, rest, tail, True))
    pl.when(jnp.logical_not(first))(functools.partial(
        _post_tile, h_ref, o_ref, wo_ref, g_ref, wup_hbm, wdn_hbm, wup_ref, wdn_ref, sem, rest, tail,
        False))


def _post_tile(h_ref, o_ref, wo_ref, g_ref, wup_hbm, wdn_hbm, wup_ref, wdn_ref, sem, rest, tail, fetch):
    d_ff = wup_ref.shape[1]
    chunks = [slice(j, j + COL_CHUNK) for j in range(0, d_ff, COL_CHUNK)]
    if fetch:
        copies = [(pltpu.make_async_copy(wup_hbm.at[:, cols], wup_ref.at[:, cols], sem.at[0, c]),
                   pltpu.make_async_copy(wdn_hbm.at[cols, :], wdn_ref.at[cols, :], sem.at[1, c]))
                  for c, cols in enumerate(chunks)]
        for up, dn in copies:
            up.start()
            dn.start()
    h1 = h_ref[...] + _dot(o_ref[...], wo_ref[...])
    hn = _rms(h1, g_ref[...]).astype(BF16)
    acc = h1
    for c, cols in enumerate(chunks):
        if fetch:
            copies[c][0].wait()
            copies[c][1].wait()
        u = jnp.maximum(_dot(hn, wup_ref[:, cols]), 0.0)
        acc = acc + _dot((u * u).astype(BF16), wdn_ref[cols, :])
    if tail == "none":
        (out_ref,) = rest
    elif tail == "final":
        gfin_ref, out_ref = rest
        acc = _rms(acc, gfin_ref[...])
    else:
        if tail == "kvq":
            gq_ref, wq_ref, gkv_ref, wk_ref, wvt_ref, out_ref, q_ref, kp_ref, vt_ref = rest
        else:
            gq_ref, wq_ref, out_ref, q_ref = rest
        xhat = acc * lax.rsqrt(jnp.mean(acc * acc, axis=-1, keepdims=True) + EPS)
        _store_q((xhat * gq_ref[...]).astype(BF16), wq_ref, q_ref)
        if tail == "kvq":
            _store_kv((xhat * gkv_ref[...]).astype(BF16), wk_ref, wvt_ref, kp_ref, vt_ref)
    out_ref[...] = acc


def _post(h, o, params, tail, casts, n_q_cols=0, n_vt_rows=0):
    n, d = h.shape
    assert len(params) == 4 + _POST_TAIL_INPUTS[tail]
    w_up, w_down = params[2][0], params[3][0]
    rows = lambda width: pl.BlockSpec((ROW_TILE, width), lambda i: (i, 0))
    out_specs = [rows(d)]
    out_shape = [jax.ShapeDtypeStruct((n, d), F32)]
    if tail in ("q", "kvq"):
        out_specs.append(rows(n_q_cols))
        out_shape.append(jax.ShapeDtypeStruct((n, n_q_cols), BF16))
    if tail == "kvq":
        out_specs += [rows(4 * LANE), pl.BlockSpec((n_vt_rows, ROW_TILE), lambda i: (0, i))]
        out_shape += [jax.ShapeDtypeStruct((n, 4 * LANE), BF16),
                      jax.ShapeDtypeStruct((n_vt_rows, n), BF16)]
    n_main = len(out_specs)
    outs = pl.pallas_call(
        functools.partial(_post_kernel, tail=tail, n_cast=len(casts)),
        grid=(n // ROW_TILE,),
        in_specs=[rows(d), rows(o.shape[1])] + [spec for _, spec in params]
                 + [c.in_spec for c in casts],
        out_specs=out_specs + [c.out_spec for c in casts],
        out_shape=out_shape + [c.out_shape for c in casts],
        scratch_shapes=[pltpu.VMEM(w_up.shape, BF16), pltpu.VMEM(w_down.shape, BF16),
                        pltpu.SemaphoreType.DMA((2, w_up.shape[1] // COL_CHUNK))],
        compiler_params=_params("arbitrary"),
        name="post_mlp",
    )(h, o, *[arr for arr, _ in params], *[c.stack for c in casts])
    return outs[:n_main], outs[n_main:]


def _rel_buckets_merged():
    r = np.arange(BLOCK)[:, None]
    q = np.arange(BLOCK)[None, :]
    n = np.where(r > q, q + BLOCK - r, q - r)
    max_exact = REL_BUCKETS // 2
    large = max_exact + (np.log(np.maximum(n, 1) / max_exact)
                         / np.log(REL_MAX_DIST / max_exact)
                         * (REL_BUCKETS - max_exact)).astype(np.int32)
    large = np.minimum(large, REL_BUCKETS - 1)
    return np.where(n < max_exact, n, large).astype(np.int32)


def _swa_kernel(rel_ref, sink_ref, bucket_ref, q_ref, kpp_ref, kpc_ref, vtp_ref, vtc_ref, o_ref,
                bias_ref, *, n_kv):
    n_q = bias_ref.shape[0]
    hd = SWA_HEAD_DIM
    pairs_per_kv = SWA_GROUP // 2
    neg2 = NEG * LOG2E

    @pl.when((pl.program_id(0) == 0) & (pl.program_id(1) == 0))
    def _():
        bucket = bucket_ref[...]

        def per_head(h, carry):
            def per_bucket(b, acc):
                return jnp.where(bucket == b, rel_ref[b, h] * LOG2E, acc)
            bias_ref[h] = lax.fori_loop(0, REL_BUCKETS, per_bucket,
                                        jnp.zeros(bucket.shape, F32))
            return carry

        lax.fori_loop(0, n_q, per_head, 0)

    key_r = lax.broadcasted_iota(jnp.int32, (BLOCK, BLOCK), 0)
    qry = lax.broadcasted_iota(jnp.int32, (BLOCK, BLOCK), 1)
    lower = key_r > qry
    seq_start = pl.program_id(1) == 0
    zeros = jnp.zeros((hd, 2 * BLOCK), BF16)
    for qb in range(SWA_QBLOCKS):
        rows = slice(qb * BLOCK, (qb + 1) * BLOCK)
        prev_rows = slice((qb - 1) * BLOCK, qb * BLOCK)
        for hkv in range(n_kv):
            def k_band(col):
                cols = slice(col * LANE, (col + 1) * LANE)
                prev = kpp_ref[:, cols] if qb == 0 else kpc_ref[prev_rows, cols]
                return [prev, kpc_ref[rows, cols]]
            k_stack = jnp.concatenate(k_band(2 * hkv) + k_band(2 * hkv + 1), axis=0)
            vrow = slice(hkv * hd, (hkv + 1) * hd)
            if qb == 0:
                vt_prev = jnp.where(seq_start, jnp.zeros((hd, BLOCK), BF16), vtp_ref[vrow, :])
                vt_band = jnp.concatenate([vt_prev, vtc_ref[vrow, :BLOCK]], axis=1)
            else:
                vt_band = vtc_ref[vrow, (qb - 1) * BLOCK:(qb + 1) * BLOCK]
            v_bd = jnp.concatenate([jnp.concatenate([vt_band, zeros], axis=1),
                                    jnp.concatenate([zeros, vt_band], axis=1)], axis=0)
            for pr in range(pairs_per_kv):
                pair = hkv * pairs_per_kv + pr
                cols = slice(pair * LANE, (pair + 1) * LANE)
                st = _dot_nt(k_stack, q_ref[rows, cols])
                ps, rden = [], []
                for e in range(2):
                    hq = 2 * pair + e
                    base = e * 2 * BLOCK
                    s = jnp.where(lower, st[base:base + BLOCK], st[base + BLOCK:base + 2 * BLOCK])
                    s = s + bias_ref[hq]
                    if qb == 0:
                        s = jnp.where(seq_start & lower, neg2, s)
                    sink = sink_ref[hq] * LOG2E
                    m = jnp.maximum(jnp.max(s, axis=0, keepdims=True), jnp.maximum(sink, neg2))
                    p = jnp.exp2(s - m)
                    pm = jnp.exp2(neg2 - m)
                    den = jnp.sum(p, axis=0, keepdims=True) + BLOCK * pm + jnp.exp2(sink - m)
                    pm_tile = jnp.broadcast_to(pm, p.shape)
                    ps += [jnp.where(lower, p, pm_tile).astype(BF16),
                           jnp.where(lower, pm_tile, p).astype(BF16)]
                    rden.append(jnp.broadcast_to(1.0 / den, (hd, BLOCK)))
                ot = _dot(v_bd, jnp.concatenate(ps, axis=0)) * jnp.concatenate(rden, axis=0)
                o_ref[rows, cols] = ot.T.astype(BF16)


def _swa_core(q, kpad, vt, rel_table, sinks, batch, seq):
    n, dq = q.shape
    n_q = dq // SWA_HEAD_DIM
    n_kv = n_q // SWA_GROUP
    rows = SWA_QBLOCKS * BLOCK
    steps = seq // rows
    q3 = q.reshape(batch, seq, dq)
    kp3 = kpad.reshape(batch, seq, kpad.shape[1])
    assert WINDOW == BLOCK
    bucket = jnp.asarray(_rel_buckets_merged())
    smem = pl.BlockSpec(memory_space=pltpu.SMEM)
    prev = lambda t: jnp.maximum(SWA_QBLOCKS * t - 1, 0)
    out = pl.pallas_call(
        functools.partial(_swa_kernel, n_kv=n_kv),
        grid=(batch, steps),
        in_specs=[smem, smem,
                  pl.BlockSpec(bucket.shape, lambda b, t: (0, 0)),
                  pl.BlockSpec((None, rows, dq), lambda b, t: (b, t, 0)),
                  pl.BlockSpec((None, BLOCK, kpad.shape[1]), lambda b, t: (b, prev(t), 0)),
                  pl.BlockSpec((None, rows, kpad.shape[1]), lambda b, t: (b, t, 0)),
                  pl.BlockSpec((vt.shape[0], BLOCK),
                               lambda b, t: (0, b * steps * SWA_QBLOCKS + prev(t))),
                  pl.BlockSpec((vt.shape[0], rows), lambda b, t: (0, b * steps + t))],
        out_specs=pl.BlockSpec((None, rows, dq), lambda b, t: (b, t, 0)),
        out_shape=jax.ShapeDtypeStruct((batch, seq, dq), BF16),
        scratch_shapes=[pltpu.VMEM((n_q,) + bucket.shape, F32)],
        compiler_params=_params("arbitrary", "arbitrary"),
        name="swa_core",
    )(rel_table, sinks, bucket, q3, kp3, kp3, vt, vt)
    return out.reshape(n, dq)


def kernel(x, a_w_in, a_w_gk2, a_b_gk, a_onorm, a_w_out, kv_norm, w_kv, b_w_q, b_sinks, b_w_out,
           rel_table, ln_mix, ln_mlp, w_up, w_down, ln_final):
    batch, seq, d = x.shape
    n_a = a_w_in.shape[0]
    n_b = b_w_q.shape[0]
    kd = a_w_gk2.shape[2]
    vd = (a_w_in.shape[2] - GLA_GATE_RANK - 2 * kd) // 2

    depth = n_a + n_b
    assert n_a >= 1
    half = w_kv.shape[1] // 2
    assert half == LANE and 2 * SWA_HEAD_DIM == LANE

    w_in_t = jnp.swapaxes(a_w_in, 1, 2)
    n_main = 2 * kd + 2 * vd

    w_glr = jnp.pad(w_in_t[:, n_main:, :], ((0, 0), (0, LANE - GLA_GATE_RANK), (0, 0))).astype(BF16)
    w_gk2 = jnp.pad(a_w_gk2, ((0, 0), (0, LANE - GLA_GATE_RANK), (0, 0))).astype(BF16)
    w_vt = w_kv[:, half:].T.astype(BF16)
    g_mix = ln_mix.reshape(depth, 1, d)
    g_mlp = ln_mlp.reshape(depth, 1, d)
    g_kv = kv_norm.reshape(1, d)
    g_final = ln_final.reshape(1, d)
    b_gk = a_b_gk.reshape(n_a, 1, kd)

    calls = []
    for layer in range(depth):
        is_gla = layer < n_a
        j = layer if is_gla else layer - n_a
        if is_gla:
            calls.append(("inproj", layer, {}))
        needs = {"w_out": (a_w_out if is_gla else b_w_out, j, d, d),
                 "w_up": (w_up, layer, d, w_up.shape[2]), "w_down": (w_down, layer, w_down.shape[1], d)}
        if n_a <= layer + 1 < depth:
            needs["w_q"] = (b_w_q, layer + 1 - n_a, d, b_w_q.shape[2])
        if layer + 1 == n_a:
            needs["w_k"] = (w_kv[None], 0, d, half)
        calls.append(("post", layer, needs))
    steps = batch * seq // ROW_TILE
    bf = {name: stack[idx, :rows, :cols].astype(BF16)
          for name, (stack, idx, rows, cols) in calls[0][2].items()}

    h = x.reshape(batch * seq, d)
    for c, (kind, layer, _) in enumerate(calls):
        is_gla = layer < n_a
        j = layer if is_gla else layer - n_a
        nxt = calls[c + 1][2] if c + 1 < len(calls) else {}
        casts = [_Cast(stack, idx, rows, cols, steps) for stack, idx, rows, cols in nxt.values()]
        if kind == "inproj":
            proj, el, cast_out = _gla_inproj(h, j, layer, g_mix, w_in_t, w_glr, w_gk2, b_gk,
                                             kd, vd, casts)
            o = _gla_core(proj, el, a_onorm[j].reshape(1, -1), batch, seq, kd, vd)
        else:
            if not is_gla:
                o = _swa_core(q, kpad, vt, rel_table, b_sinks[j], batch, seq)
            params = [(bf["w_out"], _resident(bf["w_out"].shape)), (g_mlp, _layer(g_mlp, layer)),
                      (bf["w_up"], pl.BlockSpec(memory_space=pl.ANY)),
                      (bf["w_down"], pl.BlockSpec(memory_space=pl.ANY))]
            if layer == depth - 1:
                (h,), cast_out = _post(h, o, params + [(g_final, _resident(g_final.shape))],
                                       "final", casts)
            elif layer + 1 < n_a:
                (h,), cast_out = _post(h, o, params, "none", casts)
            else:
                params += [(g_mix, _layer(g_mix, layer + 1)), (bf["w_q"], _resident(bf["w_q"].shape))]
                if layer + 1 == n_a:
                    params += [(g_kv, _resident(g_kv.shape)), (bf["w_k"], _resident(bf["w_k"].shape)),
                               (w_vt, _resident(w_vt.shape))]
                    (h, q, kpad, vt), cast_out = _post(h, o, params, "kvq", casts,
                                                       bf["w_q"].shape[1], w_vt.shape[0])
                else:
                    (h, q), cast_out = _post(h, o, params, "q", casts, bf["w_q"].shape[1])
        bf = dict(zip(nxt.keys(), cast_out))
    return h.reshape(batch, seq, d)
```

```python
import functools
import math

import numpy as np
import jax
import jax.numpy as jnp
from jax import lax
from jax.experimental import pallas as pl
from jax.experimental.pallas import tpu as pltpu

F32 = jnp.float32
BF16 = jnp.bfloat16

EPS = 1e-6
NEG = -1e30
LOG2E = math.log2(math.e)

GLA_HEADS = 4
GLA_GATE_RANK = 16
GLA_GATE_NORM = 16.0
GLA_CHUNK = 64

SWA_HEAD_DIM = 64
SWA_GROUP = 8
WINDOW = 128
BLOCK = 128
REL_BUCKETS = 32
REL_MAX_DIST = 128

LANE = 128
SUBLANES = 8
ROW_TILE = 512
COL_CHUNK = 512
GLA_ROWS = 2048
SWA_QBLOCKS = 16
VMEM_LIMIT = 56 * 1024 * 1024

_NT = (((1,), (1,)), ((), ()))
_TN = (((0,), (0,)), ((), ()))


def _rms(x, g):
    ms = jnp.mean(x * x, axis=-1, keepdims=True)
    return x * lax.rsqrt(ms + EPS) * g


def _dot(a, b):
    return jnp.dot(a, b, preferred_element_type=F32)


def _dot_nt(a, b):
    return lax.dot_general(a, b, _NT, preferred_element_type=F32)


def _dot_tn(a, b):
    return lax.dot_general(a, b, _TN, preferred_element_type=F32)


def _params(*sem):
    return pltpu.CompilerParams(dimension_semantics=sem, vmem_limit_bytes=VMEM_LIMIT)


def _resident(shape):
    zeros = (0,) * len(shape)
    return pl.BlockSpec(shape, lambda *_: zeros, pipeline_mode=pl.Buffered(1))


def _layer(stacked, layer):
    idx = (layer,) + (0,) * (stacked.ndim - 1)
    return pl.BlockSpec((None,) + stacked.shape[1:], lambda *_: idx, pipeline_mode=pl.Buffered(1))


class _Cast:
    def __init__(self, stack, layer, rows, cols, steps):
        assert rows % steps == 0 and cols % LANE == 0
        block = (rows // steps, cols)
        self.stack = stack
        self.in_spec = pl.BlockSpec((None,) + block, lambda i: (layer, i, 0))
        self.out_spec = pl.BlockSpec(block, lambda i: (i, 0))
        self.out_shape = jax.ShapeDtypeStruct((rows, cols), BF16)


def _run_casts(refs, n_in, n_cast):
    n_out = len(refs) - n_in - 2 * n_cast
    for src, dst in zip(refs[n_in:n_in + n_cast], refs[n_in + n_cast + n_out:]):
        dst[...] = src[...].astype(BF16)
    return refs[:n_in], refs[n_in + n_cast:n_in + n_cast + n_out]


def _store_q(xn, wq_ref, q_ref):
    scale = SWA_HEAD_DIM ** -0.5 * LOG2E
    for j in range(0, q_ref.shape[-1], COL_CHUNK):
        q_ref[:, j:j + COL_CHUNK] = (_dot(xn, wq_ref[:, j:j + COL_CHUNK]) * scale).astype(BF16)


def _store_kv(xn, wk_ref, wvt_ref, kp_ref, vt_ref):
    k = _dot(xn, wk_ref[...])
    kr = pltpu.roll(k, SWA_HEAD_DIM, axis=1)
    low = lax.broadcasted_iota(jnp.int32, k.shape, 1) < SWA_HEAD_DIM
    pieces = (jnp.where(low, k, 0.0), jnp.where(low, 0.0, kr),
              jnp.where(low, kr, 0.0), jnp.where(low, 0.0, k))
    for i, piece in enumerate(pieces):
        kp_ref[:, i * LANE:(i + 1) * LANE] = piece.astype(BF16)
    vt_ref[...] = _dot_nt(wvt_ref[...], xn).astype(BF16)


def _gla_inproj_kernel(*refs, layer, kd, vd, hk, n_cast):
    *refs, w_ref, w32_ref, sem = refs
    ins, outs = _run_casts(refs, 6, n_cast)
    first = pl.program_id(0) == 0
    tile = functools.partial(_gla_inproj_tile, ins, outs, w_ref, w32_ref, sem, layer, kd, vd, hk)
    pl.when(first)(functools.partial(tile, True))
    pl.when(jnp.logical_not(first))(functools.partial(tile, False))


def _gla_inproj_tile(ins, outs, w_ref, w32_ref, sem, layer, kd, vd, hk, fetch):
    x_ref, g_ref, whbm_ref, wglr_ref, wgk2_ref, bgk_ref = ins
    o_ref, el_ref = outs
    chunks = ([slice(2 * kd + j, 2 * kd + j + COL_CHUNK) for j in range(0, 2 * vd, COL_CHUNK)]
              + [slice(0, kd), slice(kd, 2 * kd)])
    if fetch:
        copies = [pltpu.make_async_copy(whbm_ref.at[layer, rows, :], w32_ref.at[rows, :], sem.at[i])
                  for i, rows in enumerate(chunks)]
        for copy in copies:
            copy.start()

    def weights(i):
        rows = chunks[i]
        if fetch:
            copies[i].wait()
            w_ref[rows, :] = w32_ref[rows, :].astype(BF16)
        return w_ref[rows, :]

    tm = x_ref.shape[0]
    c = GLA_CHUNK
    xn = _rms(x_ref[...], g_ref[...]).astype(BF16)
    glr = _dot_nt(xn, wglr_ref[...])
    z = _dot(glr.astype(BF16), wgk2_ref[...]) + bgk_ref[...]
    for j in range(0, vd, COL_CHUNK):
        o_ref[:, j:j + COL_CHUNK] = _dot_nt(xn, weights(j // COL_CHUNK)).astype(BF16)
    for j in range(vd, 2 * vd, COL_CHUNK):
        gt = _dot_nt(xn, weights(j // COL_CHUNK))
        o_ref[:, j:j + COL_CHUNK] = (gt / (1.0 + jnp.exp(-gt))).astype(BF16)
    gk = (jnp.minimum(z, 0.0) - jnp.log(1.0 + jnp.exp(-jnp.abs(z)))) * (LOG2E / GLA_GATE_NORM)
    groups = c // SUBLANES
    g4 = gk.reshape(tm // c, groups, SUBLANES, kd)
    sub = lax.broadcasted_iota(jnp.int32, g4.shape, 2)
    shift = 1
    while shift < SUBLANES:
        g4 = g4 + jnp.where(sub >= shift, pltpu.roll(g4, shift, axis=2), 0.0)
        shift *= 2
    pieces, carry = [], None
    for g in range(groups):
        piece = g4[:, g] if carry is None else g4[:, g] + carry
        pieces.append(piece)
        carry = piece[:, SUBLANES - 1:, :]
    bcum = jnp.stack(pieces, axis=1).reshape(tm, kd)
    q = _dot_nt(xn, weights(len(chunks) - 2))
    o_ref[:, 2 * vd:2 * vd + kd] = ((q * (hk ** -0.5)) * jnp.exp2(bcum)).astype(BF16)
    k = _dot_nt(xn, weights(len(chunks) - 1))
    kt = k * jnp.exp2(-bcum)
    o_ref[:, 2 * vd + kd:2 * vd + 2 * kd] = kt.astype(BF16)
    for i in range(tm // c):
        el = jnp.exp2(bcum[(i + 1) * c - 1:(i + 1) * c, :])
        el_ref[i:i + 1, :] = el
        o_ref[i * c:(i + 1) * c, 2 * vd + 2 * kd:2 * vd + 3 * kd] = (kt[i * c:(i + 1) * c] * el).astype(BF16)


def _gla_inproj(x, layer, mix_layer, g_mix, w_in_t, w_glr, w_gk2, b_gk, kd, vd, casts):
    n, d = x.shape
    n_out = 2 * vd + 3 * kd
    per_tile = ROW_TILE // GLA_CHUNK
    proj, el, *cast_out = pl.pallas_call(
        functools.partial(_gla_inproj_kernel, layer=layer, kd=kd, vd=vd, hk=kd // GLA_HEADS,
                          n_cast=len(casts)),
        grid=(n // ROW_TILE,),
        in_specs=[pl.BlockSpec((ROW_TILE, d), lambda i: (i, 0)),
                  _layer(g_mix, mix_layer), pl.BlockSpec(memory_space=pl.ANY), _layer(w_glr, layer),
                  _layer(w_gk2, layer), _layer(b_gk, layer)]
                 + [c.in_spec for c in casts],
        out_specs=[pl.BlockSpec((ROW_TILE, n_out), lambda i: (i, 0)),
                   pl.BlockSpec((per_tile, kd), lambda i: (i, 0))] + [c.out_spec for c in casts],
        out_shape=[jax.ShapeDtypeStruct((n, n_out), BF16),
                   jax.ShapeDtypeStruct((n // GLA_CHUNK, kd), F32)] + [c.out_shape for c in casts],
        scratch_shapes=[pltpu.VMEM((2 * kd + 2 * vd, d), BF16), pltpu.VMEM((2 * kd + 2 * vd, d), F32),
                        pltpu.SemaphoreType.DMA((2 * vd // COL_CHUNK + 2,))],
        compiler_params=_params("arbitrary"),
        name="gla_inproj",
    )(x, g_mix, w_in_t, w_glr, w_gk2, b_gk, *[c.stack for c in casts])
    return proj, el, cast_out


def _gla_kernel(v_ref, sg_ref, qt_ref, kt_ref, kd_ref, el_ref, gon_ref, o_ref, st_ref, *, hk, hv):
    @pl.when(pl.program_id(1) == 0)
    def _():
        st_ref[...] = jnp.zeros_like(st_ref)

    c = GLA_CHUNK
    hc = GLA_HEADS * c
    ri = lax.broadcasted_iota(jnp.int32, (hc, hc), 0)
    ci = lax.broadcasted_iota(jnp.int32, (hc, hc), 1)
    causal = (ri >= ci) & (ri // c == ci // c)
    gon = gon_ref[...]
    for i in range(v_ref.shape[0] // c):
        rows = slice(i * c, (i + 1) * c)
        el = el_ref[i:i + 1, :]
        heads_k = [slice(h * hk, (h + 1) * hk) for h in range(GLA_HEADS)]
        heads_v = [slice(h * hv, (h + 1) * hv) for h in range(GLA_HEADS)]
        q_stack = jnp.concatenate([qt_ref[rows, ks] for ks in heads_k], axis=0)
        k_stack = jnp.concatenate([kt_ref[rows, ks] for ks in heads_k], axis=0)
        v_stack = jnp.concatenate([v_ref[rows, vs] for vs in heads_v], axis=0)
        attn = jnp.where(causal, _dot_nt(q_stack, k_stack), 0.0).astype(BF16)
        o_intra = _dot(attn, v_stack)
        for h in range(GLA_HEADS):
            ks, vs = heads_k[h], heads_v[h]
            st = st_ref[h]
            o = o_intra[h * c:(h + 1) * c] + _dot_nt(qt_ref[rows, ks], st.astype(BF16))
            st_ref[h] = st * el[:, ks] + _dot_tn(v_ref[rows, vs], kd_ref[rows, ks])
            o = o * lax.rsqrt(jnp.mean(o * o, axis=-1, keepdims=True) + EPS) * gon
            o_ref[rows, vs] = (o * sg_ref[rows, vs].astype(F32)).astype(BF16)


def _gla_core(proj, el, gon, batch, seq, kd, vd):
    n = proj.shape[0]
    hk = kd // GLA_HEADS
    hv = vd // GLA_HEADS
    proj3 = proj.reshape(batch, seq, proj.shape[1])
    el3 = el.reshape(batch, seq // GLA_CHUNK, kd)
    kblk = 2 * vd // kd
    out = pl.pallas_call(
        functools.partial(_gla_kernel, hk=hk, hv=hv),
        grid=(batch, seq // GLA_ROWS),
        in_specs=[pl.BlockSpec((None, GLA_ROWS, vd), lambda b, t: (b, t, 0)),
                  pl.BlockSpec((None, GLA_ROWS, vd), lambda b, t: (b, t, 1)),
                  pl.BlockSpec((None, GLA_ROWS, kd), lambda b, t: (b, t, kblk)),
                  pl.BlockSpec((None, GLA_ROWS, kd), lambda b, t: (b, t, kblk + 1)),
                  pl.BlockSpec((None, GLA_ROWS, kd), lambda b, t: (b, t, kblk + 2)),
                  pl.BlockSpec((None, GLA_ROWS // GLA_CHUNK, kd), lambda b, t: (b, t, 0)),
                  pl.BlockSpec((1, hv), lambda b, t: (0, 0))],
        out_specs=pl.BlockSpec((None, GLA_ROWS, vd), lambda b, t: (b, t, 0)),
        out_shape=jax.ShapeDtypeStruct((batch, seq, vd), BF16),
        scratch_shapes=[pltpu.VMEM((GLA_HEADS, hv, hk), F32)],
        compiler_params=_params("parallel", "arbitrary"),
        name="gla_core",
    )(proj3, proj3, proj3, proj3, proj3, el3, gon)
    return out.reshape(n, vd)


_POST_TAIL_INPUTS = {"none": 0, "final": 1, "q": 2, "kvq": 5}


def _post_kernel(*refs, tail, n_cast):
    *refs, wup_ref, wdn_ref, sem = refs
    ins, outs = _run_casts(refs, 6 + _POST_TAIL_INPUTS[tail], n_cast)
    h_ref, o_ref, wo_ref, g_ref, wup_hbm, wdn_hbm = ins[:6]
    rest = ins[6:] + outs
    first = pl.program_id(0) == 0
    pl.when(first)(functools.partial(_post_tile, h_ref, o_ref, wo_ref, g_ref, wup_hbm, wdn_hbm,
                                     wup_ref, wdn_ref, sem, rest, tail, True))
    pl.when(jnp.logical_not(first))(functools.partial(
        _post_tile, h_ref, o_ref, wo_ref, g_ref, wup_hbm, wdn_hbm, wup_ref, wdn_ref, sem, rest, tail,
        False))


def _post_tile(h_ref, o_ref, wo_ref, g_ref, wup_hbm, wdn_hbm, wup_ref, wdn_ref, sem, rest, tail, fetch):
    d_ff = wup_ref.shape[1]
    chunks = [slice(j, j + COL_CHUNK) for j in range(0, d_ff, COL_CHUNK)]
    if fetch:
        copies = [(pltpu.make_async_copy(wup_hbm.at[:, cols], wup_ref.at[:, cols], sem.at[0, c]),
                   pltpu.make_async_copy(wdn_hbm.at[cols, :], wdn_ref.at[cols, :], sem.at[1, c]))
                  for c, cols in enumerate(chunks)]
        for up, dn in copies:
            up.start()
            dn.start()
    h1 = h_ref[...] + _dot(o_ref[...], wo_ref[...])
    hn = _rms(h1, g_ref[...]).astype(BF16)
    acc = h1
    for c, cols in enumerate(chunks):
        if fetch:
            copies[c][0].wait()
            copies[c][1].wait()
        u = jnp.maximum(_dot(hn, wup_ref[:, cols]), 0.0)
        acc = acc + _dot((u * u).astype(BF16), wdn_ref[cols, :])
    if tail == "none":
        (out_ref,) = rest
    elif tail == "final":
        gfin_ref, out_ref = rest
        acc = _rms(acc, gfin_ref[...])
    else:
        if tail == "kvq":
            gq_ref, wq_ref, gkv_ref, wk_ref, wvt_ref, out_ref, q_ref, kp_ref, vt_ref = rest
        else:
            gq_ref, wq_ref, out_ref, q_ref = rest
        xhat = acc * lax.rsqrt(jnp.mean(acc * acc, axis=-1, keepdims=True) + EPS)
        _store_q((xhat * gq_ref[...]).astype(BF16), wq_ref, q_ref)
        if tail == "kvq":
            _store_kv((xhat * gkv_ref[...]).astype(BF16), wk_ref, wvt_ref, kp_ref, vt_ref)
    out_ref[...] = acc


def _post(h, o, params, tail, casts, n_q_cols=0, n_vt_rows=0):
    n, d = h.shape
    assert len(params) == 4 + _POST_TAIL_INPUTS[tail]
    w_up, w_down = params[2][0], params[3][0]
    rows = lambda width: pl.BlockSpec((ROW_TILE, width), lambda i: (i, 0))
    out_specs = [rows(d)]
    out_shape = [jax.ShapeDtypeStruct((n, d), F32)]
    if tail in ("q", "kvq"):
        out_specs.append(rows(n_q_cols))
        out_shape.append(jax.ShapeDtypeStruct((n, n_q_cols), BF16))
    if tail == "kvq":
        out_specs += [rows(4 * LANE), pl.BlockSpec((n_vt_rows, ROW_TILE), lambda i: (0, i))]
        out_shape += [jax.ShapeDtypeStruct((n, 4 * LANE), BF16),
                      jax.ShapeDtypeStruct((n_vt_rows, n), BF16)]
    n_main = len(out_specs)
    outs = pl.pallas_call(
        functools.partial(_post_kernel, tail=tail, n_cast=len(casts)),
        grid=(n // ROW_TILE,),
        in_specs=[rows(d), rows(o.shape[1])] + [spec for _, spec in params]
                 + [c.in_spec for c in casts],
        out_specs=out_specs + [c.out_spec for c in casts],
        out_shape=out_shape + [c.out_shape for c in casts],
        scratch_shapes=[pltpu.VMEM(w_up.shape, BF16), pltpu.VMEM(w_down.shape, BF16),
                        pltpu.SemaphoreType.DMA((2, w_up.shape[1] // COL_CHUNK))],
        compiler_params=_params("arbitrary"),
        name="post_mlp",
    )(h, o, *[arr for arr, _ in params], *[c.stack for c in casts])
    return outs[:n_main], outs[n_main:]


def _rel_buckets_merged():
    r = np.arange(BLOCK)[:, None]
    q = np.arange(BLOCK)[None, :]
    n = np.where(r > q, q + BLOCK - r, q - r)
    max_exact = REL_BUCKETS // 2
    large = max_exact + (np.log(np.maximum(n, 1) / max_exact)
                         / np.log(REL_MAX_DIST / max_exact)
                         * (REL_BUCKETS - max_exact)).astype(np.int32)
    large = np.minimum(large, REL_BUCKETS - 1)
    return np.where(n < max_exact, n, large).astype(np.int32)


def _swa_kernel(rel_ref, sink_ref, bucket_ref, q_ref, kpp_ref, kpc_ref, vtp_ref, vtc_ref, o_ref,
                bias_ref, *, n_kv):
    n_q = bias_ref.shape[0]
    hd = SWA_HEAD_DIM
    pairs_per_kv = SWA_GROUP // 2
    neg2 = NEG * LOG2E

    @pl.when((pl.program_id(0) == 0) & (pl.program_id(1) == 0))
    def _():
        bucket = bucket_ref[...]

        def per_head(h, carry):
            def per_bucket(b, acc):
                return jnp.where(bucket == b, rel_ref[b, h] * LOG2E, acc)
            bias_ref[h] = lax.fori_loop(0, REL_BUCKETS, per_bucket,
                                        jnp.zeros(bucket.shape, F32))
            return carry

        lax.fori_loop(0, n_q, per_head, 0)

    key_r = lax.broadcasted_iota(jnp.int32, (BLOCK, BLOCK), 0)
    qry = lax.broadcasted_iota(jnp.int32, (BLOCK, BLOCK), 1)
    lower = key_r > qry
    seq_start = pl.program_id(1) == 0
    zeros = jnp.zeros((hd, 2 * BLOCK), BF16)
    for qb in range(SWA_QBLOCKS):
        rows = slice(qb * BLOCK, (qb + 1) * BLOCK)
        prev_rows = slice((qb - 1) * BLOCK, qb * BLOCK)
        for hkv in range(n_kv):
            def k_band(col):
                cols = slice(col * LANE, (col + 1) * LANE)
                prev = kpp_ref[:, cols] if qb == 0 else kpc_ref[prev_rows, cols]
                return [prev, kpc_ref[rows, cols]]
            k_stack = jnp.concatenate(k_band(2 * hkv) + k_band(2 * hkv + 1), axis=0)
            vrow = slice(hkv * hd, (hkv + 1) * hd)
            if qb == 0:
                vt_prev = jnp.where(seq_start, jnp.zeros((hd, BLOCK), BF16), vtp_ref[vrow, :])
                vt_band = jnp.concatenate([vt_prev, vtc_ref[vrow, :BLOCK]], axis=1)
            else:
                vt_band = vtc_ref[vrow, (qb - 1) * BLOCK:(qb + 1) * BLOCK]
            v_bd = jnp.concatenate([jnp.concatenate([vt_band, zeros], axis=1),
                                    jnp.concatenate([zeros, vt_band], axis=1)], axis=0)
            for pr in range(pairs_per_kv):
                pair = hkv * pairs_per_kv + pr
                cols = slice(pair * LANE, (pair + 1) * LANE)
                st = _dot_nt(k_stack, q_ref[rows, cols])
                ps, rden = [], []
                for e in range(2):
                    hq = 2 * pair + e
                    base = e * 2 * BLOCK
                    s = jnp.where(lower, st[base:base + BLOCK], st[base + BLOCK:base + 2 * BLOCK])
                    s = s + bias_ref[hq]
                    if qb == 0:
                        s = jnp.where(seq_start & lower, neg2, s)
                    sink = sink_ref[hq] * LOG2E
                    m = jnp.maximum(jnp.max(s, axis=0, keepdims=True), jnp.maximum(sink, neg2))
                    p = jnp.exp2(s - m)
                    pm = jnp.exp2(neg2 - m)
                    den = jnp.sum(p, axis=0, keepdims=True) + BLOCK * pm + jnp.exp2(sink - m)
                    pm_tile = jnp.broadcast_to(pm, p.shape)
                    ps += [jnp.where(lower, p, pm_tile).astype(BF16),
                           jnp.where(lower, pm_tile, p).astype(BF16)]
                    rden.append(jnp.broadcast_to(1.0 / den, (hd, BLOCK)))
                ot = _dot(v_bd, jnp.concatenate(ps, axis=0)) * jnp.concatenate(rden, axis=0)
                o_ref[rows, cols] = ot.T.astype(BF16)


def _swa_core(q, kpad, vt, rel_table, sinks, batch, seq):
    n, dq = q.shape
    n_q = dq // SWA_HEAD_DIM
    n_kv = n_q // SWA_GROUP
    rows = SWA_QBLOCKS * BLOCK
    steps = seq // rows
    q3 = q.reshape(batch, seq, dq)
    kp3 = kpad.reshape(batch, seq, kpad.shape[1])
    assert WINDOW == BLOCK
    bucket = jnp.asarray(_rel_buckets_merged())
    smem = pl.BlockSpec(memory_space=pltpu.SMEM)
    prev = lambda t: jnp.maximum(SWA_QBLOCKS * t - 1, 0)
    out = pl.pallas_call(
        functools.partial(_swa_kernel, n_kv=n_kv),
        grid=(batch, steps),
        in_specs=[smem, smem,
                  pl.BlockSpec(bucket.shape, lambda b, t: (0, 0)),
                  pl.BlockSpec((None, rows, dq), lambda b, t: (b, t, 0)),
                  pl.BlockSpec((None, BLOCK, kpad.shape[1]), lambda b, t: (b, prev(t), 0)),
                  pl.BlockSpec((None, rows, kpad.shape[1]), lambda b, t: (b, t, 0)),
                  pl.BlockSpec((vt.shape[0], BLOCK),
                               lambda b, t: (0, b * steps * SWA_QBLOCKS + prev(t))),
                  pl.BlockSpec((vt.shape[0], rows), lambda b, t: (0, b * steps + t))],
        out_specs=pl.BlockSpec((None, rows, dq), lambda b, t: (b, t, 0)),
        out_shape=jax.ShapeDtypeStruct((batch, seq, dq), BF16),
        scratch_shapes=[pltpu.VMEM((n_q,) + bucket.shape, F32)],
        compiler_params=_params("arbitrary", "arbitrary"),
        name="swa_core",
    )(rel_table, sinks, bucket, q3, kp3, kp3, vt, vt)
    return out.reshape(n, dq)


def kernel(x, a_w_in, a_w_gk2, a_b_gk, a_onorm, a_w_out, kv_norm, w_kv, b_w_q, b_sinks, b_w_out,
           rel_table, ln_mix, ln_mlp, w_up, w_down, ln_final):
    batch, seq, d = x.shape
    n_a = a_w_in.shape[0]
    n_b = b_w_q.shape[0]
    kd = a_w_gk2.shape[2]
    vd = (a_w_in.shape[2] - GLA_GATE_RANK - 2 * kd) // 2

    depth = n_a + n_b
    assert n_a >= 1
    half = w_kv.shape[1] // 2
    assert half == LANE and 2 * SWA_HEAD_DIM == LANE

    w_in_t = jnp.swapaxes(a_w_in, 1, 2)
    n_main = 2 * kd + 2 * vd

    w_glr = jnp.pad(w_in_t[:, n_main:, :], ((0, 0), (0, LANE - GLA_GATE_RANK), (0, 0))).astype(BF16)
    w_gk2 = jnp.pad(a_w_gk2, ((0, 0), (0, LANE - GLA_GATE_RANK), (0, 0))).astype(BF16)
    w_vt = w_kv[:, half:].T.astype(BF16)
    g_mix = ln_mix.reshape(depth, 1, d)
    g_mlp = ln_mlp.reshape(depth, 1, d)
    g_kv = kv_norm.reshape(1, d)
    g_final = ln_final.reshape(1, d)
    b_gk = a_b_gk.reshape(n_a, 1, kd)

    calls = []
    for layer in range(depth):
        is_gla = layer < n_a
        j = layer if is_gla else layer - n_a
        if is_gla:
            calls.append(("inproj", layer, {}))
        needs = {"w_out": (a_w_out if is_gla else b_w_out, j, d, d),
                 "w_up": (w_up, layer, d, w_up.shape[2]), "w_down": (w_down, layer, w_down.shape[1], d)}
        if n_a <= layer + 1 < depth:
            needs["w_q"] = (b_w_q, layer + 1 - n_a, d, b_w_q.shape[2])
        if layer + 1 == n_a:
            needs["w_k"] = (w_kv[None], 0, d, half)
        calls.append(("post", layer, needs))
    steps = batch * seq // ROW_TILE
    bf = {name: stack[idx, :rows, :cols].astype(BF16)
          for name, (stack, idx, rows, cols) in calls[0][2].items()}

    h = x.reshape(batch * seq, d)
    for c, (kind, layer, _) in enumerate(calls):
        is_gla = layer < n_a
        j = layer if is_gla else layer - n_a
        nxt = calls[c + 1][2] if c + 1 < len(calls) else {}
        casts = [_Cast(stack, idx, rows, cols, steps) for stack, idx, rows, cols in nxt.values()]
        if kind == "inproj":
            proj, el, cast_out = _gla_inproj(h, j, layer, g_mix, w_in_t, w_glr, w_gk2, b_gk,
                                             kd, vd, casts)
            o = _gla_core(proj, el, a_onorm[j].reshape(1, -1), batch, seq, kd, vd)
        else:
            if not is_gla:
                o = _swa_core(q, kpad, vt, rel_table, b_sinks[j], batch, seq)
            params = [(bf["w_out"], _resident(bf["w_out"].shape)), (g_mlp, _layer(g_mlp, layer)),
                      (bf["w_up"], pl.BlockSpec(memory_space=pl.ANY)),
                      (bf["w_down"], pl.BlockSpec(memory_space=pl.ANY))]
            if layer == depth - 1:
                (h,), cast_out = _post(h, o, params + [(g_final, _resident(g_final.shape))],
                                       "final", casts)
            elif layer + 1 < n_a:
                (h,), cast_out = _post(h, o, params, "none", casts)
            else:
                params += [(g_mix, _layer(g_mix, layer + 1)), (bf["w_q"], _resident(bf["w_q"].shape))]
                if layer + 1 == n_a:
                    params += [(g_kv, _resident(g_kv.shape)), (bf["w_k"], _resident(bf["w_k"].shape)),
                               (w_vt, _resident(w_vt.shape))]
                    (h, q, kpad, vt), cast_out = _post(h, o, params, "kvq", casts,
                                                       bf["w_q"].shape[1], w_vt.shape[0])
                else:
                    (h, q), cast_out = _post(h, o, params, "q", casts, bf["w_q"].shape[1])
        bf = dict(zip(nxt.keys(), cast_out))
    return h.reshape(batch, seq, d)
```

```python
import functools
import math

import numpy as np
import jax
import jax.numpy as jnp
from jax import lax
from jax.experimental import pallas as pl
from jax.experimental.pallas import tpu as pltpu

F32 = jnp.float32
BF16 = jnp.bfloat16

EPS = 1e-6
NEG = -1e30
LOG2E = math.log2(math.e)

GLA_HEADS = 4
GLA_GATE_RANK = 16
GLA_GATE_NORM = 16.0
GLA_CHUNK = 64

SWA_HEAD_DIM = 64
SWA_GROUP = 8
WINDOW = 128
BLOCK = 128
REL_BUCKETS = 32
REL_MAX_DIST = 128

LANE = 128
SUBLANES = 8
ROW_TILE = 512
COL_CHUNK = 512
GLA_ROWS = 2048
SWA_QBLOCKS = 16
VMEM_LIMIT = 56 * 1024 * 1024

_NT = (((1,), (1,)), ((), ()))
_TN = (((0,), (0,)), ((), ()))


def _rms(x, g):
    ms = jnp.mean(x * x, axis=-1, keepdims=True)
    return x * lax.rsqrt(ms + EPS) * g


def _dot(a, b):
    return jnp.dot(a, b, preferred_element_type=F32)


def _dot_nt(a, b):
    return lax.dot_general(a, b, _NT, preferred_element_type=F32)


def _dot_tn(a, b):
    return lax.dot_general(a, b, _TN, preferred_element_type=F32)


def _params(*sem):
    return pltpu.CompilerParams(dimension_semantics=sem, vmem_limit_bytes=VMEM_LIMIT)


def _resident(shape):
    zeros = (0,) * len(shape)
    return pl.BlockSpec(shape, lambda *_: zeros, pipeline_mode=pl.Buffered(1))


def _layer(stacked, layer):
    idx = (layer,) + (0,) * (stacked.ndim - 1)
    return pl.BlockSpec((None,) + stacked.shape[1:], lambda *_: idx, pipeline_mode=pl.Buffered(1))


class _Cast:
    def __init__(self, stack, layer, rows, cols, steps):
        assert rows % steps == 0 and cols % LANE == 0
        block = (rows // steps, cols)
        self.stack = stack
        self.in_spec = pl.BlockSpec((None,) + block, lambda i: (layer, i, 0))
        self.out_spec = pl.BlockSpec(block, lambda i: (i, 0))
        self.out_shape = jax.ShapeDtypeStruct((rows, cols), BF16)


def _run_casts(refs, n_in, n_cast):
    n_out = len(refs) - n_in - 2 * n_cast
    for src, dst in zip(refs[n_in:n_in + n_cast], refs[n_in + n_cast + n_out:]):
        dst[...] = src[...].astype(BF16)
    return refs[:n_in], refs[n_in + n_cast:n_in + n_cast + n_out]


def _store_q(xn, wq_ref, q_ref):
    scale = SWA_HEAD_DIM ** -0.5 * LOG2E
    for j in range(0, q_ref.shape[-1], COL_CHUNK):
        q_ref[:, j:j + COL_CHUNK] = (_dot(xn, wq_ref[:, j:j + COL_CHUNK]) * scale).astype(BF16)


def _store_kv(xn, wk_ref, wvt_ref, kp_ref, vt_ref):
    k = _dot(xn, wk_ref[...])
    kr = pltpu.roll(k, SWA_HEAD_DIM, axis=1)
    low = lax.broadcasted_iota(jnp.int32, k.shape, 1) < SWA_HEAD_DIM
    pieces = (jnp.where(low, k, 0.0), jnp.where(low, 0.0, kr),
              jnp.where(low, kr, 0.0), jnp.where(low, 0.0, k))
    for i, piece in enumerate(pieces):
        kp_ref[:, i * LANE:(i + 1) * LANE] = piece.astype(BF16)
    vt_ref[...] = _dot_nt(wvt_ref[...], xn).astype(BF16)


def _gla_inproj_kernel(*refs, kd, vd, hk, n_cast):
    *refs, w_ref = refs
    (x_ref, g_ref, w32_ref, wglr_ref, wgk2_ref, bgk_ref), (o_ref, el_ref) = _run_casts(refs, 6, n_cast)

    @pl.when(pl.program_id(0) == 0)
    def _():
        w_ref[...] = w32_ref[:w_ref.shape[0], :].astype(BF16)

    tm = x_ref.shape[0]
    c = GLA_CHUNK
    xn = _rms(x_ref[...], g_ref[...]).astype(BF16)
    glr = _dot_nt(xn, wglr_ref[...])
    z = _dot(glr.astype(BF16), wgk2_ref[...]) + bgk_ref[...]
    for j in range(0, vd, COL_CHUNK):
        o_ref[:, j:j + COL_CHUNK] = _dot_nt(
            xn, w_ref[2 * kd + j:2 * kd + j + COL_CHUNK, :]).astype(BF16)
    for j in range(vd, 2 * vd, COL_CHUNK):
        gt = _dot_nt(xn, w_ref[2 * kd + j:2 * kd + j + COL_CHUNK, :])
        o_ref[:, j:j + COL_CHUNK] = (gt / (1.0 + jnp.exp(-gt))).astype(BF16)
    gk = (jnp.minimum(z, 0.0) - jnp.log(1.0 + jnp.exp(-jnp.abs(z)))) * (LOG2E / GLA_GATE_NORM)
    groups = c // SUBLANES
    g4 = gk.reshape(tm // c, groups, SUBLANES, kd)
    sub = lax.broadcasted_iota(jnp.int32, g4.shape, 2)
    shift = 1
    while shift < SUBLANES:
        g4 = g4 + jnp.where(sub >= shift, pltpu.roll(g4, shift, axis=2), 0.0)
        shift *= 2
    pieces, carry = [], None
    for g in range(groups):
        piece = g4[:, g] if carry is None else g4[:, g] + carry
        pieces.append(piece)
        carry = piece[:, SUBLANES - 1:, :]
    bcum = jnp.stack(pieces, axis=1).reshape(tm, kd)
    q = _dot_nt(xn, w_ref[:kd, :])
    o_ref[:, 2 * vd:2 * vd + kd] = ((q * (hk ** -0.5)) * jnp.exp2(bcum)).astype(BF16)
    k = _dot_nt(xn, w_ref[kd:2 * kd, :])
    o_ref[:, 2 * vd + kd:2 * vd + 2 * kd] = (k * jnp.exp2(-bcum)).astype(BF16)
    for i in range(tm // c):
        el_ref[i:i + 1, :] = jnp.exp2(bcum[(i + 1) * c - 1:(i + 1) * c, :])


def _gla_inproj(x, layer, mix_layer, g_mix, w_in_t, w_glr, w_gk2, b_gk, kd, vd, casts):
    n, d = x.shape
    n_out = 2 * vd + 2 * kd
    per_tile = ROW_TILE // GLA_CHUNK
    proj, el, *cast_out = pl.pallas_call(
        functools.partial(_gla_inproj_kernel, kd=kd, vd=vd, hk=kd // GLA_HEADS, n_cast=len(casts)),
        grid=(n // ROW_TILE,),
        in_specs=[pl.BlockSpec((ROW_TILE, d), lambda i: (i, 0)),
                  _layer(g_mix, mix_layer), _layer(w_in_t, layer), _layer(w_glr, layer),
                  _layer(w_gk2, layer), _layer(b_gk, layer)]
                 + [c.in_spec for c in casts],
        out_specs=[pl.BlockSpec((ROW_TILE, n_out), lambda i: (i, 0)),
                   pl.BlockSpec((per_tile, kd), lambda i: (i, 0))] + [c.out_spec for c in casts],
        out_shape=[jax.ShapeDtypeStruct((n, n_out), BF16),
                   jax.ShapeDtypeStruct((n // GLA_CHUNK, kd), F32)] + [c.out_shape for c in casts],
        scratch_shapes=[pltpu.VMEM((2 * kd + 2 * vd, d), BF16)],
        compiler_params=_params("arbitrary"),
        name="gla_inproj",
    )(x, g_mix, w_in_t, w_glr, w_gk2, b_gk, *[c.stack for c in casts])
    return proj, el, cast_out


def _gla_kernel(v_ref, sg_ref, qt_ref, kt_ref, el_ref, gon_ref, o_ref, st_ref, *, hk, hv):
    @pl.when(pl.program_id(1) == 0)
    def _():
        st_ref[...] = jnp.zeros_like(st_ref)

    c = GLA_CHUNK
    hc = GLA_HEADS * c
    ri = lax.broadcasted_iota(jnp.int32, (hc, hc), 0)
    ci = lax.broadcasted_iota(jnp.int32, (hc, hc), 1)
    causal = (ri >= ci) & (ri // c == ci // c)
    gon = gon_ref[...]
    for i in range(v_ref.shape[0] // c):
        rows = slice(i * c, (i + 1) * c)
        el = el_ref[i:i + 1, :]
        heads_k = [slice(h * hk, (h + 1) * hk) for h in range(GLA_HEADS)]
        heads_v = [slice(h * hv, (h + 1) * hv) for h in range(GLA_HEADS)]
        q_stack = jnp.concatenate([qt_ref[rows, ks] for ks in heads_k], axis=0)
        k_stack = jnp.concatenate([kt_ref[rows, ks] for ks in heads_k], axis=0)
        v_stack = jnp.concatenate([v_ref[rows, vs] for vs in heads_v], axis=0)
        attn = jnp.where(causal, _dot_nt(q_stack, k_stack), 0.0).astype(BF16)
        o_intra = _dot(attn, v_stack)
        for h in range(GLA_HEADS):
            ks, vs = heads_k[h], heads_v[h]
            st = st_ref[h]
            o = o_intra[h * c:(h + 1) * c] + _dot_nt(qt_ref[rows, ks], st.astype(BF16))
            kdec = (kt_ref[rows, ks].astype(F32) * el[:, ks]).astype(BF16)
            st_ref[h] = st * el[:, ks] + _dot_tn(v_ref[rows, vs], kdec)
            o = o * lax.rsqrt(jnp.mean(o * o, axis=-1, keepdims=True) + EPS) * gon
            o_ref[rows, vs] = (o * sg_ref[rows, vs].astype(F32)).astype(BF16)


def _gla_core(proj, el, gon, batch, seq, kd, vd):
    n = proj.shape[0]
    hk = kd // GLA_HEADS
    hv = vd // GLA_HEADS
    proj3 = proj.reshape(batch, seq, proj.shape[1])
    el3 = el.reshape(batch, seq // GLA_CHUNK, kd)
    kblk = 2 * vd // kd
    out = pl.pallas_call(
        functools.partial(_gla_kernel, hk=hk, hv=hv),
        grid=(batch, seq // GLA_ROWS),
        in_specs=[pl.BlockSpec((None, GLA_ROWS, vd), lambda b, t: (b, t, 0)),
                  pl.BlockSpec((None, GLA_ROWS, vd), lambda b, t: (b, t, 1)),
                  pl.BlockSpec((None, GLA_ROWS, kd), lambda b, t: (b, t, kblk)),
                  pl.BlockSpec((None, GLA_ROWS, kd), lambda b, t: (b, t, kblk + 1)),
                  pl.BlockSpec((None, GLA_ROWS // GLA_CHUNK, kd), lambda b, t: (b, t, 0)),
                  pl.BlockSpec((1, hv), lambda b, t: (0, 0))],
        out_specs=pl.BlockSpec((None, GLA_ROWS, vd), lambda b, t: (b, t, 0)),
        out_shape=jax.ShapeDtypeStruct((batch, seq, vd), BF16),
        scratch_shapes=[pltpu.VMEM((GLA_HEADS, hv, hk), F32)],
        compiler_params=_params("parallel", "arbitrary"),
        name="gla_core",
    )(proj3, proj3, proj3, proj3, el3, gon)
    return out.reshape(n, vd)


_POST_TAIL_INPUTS = {"none": 0, "final": 1, "q": 2, "kvq": 5}


def _post_kernel(*refs, tail, n_cast):
    *refs, wup_ref, wdn_ref, sem = refs
    ins, outs = _run_casts(refs, 6 + _POST_TAIL_INPUTS[tail], n_cast)
    h_ref, o_ref, wo_ref, g_ref, wup_hbm, wdn_hbm = ins[:6]
    rest = ins[6:] + outs
    first = pl.program_id(0) == 0
    pl.when(first)(functools.partial(_post_tile, h_ref, o_ref, wo_ref, g_ref, wup_hbm, wdn_hbm,
                                     wup_ref, wdn_ref, sem, rest, tail, True))
    pl.when(jnp.logical_not(first))(functools.partial(
        _post_tile, h_ref, o_ref, wo_ref, g_ref, wup_hbm, wdn_hbm, wup_ref, wdn_ref, sem, rest, tail,
        False))


def _post_tile(h_ref, o_ref, wo_ref, g_ref, wup_hbm, wdn_hbm, wup_ref, wdn_ref, sem, rest, tail, fetch):
    d_ff = wup_ref.shape[1]
    chunks = [slice(j, j + COL_CHUNK) for j in range(0, d_ff, COL_CHUNK)]
    if fetch:
        copies = [(pltpu.make_async_copy(wup_hbm.at[:, cols], wup_ref.at[:, cols], sem.at[0, c]),
                   pltpu.make_async_copy(wdn_hbm.at[cols, :], wdn_ref.at[cols, :], sem.at[1, c]))
                  for c, cols in enumerate(chunks)]
        for up, dn in copies:
            up.start()
            dn.start()
    h1 = h_ref[...] + _dot(o_ref[...], wo_ref[...])
    hn = _rms(h1, g_ref[...]).astype(BF16)
    acc = h1
    for c, cols in enumerate(chunks):
        if fetch:
            copies[c][0].wait()
            copies[c][1].wait()
        u = jnp.maximum(_dot(hn, wup_ref[:, cols]), 0.0)
        acc = acc + _dot((u * u).astype(BF16), wdn_ref[cols, :])
    if tail == "none":
        (out_ref,) = rest
    elif tail == "final":
        gfin_ref, out_ref = rest
        acc = _rms(acc, gfin_ref[...])
    else:
        if tail == "kvq":
            gq_ref, wq_ref, gkv_ref, wk_ref, wvt_ref, out_ref, q_ref, kp_ref, vt_ref = rest
        else:
            gq_ref, wq_ref, out_ref, q_ref = rest
        xhat = acc * lax.rsqrt(jnp.mean(acc * acc, axis=-1, keepdims=True) + EPS)
        _store_q((xhat * gq_ref[...]).astype(BF16), wq_ref, q_ref)
        if tail == "kvq":
            _store_kv((xhat * gkv_ref[...]).astype(BF16), wk_ref, wvt_ref, kp_ref, vt_ref)
    out_ref[...] = acc


def _post(h, o, params, tail, casts, n_q_cols=0, n_vt_rows=0):
    n, d = h.shape
    assert len(params) == 4 + _POST_TAIL_INPUTS[tail]
    w_up, w_down = params[2][0], params[3][0]
    rows = lambda width: pl.BlockSpec((ROW_TILE, width), lambda i: (i, 0))
    out_specs = [rows(d)]
    out_shape = [jax.ShapeDtypeStruct((n, d), F32)]
    if tail in ("q", "kvq"):
        out_specs.append(rows(n_q_cols))
        out_shape.append(jax.ShapeDtypeStruct((n, n_q_cols), BF16))
    if tail == "kvq":
        out_specs += [rows(4 * LANE), pl.BlockSpec((n_vt_rows, ROW_TILE), lambda i: (0, i))]
        out_shape += [jax.ShapeDtypeStruct((n, 4 * LANE), BF16),
                      jax.ShapeDtypeStruct((n_vt_rows, n), BF16)]
    n_main = len(out_specs)
    outs = pl.pallas_call(
        functools.partial(_post_kernel, tail=tail, n_cast=len(casts)),
        grid=(n // ROW_TILE,),
        in_specs=[rows(d), rows(o.shape[1])] + [spec for _, spec in params]
                 + [c.in_spec for c in casts],
        out_specs=out_specs + [c.out_spec for c in casts],
        out_shape=out_shape + [c.out_shape for c in casts],
        scratch_shapes=[pltpu.VMEM(w_up.shape, BF16), pltpu.VMEM(w_down.shape, BF16),
                        pltpu.SemaphoreType.DMA((2, w_up.shape[1] // COL_CHUNK))],
        compiler_params=_params("arbitrary"),
        name="post_mlp",
    )(h, o, *[arr for arr, _ in params], *[c.stack for c in casts])
    return outs[:n_main], outs[n_main:]


def _rel_buckets_merged():
    r = np.arange(BLOCK)[:, None]
    q = np.arange(BLOCK)[None, :]
    n = np.where(r > q, q + BLOCK - r, q - r)
    max_exact = REL_BUCKETS // 2
    large = max_exact + (np.log(np.maximum(n, 1) / max_exact)
                         / np.log(REL_MAX_DIST / max_exact)
                         * (REL_BUCKETS - max_exact)).astype(np.int32)
    large = np.minimum(large, REL_BUCKETS - 1)
    return np.where(n < max_exact, n, large).astype(np.int32)


def _swa_kernel(rel_ref, sink_ref, bucket_ref, q_ref, kpp_ref, kpc_ref, vtp_ref, vtc_ref, o_ref,
                bias_ref, *, n_kv):
    n_q = bias_ref.shape[0]
    hd = SWA_HEAD_DIM
    pairs_per_kv = SWA_GROUP // 2
    neg2 = NEG * LOG2E

    @pl.when((pl.program_id(0) == 0) & (pl.program_id(1) == 0))
    def _():
        bucket = bucket_ref[...]

        def per_head(h, carry):
            def per_bucket(b, acc):
                return jnp.where(bucket == b, rel_ref[b, h] * LOG2E, acc)
            bias_ref[h] = lax.fori_loop(0, REL_BUCKETS, per_bucket,
                                        jnp.zeros(bucket.shape, F32))
            return carry

        lax.fori_loop(0, n_q, per_head, 0)

    key_r = lax.broadcasted_iota(jnp.int32, (BLOCK, BLOCK), 0)
    qry = lax.broadcasted_iota(jnp.int32, (BLOCK, BLOCK), 1)
    lower = key_r > qry
    seq_start = pl.program_id(1) == 0
    zeros = jnp.zeros((hd, 2 * BLOCK), BF16)
    for qb in range(SWA_QBLOCKS):
        rows = slice(qb * BLOCK, (qb + 1) * BLOCK)
        prev_rows = slice((qb - 1) * BLOCK, qb * BLOCK)
        for hkv in range(n_kv):
            def k_band(col):
                cols = slice(col * LANE, (col + 1) * LANE)
                prev = kpp_ref[:, cols] if qb == 0 else kpc_ref[prev_rows, cols]
                return [prev, kpc_ref[rows, cols]]
            k_stack = jnp.concatenate(k_band(2 * hkv) + k_band(2 * hkv + 1), axis=0)
            vrow = slice(hkv * hd, (hkv + 1) * hd)
            if qb == 0:
                vt_prev = jnp.where(seq_start, jnp.zeros((hd, BLOCK), BF16), vtp_ref[vrow, :])
                vt_band = jnp.concatenate([vt_prev, vtc_ref[vrow, :BLOCK]], axis=1)
            else:
                vt_band = vtc_ref[vrow, (qb - 1) * BLOCK:(qb + 1) * BLOCK]
            v_bd = jnp.concatenate([jnp.concatenate([vt_band, zeros], axis=1),
                                    jnp.concatenate([zeros, vt_band], axis=1)], axis=0)
            for pr in range(pairs_per_kv):
                pair = hkv * pairs_per_kv + pr
                cols = slice(pair * LANE, (pair + 1) * LANE)
                st = _dot_nt(k_stack, q_ref[rows, cols])
                ps, rden = [], []
                for e in range(2):
                    hq = 2 * pair + e
                    base = e * 2 * BLOCK
                    s = jnp.where(lower, st[base:base + BLOCK], st[base + BLOCK:base + 2 * BLOCK])
                    s = s + bias_ref[hq]
                    if qb == 0:
                        s = jnp.where(seq_start & lower, neg2, s)
                    sink = sink_ref[hq] * LOG2E
                    m = jnp.maximum(jnp.max(s, axis=0, keepdims=True), jnp.maximum(sink, neg2))
                    p = jnp.exp2(s - m)
                    pm = jnp.exp2(neg2 - m)
                    den = jnp.sum(p, axis=0, keepdims=True) + BLOCK * pm + jnp.exp2(sink - m)
                    pm_tile = jnp.broadcast_to(pm, p.shape)
                    ps += [jnp.where(lower, p, pm_tile).astype(BF16),
                           jnp.where(lower, pm_tile, p).astype(BF16)]
                    rden.append(jnp.broadcast_to(1.0 / den, (hd, BLOCK)))
                ot = _dot(v_bd, jnp.concatenate(ps, axis=0)) * jnp.concatenate(rden, axis=0)
                o_ref[rows, cols] = ot.T.astype(BF16)


def _swa_core(q, kpad, vt, rel_table, sinks, batch, seq):
    n, dq = q.shape
    n_q = dq // SWA_HEAD_DIM
    n_kv = n_q // SWA_GROUP
    rows = SWA_QBLOCKS * BLOCK
    steps = seq // rows
    q3 = q.reshape(batch, seq, dq)
    kp3 = kpad.reshape(batch, seq, kpad.shape[1])
    assert WINDOW == BLOCK
    bucket = jnp.asarray(_rel_buckets_merged())
    smem = pl.BlockSpec(memory_space=pltpu.SMEM)
    prev = lambda t: jnp.maximum(SWA_QBLOCKS * t - 1, 0)
    out = pl.pallas_call(
        functools.partial(_swa_kernel, n_kv=n_kv),
        grid=(batch, steps),
        in_specs=[smem, smem,
                  pl.BlockSpec(bucket.shape, lambda b, t: (0, 0)),
                  pl.BlockSpec((None, rows, dq), lambda b, t: (b, t, 0)),
                  pl.BlockSpec((None, BLOCK, kpad.shape[1]), lambda b, t: (b, prev(t), 0)),
                  pl.BlockSpec((None, rows, kpad.shape[1]), lambda b, t: (b, t, 0)),
                  pl.BlockSpec((vt.shape[0], BLOCK),
                               lambda b, t: (0, b * steps * SWA_QBLOCKS + prev(t))),
                  pl.BlockSpec((vt.shape[0], rows), lambda b, t: (0, b * steps + t))],
        out_specs=pl.BlockSpec((None, rows, dq), lambda b, t: (b, t, 0)),
        out_shape=jax.ShapeDtypeStruct((batch, seq, dq), BF16),
        scratch_shapes=[pltpu.VMEM((n_q,) + bucket.shape, F32)],
        compiler_params=_params("arbitrary", "arbitrary"),
        name="swa_core",
    )(rel_table, sinks, bucket, q3, kp3, kp3, vt, vt)
    return out.reshape(n, dq)


def kernel(x, a_w_in, a_w_gk2, a_b_gk, a_onorm, a_w_out, kv_norm, w_kv, b_w_q, b_sinks, b_w_out,
           rel_table, ln_mix, ln_mlp, w_up, w_down, ln_final):
    batch, seq, d = x.shape
    n_a = a_w_in.shape[0]
    n_b = b_w_q.shape[0]
    kd = a_w_gk2.shape[2]
    vd = (a_w_in.shape[2] - GLA_GATE_RANK - 2 * kd) // 2

    depth = n_a + n_b
    assert n_a >= 1
    half = w_kv.shape[1] // 2
    assert half == LANE and 2 * SWA_HEAD_DIM == LANE

    w_in_t = jnp.swapaxes(a_w_in, 1, 2)
    n_main = 2 * kd + 2 * vd

    w_glr = jnp.pad(w_in_t[:, n_main:, :], ((0, 0), (0, LANE - GLA_GATE_RANK), (0, 0))).astype(BF16)
    w_gk2 = jnp.pad(a_w_gk2, ((0, 0), (0, LANE - GLA_GATE_RANK), (0, 0))).astype(BF16)
    w_vt = w_kv[:, half:].T.astype(BF16)
    g_mix = ln_mix.reshape(depth, 1, d)
    g_mlp = ln_mlp.reshape(depth, 1, d)
    g_kv = kv_norm.reshape(1, d)
    g_final = ln_final.reshape(1, d)
    b_gk = a_b_gk.reshape(n_a, 1, kd)

    calls = []
    for layer in range(depth):
        is_gla = layer < n_a
        j = layer if is_gla else layer - n_a
        if is_gla:
            calls.append(("inproj", layer, {}))
        needs = {"w_out": (a_w_out if is_gla else b_w_out, j, d, d),
                 "w_up": (w_up, layer, d, w_up.shape[2]), "w_down": (w_down, layer, w_down.shape[1], d)}
        if n_a <= layer + 1 < depth:
            needs["w_q"] = (b_w_q, layer + 1 - n_a, d, b_w_q.shape[2])
        if layer + 1 == n_a:
            needs["w_k"] = (w_kv[None], 0, d, half)
        calls.append(("post", layer, needs))
    steps = batch * seq // ROW_TILE
    bf = {name: stack[idx, :rows, :cols].astype(BF16)
          for name, (stack, idx, rows, cols) in calls[0][2].items()}

    h = x.reshape(batch * seq, d)
    for c, (kind, layer, _) in enumerate(calls):
        is_gla = layer < n_a
        j = layer if is_gla else layer - n_a
        nxt = calls[c + 1][2] if c + 1 < len(calls) else {}
        casts = [_Cast(stack, idx, rows, cols, steps) for stack, idx, rows, cols in nxt.values()]
        if kind == "inproj":
            proj, el, cast_out = _gla_inproj(h, j, layer, g_mix, w_in_t, w_glr, w_gk2, b_gk,
                                             kd, vd, casts)
            o = _gla_core(proj, el, a_onorm[j].reshape(1, -1), batch, seq, kd, vd)
        else:
            if not is_gla:
                o = _swa_core(q, kpad, vt, rel_table, b_sinks[j], batch, seq)
            params = [(bf["w_out"], _resident(bf["w_out"].shape)), (g_mlp, _layer(g_mlp, layer)),
                      (bf["w_up"], pl.BlockSpec(memory_space=pl.ANY)),
                      (bf["w_down"], pl.BlockSpec(memory_space=pl.ANY))]
            if layer == depth - 1:
                (h,), cast_out = _post(h, o, params + [(g_final, _resident(g_final.shape))],
                                       "final", casts)
            elif layer + 1 < n_a:
                (h,), cast_out = _post(h, o, params, "none", casts)
            else:
                params += [(g_mix, _layer(g_mix, layer + 1)), (bf["w_q"], _resident(bf["w_q"].shape))]
                if layer + 1 == n_a:
                    params += [(g_kv, _resident(g_kv.shape)), (bf["w_k"], _resident(bf["w_k"].shape)),
                               (w_vt, _resident(w_vt.shape))]
                    (h, q, kpad, vt), cast_out = _post(h, o, params, "kvq", casts,
                                                       bf["w_q"].shape[1], w_vt.shape[0])
                else:
                    (h, q), cast_out = _post(h, o, params, "q", casts, bf["w_q"].shape[1])
        bf = dict(zip(nxt.keys(), cast_out))
    return h.reshape(batch, seq, d)
```

```python
import functools
import math

import numpy as np
import jax
import jax.numpy as jnp
from jax import lax
from jax.experimental import pallas as pl
from jax.experimental.pallas import tpu as pltpu

F32 = jnp.float32
BF16 = jnp.bfloat16

EPS = 1e-6
NEG = -1e30
LOG2E = math.log2(math.e)

GLA_HEADS = 4
GLA_GATE_RANK = 16
GLA_GATE_NORM = 16.0
GLA_CHUNK = 64

SWA_HEAD_DIM = 64
SWA_GROUP = 8
WINDOW = 128
BLOCK = 128
REL_BUCKETS = 32
REL_MAX_DIST = 128

LANE = 128
SUBLANES = 8
ROW_TILE = 512
COL_CHUNK = 512
GLA_ROWS = 2048
SWA_QBLOCKS = 16
VMEM_LIMIT = 56 * 1024 * 1024

_NT = (((1,), (1,)), ((), ()))
_TN = (((0,), (0,)), ((), ()))


def _rms(x, g):
    ms = jnp.mean(x * x, axis=-1, keepdims=True)
    return x * lax.rsqrt(ms + EPS) * g


def _dot(a, b):
    return jnp.dot(a, b, preferred_element_type=F32)


def _dot_nt(a, b):
    return lax.dot_general(a, b, _NT, preferred_element_type=F32)


def _dot_tn(a, b):
    return lax.dot_general(a, b, _TN, preferred_element_type=F32)


def _params(*sem):
    return pltpu.CompilerParams(dimension_semantics=sem, vmem_limit_bytes=VMEM_LIMIT)


def _resident(shape):
    zeros = (0,) * len(shape)
    return pl.BlockSpec(shape, lambda *_: zeros, pipeline_mode=pl.Buffered(1))


def _layer(stacked, layer):
    idx = (layer,) + (0,) * (stacked.ndim - 1)
    return pl.BlockSpec((None,) + stacked.shape[1:], lambda *_: idx, pipeline_mode=pl.Buffered(1))


class _Cast:
    def __init__(self, stack, layer, rows, cols, steps):
        assert rows % steps == 0 and cols % LANE == 0
        block = (rows // steps, cols)
        self.stack = stack
        self.in_spec = pl.BlockSpec((None,) + block, lambda i: (layer, i, 0))
        self.out_spec = pl.BlockSpec(block, lambda i: (i, 0))
        self.out_shape = jax.ShapeDtypeStruct((rows, cols), BF16)


def _run_casts(refs, n_in, n_cast):
    n_out = len(refs) - n_in - 2 * n_cast
    for src, dst in zip(refs[n_in:n_in + n_cast], refs[n_in + n_cast + n_out:]):
        dst[...] = src[...].astype(BF16)
    return refs[:n_in], refs[n_in + n_cast:n_in + n_cast + n_out]


def _store_q(xn, wq_ref, q_ref):
    scale = SWA_HEAD_DIM ** -0.5 * LOG2E
    for j in range(0, q_ref.shape[-1], COL_CHUNK):
        q_ref[:, j:j + COL_CHUNK] = (_dot(xn, wq_ref[:, j:j + COL_CHUNK]) * scale).astype(BF16)


def _store_kv(xn, wk_ref, wvt_ref, kp_ref, vt_ref):
    k = _dot(xn, wk_ref[...])
    kr = pltpu.roll(k, SWA_HEAD_DIM, axis=1)
    low = lax.broadcasted_iota(jnp.int32, k.shape, 1) < SWA_HEAD_DIM
    pieces = (jnp.where(low, k, 0.0), jnp.where(low, 0.0, kr),
              jnp.where(low, kr, 0.0), jnp.where(low, 0.0, k))
    for i, piece in enumerate(pieces):
        kp_ref[:, i * LANE:(i + 1) * LANE] = piece.astype(BF16)
    vt_ref[...] = _dot_nt(wvt_ref[...], xn).astype(BF16)


def _gla_inproj_kernel(*refs, kd, vd, hk, n_cast):
    *refs, w_ref = refs
    (x_ref, g_ref, w32_ref, wglr_ref, wgk2_ref, bgk_ref), (o_ref, el_ref) = _run_casts(refs, 6, n_cast)

    @pl.when(pl.program_id(0) == 0)
    def _():
        w_ref[...] = w32_ref[:w_ref.shape[0], :].astype(BF16)

    tm = x_ref.shape[0]
    c = GLA_CHUNK
    xn = _rms(x_ref[...], g_ref[...]).astype(BF16)
    glr = _dot_nt(xn, wglr_ref[...])
    z = _dot(glr.astype(BF16), wgk2_ref[...]) + bgk_ref[...]
    for j in range(0, vd, COL_CHUNK):
        o_ref[:, j:j + COL_CHUNK] = _dot_nt(
            xn, w_ref[2 * kd + j:2 * kd + j + COL_CHUNK, :]).astype(BF16)
    for j in range(vd, 2 * vd, COL_CHUNK):
        gt = _dot_nt(xn, w_ref[2 * kd + j:2 * kd + j + COL_CHUNK, :])
        o_ref[:, j:j + COL_CHUNK] = (gt / (1.0 + jnp.exp(-gt))).astype(BF16)
    gk = (jnp.minimum(z, 0.0) - jnp.log(1.0 + jnp.exp(-jnp.abs(z)))) * (LOG2E / GLA_GATE_NORM)
    groups = c // SUBLANES
    g4 = gk.reshape(tm // c, groups, SUBLANES, kd)
    sub = lax.broadcasted_iota(jnp.int32, g4.shape, 2)
    shift = 1
    while shift < SUBLANES:
        g4 = g4 + jnp.where(sub >= shift, pltpu.roll(g4, shift, axis=2), 0.0)
        shift *= 2
    pieces, carry = [], None
    for g in range(groups):
        piece = g4[:, g] if carry is None else g4[:, g] + carry
        pieces.append(piece)
        carry = piece[:, SUBLANES - 1:, :]
    bcum = jnp.stack(pieces, axis=1).reshape(tm, kd)
    q = _dot_nt(xn, w_ref[:kd, :])
    o_ref[:, 2 * vd:2 * vd + kd] = ((q * (hk ** -0.5)) * jnp.exp2(bcum)).astype(BF16)
    k = _dot_nt(xn, w_ref[kd:2 * kd, :])
    o_ref[:, 2 * vd + kd:2 * vd + 2 * kd] = (k * jnp.exp2(-bcum)).astype(BF16)
    for i in range(tm // c):
        el_ref[i:i + 1, :] = jnp.exp2(bcum[(i + 1) * c - 1:(i + 1) * c, :])


def _gla_inproj(x, layer, mix_layer, g_mix, w_in_t, w_glr, w_gk2, b_gk, kd, vd, casts):
    n, d = x.shape
    n_out = 2 * vd + 2 * kd
    per_tile = ROW_TILE // GLA_CHUNK
    proj, el, *cast_out = pl.pallas_call(
        functools.partial(_gla_inproj_kernel, kd=kd, vd=vd, hk=kd // GLA_HEADS, n_cast=len(casts)),
        grid=(n // ROW_TILE,),
        in_specs=[pl.BlockSpec((ROW_TILE, d), lambda i: (i, 0)),
                  _layer(g_mix, mix_layer), _layer(w_in_t, layer), _layer(w_glr, layer),
                  _layer(w_gk2, layer), _layer(b_gk, layer)]
                 + [c.in_spec for c in casts],
        out_specs=[pl.BlockSpec((ROW_TILE, n_out), lambda i: (i, 0)),
                   pl.BlockSpec((per_tile, kd), lambda i: (i, 0))] + [c.out_spec for c in casts],
        out_shape=[jax.ShapeDtypeStruct((n, n_out), BF16),
                   jax.ShapeDtypeStruct((n // GLA_CHUNK, kd), F32)] + [c.out_shape for c in casts],
        scratch_shapes=[pltpu.VMEM((2 * kd + 2 * vd, d), BF16)],
        compiler_params=_params("arbitrary"),
        name="gla_inproj",
    )(x, g_mix, w_in_t, w_glr, w_gk2, b_gk, *[c.stack for c in casts])
    return proj, el, cast_out


def _gla_kernel(v_ref, sg_ref, qt_ref, kt_ref, el_ref, gon_ref, o_ref, st_ref, *, hk, hv):
    @pl.when(pl.program_id(1) == 0)
    def _():
        st_ref[...] = jnp.zeros_like(st_ref)

    c = GLA_CHUNK
    hc = GLA_HEADS * c
    ri = lax.broadcasted_iota(jnp.int32, (hc, hc), 0)
    ci = lax.broadcasted_iota(jnp.int32, (hc, hc), 1)
    causal = (ri >= ci) & (ri // c == ci // c)
    gon = gon_ref[...]
    for i in range(v_ref.shape[0] // c):
        rows = slice(i * c, (i + 1) * c)
        el = el_ref[i:i + 1, :]
        heads_k = [slice(h * hk, (h + 1) * hk) for h in range(GLA_HEADS)]
        heads_v = [slice(h * hv, (h + 1) * hv) for h in range(GLA_HEADS)]
        q_stack = jnp.concatenate([qt_ref[rows, ks] for ks in heads_k], axis=0)
        k_stack = jnp.concatenate([kt_ref[rows, ks] for ks in heads_k], axis=0)
        v_stack = jnp.concatenate([v_ref[rows, vs] for vs in heads_v], axis=0)
        attn = jnp.where(causal, _dot_nt(q_stack, k_stack), 0.0).astype(BF16)
        o_intra = _dot(attn, v_stack)
        kdec = (kt_ref[rows, :].astype(F32) * el).astype(BF16)
        for h in range(GLA_HEADS):
            ks, vs = heads_k[h], heads_v[h]
            st = st_ref[h]
            o = o_intra[h * c:(h + 1) * c] + _dot_nt(qt_ref[rows, ks], st.astype(BF16))
            st_ref[h] = st * el[:, ks] + _dot_tn(v_ref[rows, vs], kdec[:, ks])
            o = o * lax.rsqrt(jnp.mean(o * o, axis=-1, keepdims=True) + EPS) * gon
            o_ref[rows, vs] = (o * sg_ref[rows, vs].astype(F32)).astype(BF16)


def _gla_core(proj, el, gon, batch, seq, kd, vd):
    n = proj.shape[0]
    hk = kd // GLA_HEADS
    hv = vd // GLA_HEADS
    proj3 = proj.reshape(batch, seq, proj.shape[1])
    el3 = el.reshape(batch, seq // GLA_CHUNK, kd)
    kblk = 2 * vd // kd
    out = pl.pallas_call(
        functools.partial(_gla_kernel, hk=hk, hv=hv),
        grid=(batch, seq // GLA_ROWS),
        in_specs=[pl.BlockSpec((None, GLA_ROWS, vd), lambda b, t: (b, t, 0)),
                  pl.BlockSpec((None, GLA_ROWS, vd), lambda b, t: (b, t, 1)),
                  pl.BlockSpec((None, GLA_ROWS, kd), lambda b, t: (b, t, kblk)),
                  pl.BlockSpec((None, GLA_ROWS, kd), lambda b, t: (b, t, kblk + 1)),
                  pl.BlockSpec((None, GLA_ROWS // GLA_CHUNK, kd), lambda b, t: (b, t, 0)),
                  pl.BlockSpec((1, hv), lambda b, t: (0, 0))],
        out_specs=pl.BlockSpec((None, GLA_ROWS, vd), lambda b, t: (b, t, 0)),
        out_shape=jax.ShapeDtypeStruct((batch, seq, vd), BF16),
        scratch_shapes=[pltpu.VMEM((GLA_HEADS, hv, hk), F32)],
        compiler_params=_params("parallel", "arbitrary"),
        name="gla_core",
    )(proj3, proj3, proj3, proj3, el3, gon)
    return out.reshape(n, vd)


_POST_TAIL_INPUTS = {"none": 0, "final": 1, "q": 2, "kvq": 5}


def _post_kernel(*refs, tail, n_cast):
    *refs, wup_ref, wdn_ref, sem = refs
    ins, outs = _run_casts(refs, 6 + _POST_TAIL_INPUTS[tail], n_cast)
    h_ref, o_ref, wo_ref, g_ref, wup_hbm, wdn_hbm = ins[:6]
    rest = ins[6:] + outs
    first = pl.program_id(0) == 0
    pl.when(first)(functools.partial(_post_tile, h_ref, o_ref, wo_ref, g_ref, wup_hbm, wdn_hbm,
                                     wup_ref, wdn_ref, sem, rest, tail, True))
    pl.when(jnp.logical_not(first))(functools.partial(
        _post_tile, h_ref, o_ref, wo_ref, g_ref, wup_hbm, wdn_hbm, wup_ref, wdn_ref, sem, rest, tail,
        False))


def _post_tile(h_ref, o_ref, wo_ref, g_ref, wup_hbm, wdn_hbm, wup_ref, wdn_ref, sem, rest, tail, fetch):
    d_ff = wup_ref.shape[1]
    chunks = [slice(j, j + COL_CHUNK) for j in range(0, d_ff, COL_CHUNK)]
    if fetch:
        copies = [(pltpu.make_async_copy(wup_hbm.at[:, cols], wup_ref.at[:, cols], sem.at[0, c]),
                   pltpu.make_async_copy(wdn_hbm.at[cols, :], wdn_ref.at[cols, :], sem.at[1, c]))
                  for c, cols in enumerate(chunks)]
        for up, dn in copies:
            up.start()
            dn.start()
    h1 = h_ref[...] + _dot(o_ref[...], wo_ref[...])
    hn = _rms(h1, g_ref[...]).astype(BF16)
    acc = h1
    for c, cols in enumerate(chunks):
        if fetch:
            copies[c][0].wait()
            copies[c][1].wait()
        u = jnp.maximum(_dot(hn, wup_ref[:, cols]), 0.0)
        acc = acc + _dot((u * u).astype(BF16), wdn_ref[cols, :])
    if tail == "none":
        (out_ref,) = rest
    elif tail == "final":
        gfin_ref, out_ref = rest
        acc = _rms(acc, gfin_ref[...])
    else:
        if tail == "kvq":
            gq_ref, wq_ref, gkv_ref, wk_ref, wvt_ref, out_ref, q_ref, kp_ref, vt_ref = rest
        else:
            gq_ref, wq_ref, out_ref, q_ref = rest
        xhat = acc * lax.rsqrt(jnp.mean(acc * acc, axis=-1, keepdims=True) + EPS)
        _store_q((xhat * gq_ref[...]).astype(BF16), wq_ref, q_ref)
        if tail == "kvq":
            _store_kv((xhat * gkv_ref[...]).astype(BF16), wk_ref, wvt_ref, kp_ref, vt_ref)
    out_ref[...] = acc


def _post(h, o, params, tail, casts, n_q_cols=0, n_vt_rows=0):
    n, d = h.shape
    assert len(params) == 4 + _POST_TAIL_INPUTS[tail]
    w_up, w_down = params[2][0], params[3][0]
    rows = lambda width: pl.BlockSpec((ROW_TILE, width), lambda i: (i, 0))
    out_specs = [rows(d)]
    out_shape = [jax.ShapeDtypeStruct((n, d), F32)]
    if tail in ("q", "kvq"):
        out_specs.append(rows(n_q_cols))
        out_shape.append(jax.ShapeDtypeStruct((n, n_q_cols), BF16))
    if tail == "kvq":
        out_specs += [rows(4 * LANE), pl.BlockSpec((n_vt_rows, ROW_TILE), lambda i: (0, i))]
        out_shape += [jax.ShapeDtypeStruct((n, 4 * LANE), BF16),
                      jax.ShapeDtypeStruct((n_vt_rows, n), BF16)]
    n_main = len(out_specs)
    outs = pl.pallas_call(
        functools.partial(_post_kernel, tail=tail, n_cast=len(casts)),
        grid=(n // ROW_TILE,),
        in_specs=[rows(d), rows(o.shape[1])] + [spec for _, spec in params]
                 + [c.in_spec for c in casts],
        out_specs=out_specs + [c.out_spec for c in casts],
        out_shape=out_shape + [c.out_shape for c in casts],
        scratch_shapes=[pltpu.VMEM(w_up.shape, BF16), pltpu.VMEM(w_down.shape, BF16),
                        pltpu.SemaphoreType.DMA((2, w_up.shape[1] // COL_CHUNK))],
        compiler_params=_params("arbitrary"),
        name="post_mlp",
    )(h, o, *[arr for arr, _ in params], *[c.stack for c in casts])
    return outs[:n_main], outs[n_main:]


def _rel_buckets_merged():
    r = np.arange(BLOCK)[:, None]
    q = np.arange(BLOCK)[None, :]
    n = np.where(r > q, q + BLOCK - r, q - r)
    max_exact = REL_BUCKETS // 2
    large = max_exact + (np.log(np.maximum(n, 1) / max_exact)
                         / np.log(REL_MAX_DIST / max_exact)
                         * (REL_BUCKETS - max_exact)).astype(np.int32)
    large = np.minimum(large, REL_BUCKETS - 1)
    return np.where(n < max_exact, n, large).astype(np.int32)


def _swa_kernel(rel_ref, sink_ref, bucket_ref, q_ref, kpp_ref, kpc_ref, vtp_ref, vtc_ref, o_ref,
                bias_ref, *, n_kv):
    n_q = bias_ref.shape[0]
    hd = SWA_HEAD_DIM
    pairs_per_kv = SWA_GROUP // 2
    neg2 = NEG * LOG2E

    @pl.when((pl.program_id(0) == 0) & (pl.program_id(1) == 0))
    def _():
        bucket = bucket_ref[...]

        def per_head(h, carry):
            def per_bucket(b, acc):
                return jnp.where(bucket == b, rel_ref[b, h] * LOG2E, acc)
            bias_ref[h] = lax.fori_loop(0, REL_BUCKETS, per_bucket,
                                        jnp.zeros(bucket.shape, F32))
            return carry

        lax.fori_loop(0, n_q, per_head, 0)

    key_r = lax.broadcasted_iota(jnp.int32, (BLOCK, BLOCK), 0)
    qry = lax.broadcasted_iota(jnp.int32, (BLOCK, BLOCK), 1)
    lower = key_r > qry
    seq_start = pl.program_id(1) == 0
    zeros = jnp.zeros((hd, 2 * BLOCK), BF16)
    for qb in range(SWA_QBLOCKS):
        rows = slice(qb * BLOCK, (qb + 1) * BLOCK)
        prev_rows = slice((qb - 1) * BLOCK, qb * BLOCK)
        for hkv in range(n_kv):
            def k_band(col):
                cols = slice(col * LANE, (col + 1) * LANE)
                prev = kpp_ref[:, cols] if qb == 0 else kpc_ref[prev_rows, cols]
                return [prev, kpc_ref[rows, cols]]
            k_stack = jnp.concatenate(k_band(2 * hkv) + k_band(2 * hkv + 1), axis=0)
            vrow = slice(hkv * hd, (hkv + 1) * hd)
            if qb == 0:
                vt_prev = jnp.where(seq_start, jnp.zeros((hd, BLOCK), BF16), vtp_ref[vrow, :])
                vt_band = jnp.concatenate([vt_prev, vtc_ref[vrow, :BLOCK]], axis=1)
            else:
                vt_band = vtc_ref[vrow, (qb - 1) * BLOCK:(qb + 1) * BLOCK]
            v_bd = jnp.concatenate([jnp.concatenate([vt_band, zeros], axis=1),
                                    jnp.concatenate([zeros, vt_band], axis=1)], axis=0)
            for pr in range(pairs_per_kv):
                pair = hkv * pairs_per_kv + pr
                cols = slice(pair * LANE, (pair + 1) * LANE)
                st = _dot_nt(k_stack, q_ref[rows, cols])
                ps, rden = [], []
                for e in range(2):
                    hq = 2 * pair + e
                    base = e * 2 * BLOCK
                    s = jnp.where(lower, st[base:base + BLOCK], st[base + BLOCK:base + 2 * BLOCK])
                    s = s + bias_ref[hq]
                    if qb == 0:
                        s = jnp.where(seq_start & lower, neg2, s)
                    sink = sink_ref[hq] * LOG2E
                    m = jnp.maximum(jnp.max(s, axis=0, keepdims=True), jnp.maximum(sink, neg2))
                    p = jnp.exp2(s - m)
                    pm = jnp.exp2(neg2 - m)
                    den = jnp.sum(p, axis=0, keepdims=True) + BLOCK * pm + jnp.exp2(sink - m)
                    pm_tile = jnp.broadcast_to(pm, p.shape)
                    ps += [jnp.where(lower, p, pm_tile).astype(BF16),
                           jnp.where(lower, pm_tile, p).astype(BF16)]
                    rden.append(jnp.broadcast_to(1.0 / den, (hd, BLOCK)))
                ot = _dot(v_bd, jnp.concatenate(ps, axis=0)) * jnp.concatenate(rden, axis=0)
                o_ref[rows, cols] = ot.T.astype(BF16)


def _swa_core(q, kpad, vt, rel_table, sinks, batch, seq):
    n, dq = q.shape
    n_q = dq // SWA_HEAD_DIM
    n_kv = n_q // SWA_GROUP
    rows = SWA_QBLOCKS * BLOCK
    steps = seq // rows
    q3 = q.reshape(batch, seq, dq)
    kp3 = kpad.reshape(batch, seq, kpad.shape[1])
    assert WINDOW == BLOCK
    bucket = jnp.asarray(_rel_buckets_merged())
    smem = pl.BlockSpec(memory_space=pltpu.SMEM)
    prev = lambda t: jnp.maximum(SWA_QBLOCKS * t - 1, 0)
    out = pl.pallas_call(
        functools.partial(_swa_kernel, n_kv=n_kv),
        grid=(batch, steps),
        in_specs=[smem, smem,
                  pl.BlockSpec(bucket.shape, lambda b, t: (0, 0)),
                  pl.BlockSpec((None, rows, dq), lambda b, t: (b, t, 0)),
                  pl.BlockSpec((None, BLOCK, kpad.shape[1]), lambda b, t: (b, prev(t), 0)),
                  pl.BlockSpec((None, rows, kpad.shape[1]), lambda b, t: (b, t, 0)),
                  pl.BlockSpec((vt.shape[0], BLOCK),
                               lambda b, t: (0, b * steps * SWA_QBLOCKS + prev(t))),
                  pl.BlockSpec((vt.shape[0], rows), lambda b, t: (0, b * steps + t))],
        out_specs=pl.BlockSpec((None, rows, dq), lambda b, t: (b, t, 0)),
        out_shape=jax.ShapeDtypeStruct((batch, seq, dq), BF16),
        scratch_shapes=[pltpu.VMEM((n_q,) + bucket.shape, F32)],
        compiler_params=_params("arbitrary", "arbitrary"),
        name="swa_core",
    )(rel_table, sinks, bucket, q3, kp3, kp3, vt, vt)
    return out.reshape(n, dq)


def kernel(x, a_w_in, a_w_gk2, a_b_gk, a_onorm, a_w_out, kv_norm, w_kv, b_w_q, b_sinks, b_w_out,
           rel_table, ln_mix, ln_mlp, w_up, w_down, ln_final):
    batch, seq, d = x.shape
    n_a = a_w_in.shape[0]
    n_b = b_w_q.shape[0]
    kd = a_w_gk2.shape[2]
    vd = (a_w_in.shape[2] - GLA_GATE_RANK - 2 * kd) // 2

    depth = n_a + n_b
    assert n_a >= 1
    half = w_kv.shape[1] // 2
    assert half == LANE and 2 * SWA_HEAD_DIM == LANE

    w_in_t = jnp.swapaxes(a_w_in, 1, 2)
    n_main = 2 * kd + 2 * vd

    w_glr = jnp.pad(w_in_t[:, n_main:, :], ((0, 0), (0, LANE - GLA_GATE_RANK), (0, 0))).astype(BF16)
    w_gk2 = jnp.pad(a_w_gk2, ((0, 0), (0, LANE - GLA_GATE_RANK), (0, 0))).astype(BF16)
    w_vt = w_kv[:, half:].T.astype(BF16)
    g_mix = ln_mix.reshape(depth, 1, d)
    g_mlp = ln_mlp.reshape(depth, 1, d)
    g_kv = kv_norm.reshape(1, d)
    g_final = ln_final.reshape(1, d)
    b_gk = a_b_gk.reshape(n_a, 1, kd)

    calls = []
    for layer in range(depth):
        is_gla = layer < n_a
        j = layer if is_gla else layer - n_a
        if is_gla:
            calls.append(("inproj", layer, {}))
        needs = {"w_out": (a_w_out if is_gla else b_w_out, j, d, d),
                 "w_up": (w_up, layer, d, w_up.shape[2]), "w_down": (w_down, layer, w_down.shape[1], d)}
        if n_a <= layer + 1 < depth:
            needs["w_q"] = (b_w_q, layer + 1 - n_a, d, b_w_q.shape[2])
        if layer + 1 == n_a:
            needs["w_k"] = (w_kv[None], 0, d, half)
        calls.append(("post", layer, needs))
    steps = batch * seq // ROW_TILE
    bf = {name: stack[idx, :rows, :cols].astype(BF16)
          for name, (stack, idx, rows, cols) in calls[0][2].items()}

    h = x.reshape(batch * seq, d)
    for c, (kind, layer, _) in enumerate(calls):
        is_gla = layer < n_a
        j = layer if is_gla else layer - n_a
        nxt = calls[c + 1][2] if c + 1 < len(calls) else {}
        casts = [_Cast(stack, idx, rows, cols, steps) for stack, idx, rows, cols in nxt.values()]
        if kind == "inproj":
            proj, el, cast_out = _gla_inproj(h, j, layer, g_mix, w_in_t, w_glr, w_gk2, b_gk,
                                             kd, vd, casts)
            o = _gla_core(proj, el, a_onorm[j].reshape(1, -1), batch, seq, kd, vd)
        else:
            if not is_gla:
                o = _swa_core(q, kpad, vt, rel_table, b_sinks[j], batch, seq)
            params = [(bf["w_out"], _resident(bf["w_out"].shape)), (g_mlp, _layer(g_mlp, layer)),
                      (bf["w_up"], pl.BlockSpec(memory_space=pl.ANY)),
                      (bf["w_down"], pl.BlockSpec(memory_space=pl.ANY))]
            if layer == depth - 1:
                (h,), cast_out = _post(h, o, params + [(g_final, _resident(g_final.shape))],
                                       "final", casts)
            elif layer + 1 < n_a:
                (h,), cast_out = _post(h, o, params, "none", casts)
            else:
                params += [(g_mix, _layer(g_mix, layer + 1)), (bf["w_q"], _resident(bf["w_q"].shape))]
                if layer + 1 == n_a:
                    params += [(g_kv, _resident(g_kv.shape)), (bf["w_k"], _resident(bf["w_k"].shape)),
                               (w_vt, _resident(w_vt.shape))]
                    (h, q, kpad, vt), cast_out = _post(h, o, params, "kvq", casts,
                                                       bf["w_q"].shape[1], w_vt.shape[0])
                else:
                    (h, q), cast_out = _post(h, o, params, "q", casts, bf["w_q"].shape[1])
        bf = dict(zip(nxt.keys(), cast_out))
    return h.reshape(batch, seq, d)
```
